```python
import math
import jax
import jax.numpy as jnp
from jax import lax
import numpy as np

D_MODEL = 2048
BATCH = 4
SEQ = 4096
DEPTH = 2

CTX_LEN = 256
GRID_W = 64
N_EVEN = (DEPTH + 1) // 2
N_ODD = DEPTH // 2
EPS = 1e-6
F32 = jnp.float32

HG_HEADS = 8
HG_DK = 128
HG_DV = 128
HG_KW = HG_HEADS * HG_DK
HG_VW = HG_HEADS * HG_DV
HG_CHUNK = 64

LRU_WIDTH = 1024
LRU_HEADS = 8
LRU_BLOCK = LRU_WIDTH // LRU_HEADS
LRU_CONV = 4
LRU_C = 8.0

MLA_HEADS = 8
MLA_Q_LORA = 512
MLA_KV_LORA = 512
MLA_NOPE = 128
MLA_ROPE = 64
MLA_V = 128
ROPE_THETA = 10000.0
ATTN_BLOCK = 128

SSD_WIDTH = 1024
SSD_HEADDIM = 64
SSD_HEADS = SSD_WIDTH // SSD_HEADDIM
SSD_GROUPS = 2
SSD_STATE = 128
SSD_BC = SSD_GROUPS * SSD_STATE
SSD_XBC = SSD_WIDTH + 2 * SSD_BC
SSD_CONV = 4
SSD_CHUNK = 128

N_EXPERTS = 16
EXPERT_FF = 1408
CAPACITY_FACTOR = 2

EV_OUT = HG_KW + HG_VW + LRU_WIDTH
EV_STATE = 2 * HG_KW + HG_VW + LRU_WIDTH
EV_IN = EV_OUT + EV_STATE
OD_OUT = MLA_Q_LORA + SSD_WIDTH
OD_STATE = MLA_KV_LORA + MLA_ROPE + SSD_XBC + 2 * SSD_HEADS
OD_IN = OD_OUT + OD_STATE
MIX_WIDTH = HG_VW + LRU_WIDTH

kernel_name = 'hybrid_dit_hgrn2_rglru_mla_ssd_ecmoe'


def rmsnorm(x, g):
    xf = x.astype(F32)
    y = xf * lax.rsqrt(jnp.mean(xf * xf, axis=-1, keepdims=True) + EPS)
    return (y * g.astype(F32)).astype(x.dtype)


def heads(t, h):
    return t.reshape(t.shape[0], t.shape[1], h, t.shape[-1] // h)


def flip(t):
    return jnp.flip(t, axis=1)


def dwconv(x, w, b):
    k = w.shape[0]
    y = lax.conv_general_dilated(x, w[:, None, :].astype(x.dtype), window_strides=(1,), padding=[(k // 2, k - 1 - k // 2)], dimension_numbers=('NWC', 'WIO', 'NWC'), feature_group_count=x.shape[-1])
    return y + b.astype(x.dtype)


def axial_rope(rows):
    row = jnp.repeat(jnp.arange(rows, dtype=F32), GRID_W)
    col = jnp.arange(rows * GRID_W) % GRID_W
    n_freq = MLA_ROPE // 4
    inv = ROPE_THETA ** (-jnp.arange(n_freq, dtype=F32) / n_freq)
    ang = jnp.concatenate([row[:, None] * inv, col.astype(F32)[:, None] * inv], axis=-1)
    return jnp.cos(ang), jnp.sin(ang)


def apply_rope(t, cos, sin):
    t1, t2 = jnp.split(t.astype(F32), 2, axis=-1)
    cs, sn = cos[None, :, None, :], sin[None, :, None, :]
    return jnp.concatenate([t1 * cs - t2 * sn, t1 * sn + t2 * cs], axis=-1).astype(t.dtype)


def segsum(x):
    t = x.shape[-1]
    xx = jnp.broadcast_to(x[..., :, None], x.shape + (t,))
    xx = jnp.where(jnp.tril(jnp.ones((t, t), bool), -1), xx, 0.0)
    cs = jnp.cumsum(xx, axis=-2)
    return jnp.where(jnp.tril(jnp.ones((t, t), bool)), cs, -jnp.inf)


def gla_chunked(q, k, v, logf, s0):
    b, t, h, _ = k.shape
    dv = v.shape[-1]
    n = t // HG_CHUNK
    blk = lambda a: a.reshape(b, n, HG_CHUNK, h, a.shape[-1])
    k, v, g = blk(k), blk(v), jnp.cumsum(blk(logf), axis=2)
    g_last = g[:, :, -1:]
    ds = jnp.einsum('bnmhd,bnmhe->bnhde', k * jnp.exp(g_last - g), v)
    decay = jnp.exp(g_last[:, :, 0])

    def step(s, inp):
        dec, d = inp
        return dec[..., None] * s + d, s

    s_fin, s_prev = lax.scan(step, s0, (jnp.moveaxis(decay, 1, 0), jnp.moveaxis(ds, 1, 0)))
    if q is None:
        return None, s_fin
    qg = blk(q) * jnp.exp(g)
    att = jnp.einsum('bnlhd,bnmhd->bnhlm', qg, k * jnp.exp(-g))
    att = jnp.where(jnp.tril(jnp.ones((HG_CHUNK, HG_CHUNK), bool)), att, 0.0)
    o = jnp.einsum('bnhlm,bnmhe->bnlhe', att, v) + jnp.einsum('bnlhd,nbhde->bnlhe', qg, s_prev)
    return o.reshape(b, t, h, dv), s_fin


def hgrn2_scan(q_c, f_c, i_c, q_l, f_l, i_l, lb):
    def gates(fr):
        f = lb + (1.0 - lb) * jax.nn.sigmoid(fr.astype(F32))
        return heads(jnp.log(f), HG_HEADS), heads(1.0 - f, HG_HEADS)

    s0 = jnp.zeros((f_l.shape[0], HG_HEADS, HG_DK, HG_DV), F32)
    lf, k = gates(f_c)
    o_c, s_c = gla_chunked(q_c, k, i_c, lf, s0)
    lf, k = gates(f_l)
    o_l, _ = gla_chunked(q_l, k, i_l, lf, s_c)
    return o_c, o_l


def linear_scan(a, b, h0):
    def comb(lft, rgt):
        return lft[0] * rgt[0], rgt[0] * lft[1] + rgt[1]

    a_cum, h = lax.associative_scan(comb, (a, b), axis=1)
    return h + a_cum * h0[:, None, :]


def rglru_scan(x_c, x_l, wa, ba, wx, bx, lam):
    def gates(xs):
        xh = heads(xs, LRU_HEADS)
        r = jax.nn.sigmoid(jnp.einsum('bthi,hij->bthj', xh, wa).reshape(xs.shape).astype(F32) + ba)
        ig = jax.nn.sigmoid(jnp.einsum('bthi,hij->bthj', xh, wx).reshape(xs.shape).astype(F32) + bx)
        log_a = -LRU_C * r * jax.nn.softplus(-lam.astype(F32))
        return jnp.exp(log_a), jnp.sqrt(-jnp.expm1(2.0 * log_a)) * ig * xs

    a, bb = gates(x_c)
    h_c = linear_scan(a, bb, jnp.zeros((x_c.shape[0], LRU_WIDTH), F32))
    a, bb = gates(x_l)
    h_l = linear_scan(a, bb, h_c[:, -1])
    return h_c, h_l


def even_mixer(u_c, u_l, lb_f, lb_b, onorm, conv_w, conv_b, wa, ba, wx, bx, lam, need_ctx):
    def split_state(u):
        f_f, f_b, i, lx = jnp.split(u, [HG_KW, 2 * HG_KW, 2 * HG_KW + HG_VW], axis=-1)
        return f_f, f_b, heads(i, HG_HEADS), lx

    def split_out(u):
        q, g, lg = jnp.split(u, [HG_KW, HG_KW + HG_VW], axis=-1)
        return heads(jax.nn.silu(q), HG_HEADS), g, lg

    q_l, g_l, lg_l = split_out(u_l[..., :EV_OUT])
    ff_l, fb_l, i_l, lx_l = split_state(u_l[..., EV_OUT:])
    if need_ctx:
        q_c, g_c, lg_c = split_out(u_c[..., :EV_OUT])
        u_c = u_c[..., EV_OUT:]
        q_cb = flip(q_c)
    else:
        q_c, q_cb = None, None
    ff_c, fb_c, i_c, lx_c = split_state(u_c)

    oc_f, ol_f = hgrn2_scan(q_c, ff_c, i_c, q_l, ff_l, i_l, lb_f)
    oc_b, ol_b = hgrn2_scan(q_cb, flip(fb_c), flip(i_c), flip(q_l), flip(fb_l), flip(i_l), lb_b)
    hg_gain = onorm.reshape(HG_HEADS, HG_DV)

    def hg_out(o, g):
        return rmsnorm(o, hg_gain).reshape(g.shape) * jax.nn.silu(g.astype(F32))

    x_c = dwconv(lx_c, conv_w, conv_b)
    x_l = dwconv(lx_l, conv_w, conv_b)
    hc_f, hl_f = rglru_scan(x_c, x_l, wa[0], ba[0], wx[0], bx[0], lam[0])
    hc_b, hl_b = rglru_scan(flip(x_c), flip(x_l), wa[1], ba[1], wx[1], bx[1], lam[1])

    y_l = jnp.concatenate([hg_out(ol_f + flip(ol_b), g_l), (hl_f + flip(hl_b)) * jax.nn.gelu(lg_l.astype(F32))], axis=-1).astype(u_l.dtype)
    if not need_ctx:
        return None, y_l
    y_c = jnp.concatenate([hg_out(oc_f + flip(oc_b), g_c), (hc_f + flip(hc_b)) * jax.nn.gelu(lg_c.astype(F32))], axis=-1).astype(u_l.dtype)
    return y_c, y_l


def block_attention(q, k, v):
    b, t, h, dq = q.shape
    nb = t // ATTN_BLOCK
    scale = dq ** -0.5
    qb = jnp.moveaxis(q.reshape(b, nb, ATTN_BLOCK, h, dq), 1, 0)

    def one(qblk):
        s = jnp.einsum('bqhd,bkhd->bhqk', qblk, k).astype(F32) * scale
        p = jax.nn.softmax(s, axis=-1).astype(v.dtype)
        return jnp.einsum('bhqk,bkhe->bqhe', p, v)

    o = lax.map(one, qb)
    return jnp.moveaxis(o, 0, 1).reshape(b, t, h, v.shape[-1])


def ssd_chunked(x, dt, a, bm, cm, s0, want_y):
    b, t, h, p = x.shape
    n_c = t // SSD_CHUNK
    rep = h // bm.shape[2]
    blk = lambda z: z.reshape(b, n_c, SSD_CHUNK, *z.shape[2:])
    xdt = blk(x * dt[..., None])
    bh = blk(jnp.repeat(bm, rep, axis=2))
    a_dt = jnp.moveaxis(blk(dt * a), 3, 1)
    a_cs = jnp.cumsum(a_dt, axis=-1)
    states = jnp.einsum('bclhn,bhcl,bclhp->bchpn', bh, jnp.exp(a_cs[..., -1:] - a_cs), xdt)
    states = jnp.concatenate([s0[:, None], states], axis=1)
    chunk_decay = jnp.exp(segsum(jnp.pad(a_cs[..., -1], ((0, 0), (0, 0), (1, 0)))))
    new_states = jnp.einsum('bhzc,bchpn->bzhpn', chunk_decay, states)
    if not want_y:
        return None, new_states[:, -1]
    ch = blk(jnp.repeat(cm, rep, axis=2))
    scores = jnp.einsum('bclhn,bcshn->bhcls', ch, bh) * jnp.exp(segsum(a_dt))
    y = jnp.einsum('bhcls,bcshp->bclhp', scores, xdt) + jnp.einsum('bclhn,bchpn,bhcl->bclhp', ch, new_states[:, :-1], jnp.exp(a_cs))
    return y.reshape(b, t, h, p), new_states[:, -1]


def odd_mixer(u_c, u_l, q_norm, w_q_up, kv_norm, w_kv_up, conv_w, conv_b, dt_bias, a_log, d_skip, onorm, cos, sin, need_ctx):
    def mla_q(lo, use_rope):
        qq = heads(rmsnorm(lo[..., :MLA_Q_LORA], q_norm) @ w_q_up, MLA_HEADS)
        if use_rope:
            qq = jnp.concatenate([qq[..., :MLA_NOPE], apply_rope(qq[..., MLA_NOPE:], cos, sin)], axis=-1)
        return qq

    def mla_kv(st, use_rope):
        kv = heads(rmsnorm(st[..., :MLA_KV_LORA], kv_norm) @ w_kv_up, MLA_HEADS)
        k_nope, v = kv[..., :MLA_NOPE], kv[..., MLA_NOPE:]
        k_r = st[..., MLA_KV_LORA:MLA_KV_LORA + MLA_ROPE][:, :, None, :]
        if use_rope:
            k_r = apply_rope(k_r, cos, sin)
        k_r = jnp.broadcast_to(k_r, k_nope.shape[:3] + (MLA_ROPE,))
        return jnp.concatenate([k_nope, k_r], axis=-1), v

    def ssd_in(st):
        o = MLA_KV_LORA + MLA_ROPE
        xbc = jax.nn.silu(dwconv(st[..., o:o + SSD_XBC], conv_w, conv_b))
        xs = heads(xbc[..., :SSD_WIDTH], SSD_HEADS)
        bm = heads(xbc[..., SSD_WIDTH:SSD_WIDTH + SSD_BC], SSD_GROUPS)
        cm = heads(xbc[..., SSD_WIDTH + SSD_BC:], SSD_GROUPS)
        return [xs, bm, cm, st[..., o + SSD_XBC:].astype(F32)]

    def ssd_dir(inp_c, inp_l, d):
        a = -jnp.exp(a_log[d].astype(F32))

        def run(inp, s0, want_y):
            xs, bm, cm, dtr = inp
            dt = jax.nn.softplus(dtr[..., d * SSD_HEADS:(d + 1) * SSD_HEADS] + dt_bias[d].astype(F32))
            return ssd_chunked(xs, dt, a, bm, cm, s0, want_y)

        s0 = jnp.zeros((inp_l[0].shape[0], SSD_HEADS, SSD_HEADDIM, SSD_STATE), F32)
        y_c, s_c = run(inp_c, s0, need_ctx)
        y_l, _ = run(inp_l, s_c, True)
        return y_c, y_l

    def ssd_out(y, xs, z):
        y = (y + d_skip.astype(F32)[:, None] * xs).reshape(z.shape) * jax.nn.silu(z.astype(F32))
        return rmsnorm(heads(y, SSD_GROUPS), onorm.reshape(SSD_GROUPS, -1)).reshape(z.shape)

    lo_l, st_l = u_l[..., :OD_OUT], u_l[..., OD_OUT:]
    st_c = u_c[..., OD_OUT:] if need_ctx else u_c
    b, s = u_l.shape[:2]

    k_c, v_c = mla_kv(st_c, False)
    k_l, v_l = mla_kv(st_l, True)
    att_l = block_attention(mla_q(lo_l, True), jnp.concatenate([k_c, k_l], axis=1), jnp.concatenate([v_c, v_l], axis=1))

    in_c, in_l = ssd_in(st_c), ssd_in(st_l)
    yc_f, yl_f = ssd_dir(in_c, in_l, 0)
    yc_b, yl_b = ssd_dir([flip(t) for t in in_c], [flip(t) for t in in_l], 1)

    y_l = jnp.concatenate([att_l.reshape(b, s, -1), ssd_out(yl_f + flip(yl_b), in_l[0], lo_l[..., MLA_Q_LORA:])], axis=-1).astype(u_l.dtype)
    if not need_ctx:
        return None, y_l
    lo_c = u_c[..., :OD_OUT]
    att_c = block_attention(mla_q(lo_c, False), k_c, v_c)
    y_c = jnp.concatenate([att_c.reshape(b, u_c.shape[1], -1), ssd_out(yc_f + flip(yc_b), in_c[0], lo_c[..., MLA_Q_LORA:])], axis=-1).astype(u_l.dtype)
    return y_c, y_l


def ec_moe(h, w_router, w_gate, w_up, w_down):
    b, t, _ = h.shape
    cap = CAPACITY_FACTOR * t // N_EXPERTS
    aff = jax.nn.softmax((h @ w_router).astype(F32), axis=-1)
    gate, idx = lax.top_k(jnp.swapaxes(aff, 1, 2), cap)
    bidx = jnp.arange(b)[:, None, None]
    xe = h[bidx, idx]
    hid = jax.nn.silu(jnp.einsum('becd,edf->becf', xe, w_gate)) * jnp.einsum('becd,edf->becf', xe, w_up)
    ye = jnp.einsum('becf,efd->becd', hid, w_down) * gate[..., None].astype(h.dtype)
    return jnp.zeros_like(h).at[bidx, idx].add(ye)


def setup_inputs(seed: int = 0) -> dict:
    key = jax.random.key(seed)
    ks = iter(jax.random.split(key, 48))
    D = D_MODEL

    def nrm(shape, scale):
        return jax.random.normal(next(ks), shape, F32) * scale

    def gain(shape):
        return 1.0 + nrm(shape, 0.01)

    a0 = jax.random.uniform(next(ks), (N_EVEN, 2, LRU_WIDTH), F32, minval=0.9, maxval=0.999)
    s_lam = a0 ** (1.0 / LRU_C)
    lru_lambda = jnp.log(s_lam) - jnp.log1p(-s_lam)
    dt0 = jnp.exp(jax.random.uniform(next(ks), (N_ODD, 2, SSD_HEADS), F32, minval=math.log(1e-3), maxval=math.log(1e-1)))
    ssd_dt_bias = dt0 + jnp.log(-jnp.expm1(-dt0))
    ssd_a_log = jnp.log(jax.random.uniform(next(ks), (N_ODD, 2, SSD_HEADS), F32, minval=1.0, maxval=16.0))
    return {
        'x': nrm((BATCH, SEQ, D), 1.0),
        'c': nrm((BATCH, D), 1.0),
        'ctx': nrm((BATCH, CTX_LEN, D), 1.0),
        'c_ctx': nrm((D,), 1.0),
        'w_mod': nrm((DEPTH, D, 6 * D), 0.5 * D ** -0.5),
        'b_mod': nrm((DEPTH, 6 * D), 0.01),
        'norm_mix': gain((DEPTH, D)),
        'norm_ffn': gain((DEPTH, D)),
        'w_out': nrm((DEPTH, MIX_WIDTH, D), MIX_WIDTH ** -0.5),
        'ev_w_in': nrm((N_EVEN, D, EV_IN), D ** -0.5),
        'hg_lb': nrm((2, N_EVEN + 1, HG_KW), 0.1),
        'hg_onorm': gain((N_EVEN, HG_VW)),
        'lru_conv_w': nrm((N_EVEN, LRU_CONV, LRU_WIDTH), LRU_CONV ** -0.5),
        'lru_conv_b': nrm((N_EVEN, LRU_WIDTH), 0.01),
        'lru_wa': nrm((N_EVEN, 2, LRU_HEADS, LRU_BLOCK, LRU_BLOCK), LRU_BLOCK ** -0.5),
        'lru_ba': nrm((N_EVEN, 2, LRU_WIDTH), 0.01),
        'lru_wx': nrm((N_EVEN, 2, LRU_HEADS, LRU_BLOCK, LRU_BLOCK), LRU_BLOCK ** -0.5),
        'lru_bx': nrm((N_EVEN, 2, LRU_WIDTH), 0.01),
        'lru_lambda': lru_lambda,
        'od_w_in': nrm((N_ODD, D, OD_IN), D ** -0.5),
        'mla_q_norm': gain((N_ODD, MLA_Q_LORA)),
        'mla_w_q_up': nrm((N_ODD, MLA_Q_LORA, MLA_HEADS * (MLA_NOPE + MLA_ROPE)), MLA_Q_LORA ** -0.5),
        'mla_kv_norm': gain((N_ODD, MLA_KV_LORA)),
        'mla_w_kv_up': nrm((N_ODD, MLA_KV_LORA, MLA_HEADS * (MLA_NOPE + MLA_V)), MLA_KV_LORA ** -0.5),
        'ssd_conv_w': nrm((N_ODD, SSD_CONV, SSD_XBC), SSD_CONV ** -0.5),
        'ssd_conv_b': nrm((N_ODD, SSD_XBC), 0.01),
        'ssd_dt_bias': ssd_dt_bias,
        'ssd_a_log': ssd_a_log,
        'ssd_d': 1.0 + nrm((N_ODD, SSD_HEADS), 0.1),
        'ssd_onorm': gain((N_ODD, SSD_WIDTH)),
        'moe_router': nrm((DEPTH, D, N_EXPERTS), D ** -0.5),
        'moe_w_gate': nrm((DEPTH, N_EXPERTS, D, EXPERT_FF), D ** -0.5),
        'moe_w_up': nrm((DEPTH, N_EXPERTS, D, EXPERT_FF), D ** -0.5),
        'moe_w_down': nrm((DEPTH, N_EXPERTS, EXPERT_FF, D), EXPERT_FF ** -0.5),
        'final_norm': gain((D,)),
    }


def reference(x, c, ctx, c_ctx, w_mod, b_mod, norm_mix, norm_ffn, w_out, ev_w_in, hg_lb, hg_onorm, lru_conv_w, lru_conv_b, lru_wa, lru_ba, lru_wx, lru_bx, lru_lambda, od_w_in, mla_q_norm, mla_w_q_up, mla_kv_norm, mla_w_kv_up, ssd_conv_w, ssd_conv_b, ssd_dt_bias, ssd_a_log, ssd_d, ssd_onorm, moe_router, moe_w_gate, moe_w_up, moe_w_down, final_norm):
    rows = x.shape[1] // GRID_W
    cos, sin = axial_rope(rows)
    lb_all = jnp.cumsum(jax.nn.softmax(hg_lb.astype(F32), axis=1), axis=1)
    for l in range(DEPTH):
        need_ctx = l < DEPTH - 1
        j = l // 2
        mod = jax.nn.silu(c) @ w_mod[l] + b_mod[l]
        mod_c = jax.nn.silu(c_ctx) @ w_mod[l] + b_mod[l]
        sh1, sc1, g1, sh2, sc2, g2 = jnp.split(mod[:, None, :], 6, axis=-1)
        csh1, csc1, cg1, csh2, csc2, cg2 = jnp.split(mod_c, 6, axis=-1)
        h_l = rmsnorm(x, norm_mix[l]) * (1.0 + sc1) + sh1
        h_c = rmsnorm(ctx, norm_mix[l]) * (1.0 + csc1) + csh1
        if l % 2 == 0:
            w_in = ev_w_in[j]
            u_c = h_c @ (w_in if need_ctx else w_in[:, EV_OUT:])
            o_c, o_l = even_mixer(u_c, h_l @ w_in, lb_all[0, j], lb_all[1, j], hg_onorm[j], lru_conv_w[j], lru_conv_b[j], lru_wa[j], lru_ba[j], lru_wx[j], lru_bx[j], lru_lambda[j], need_ctx)
        else:
            w_in = od_w_in[j]
            u_c = h_c @ (w_in if need_ctx else w_in[:, OD_OUT:])
            o_c, o_l = odd_mixer(u_c, h_l @ w_in, mla_q_norm[j], mla_w_q_up[j], mla_kv_norm[j], mla_w_kv_up[j], ssd_conv_w[j], ssd_conv_b[j], ssd_dt_bias[j], ssd_a_log[j], ssd_d[j], ssd_onorm[j], cos, sin, need_ctx)
        x = x + g1 * (o_l @ w_out[l])
        x = x + g2 * ec_moe(rmsnorm(x, norm_ffn[l]) * (1.0 + sc2) + sh2, moe_router[l], moe_w_gate[l], moe_w_up[l], moe_w_down[l])
        if need_ctx:
            ctx = ctx + cg1 * (o_c @ w_out[l])
            ctx = ctx + cg2 * ec_moe(rmsnorm(ctx, norm_ffn[l]) * (1.0 + csc2) + csh2, moe_router[l], moe_w_gate[l], moe_w_up[l], moe_w_down[l])
    return rmsnorm(x, final_norm)
```

```python
import functools
import math

import jax
import jax.numpy as jnp
from jax import lax
from jax.experimental import pallas as pl
from jax.experimental.pallas import tpu as pltpu

D_MODEL = 2048
DEPTH = 2
CTX_LEN = 256
GRID_W = 64
EPS = 1e-6
F32 = jnp.float32
BF16 = jnp.bfloat16

HG_HEADS = 8
HG_DK = 128
HG_DV = 128
HG_KW = HG_HEADS * HG_DK
HG_VW = HG_HEADS * HG_DV
HG_CHUNK = 64

LRU_WIDTH = 1024
LRU_HEADS = 8
LRU_C = 8.0

MLA_HEADS = 8
MLA_Q_LORA = 512
MLA_KV_LORA = 512
MLA_NOPE = 128
MLA_ROPE = 64
MLA_V = 128
ROPE_THETA = 10000.0
ATTN_BLOCK = 128

SSD_WIDTH = 1024
SSD_HEADDIM = 64
SSD_HEADS = SSD_WIDTH // SSD_HEADDIM
SSD_GROUPS = 2
SSD_STATE = 128
SSD_BC = SSD_GROUPS * SSD_STATE
SSD_XBC = SSD_WIDTH + 2 * SSD_BC
SSD_CHUNK = 128

N_EXPERTS = 16
CAPACITY_FACTOR = 2

EV_OUT = HG_KW + HG_VW + LRU_WIDTH
OD_OUT = MLA_Q_LORA + SSD_WIDTH

V7X_VMEM_LIMIT_BYTES = 56 * 1024 * 1024


def _mm_kernel(a_ref, b_ref, o_ref):
    o_ref[0] = jnp.dot(a_ref[0].astype(BF16), b_ref[0].astype(BF16), preferred_element_type=F32)


def _pick_tile(n, target):
    if n <= target:
        return n
    t = target
    while n % t:
        t //= 2
    return t


def _lane_tile(n, target=1536):
    k = n // 128
    return 128 * max(d for d in range(1, k + 1) if k % d == 0 and 128 * d <= target)


def _bmm(a, b, tm=512):
    g, m, k = a.shape
    gb, _, n = b.shape
    tm = _pick_tile(m, tm)
    n_real = n
    cands = [(-(-n // 128) + i) * 128 for i in range(4)]
    n = next((c for c in cands if _lane_tile(c) >= 1024), max(cands, key=_lane_tile))
    if n != n_real:
        b = jnp.pad(b, ((0, 0), (0, 0), (0, n - n_real)))
    tn = _lane_tile(n)
    out = _bmm_call(a, b, g, gb, m, k, n, tm, tn)
    return out[..., :n_real] if n != n_real else out


def _bmm_call(a, b, g, gb, m, k, n, tm, tn):
    return pl.pallas_call(
        _mm_kernel,
        grid=(g, m // tm, n // tn),
        in_specs=[
            pl.BlockSpec((1, tm, k), lambda gi, i, j: (gi, i, 0)),
            pl.BlockSpec((1, k, tn), lambda gi, i, j: (gi % gb, 0, j)),
        ],
        out_specs=pl.BlockSpec((1, tm, tn), lambda gi, i, j: (gi, i, j)),
        out_shape=jax.ShapeDtypeStruct((g, m, n), F32),
        compiler_params=pltpu.CompilerParams(
            dimension_semantics=("parallel", "parallel", "arbitrary"),
            vmem_limit_bytes=V7X_VMEM_LIMIT_BYTES),
    )(a, b)


def _mm(a, b):
    lead = a.shape[:-1]
    m = math.prod(lead)
    out = _bmm(a.reshape(1, m, a.shape[-1]), b[None].astype(BF16))
    return out.reshape(*lead, b.shape[-1])


LRU_BLOCK = LRU_WIDTH // LRU_HEADS
LRU_L = 128
SUBLANES = 8
LRU_PAD = SUBLANES


def _lru_kernel(seq, xc_ref, gc_ref, xl_ref, gl_ref, cw_ref, cb_ref, wa_ref, wx_ref, ba_ref, bx_ref, lam_ref,
                oc_ref, ol_ref, xs_ref, hf_ref):
    t_c, t_l = seq
    n_c, n_l = t_c // LRU_L, t_l // LRU_L
    off_c, off_l = LRU_PAD, 2 * LRU_PAD + t_c
    zeros = jnp.zeros((LRU_PAD, LRU_BLOCK), F32)
    xs_ref[0:LRU_PAD] = zeros
    xs_ref[off_c + t_c:off_l] = zeros
    xs_ref[off_l + t_l:off_l + t_l + LRU_PAD] = zeros
    xs_ref[off_c:off_c + t_c] = xc_ref[0]
    xs_ref[off_l:off_l + t_l] = xl_ref[0]

    cw = cw_ref[...]
    cb = cb_ref[...]
    row = lax.broadcasted_iota(jnp.int32, (LRU_L // SUBLANES, SUBLANES, LRU_BLOCK), 1)

    def gates(r0, d):
        win = xs_ref[pl.ds(r0 - LRU_PAD, LRU_L + 2 * LRU_PAD), :]
        x = cb
        for k in range(4):
            x = x + cw[k:k + 1] * win[LRU_PAD - 2 + k:LRU_PAD - 2 + k + LRU_L]
        xb = x.astype(BF16)
        r = jax.nn.sigmoid(jnp.dot(xb, wa_ref[d, 0].astype(BF16), preferred_element_type=F32) + ba_ref[d:d + 1])
        ig = jax.nn.sigmoid(jnp.dot(xb, wx_ref[d, 0].astype(BF16), preferred_element_type=F32) + bx_ref[d:d + 1])
        log_a = -LRU_C * r * jax.nn.softplus(-lam_ref[d:d + 1])
        a = jnp.exp(log_a)
        return a, jnp.sqrt(1.0 - a * a) * ig * x

    def chunk_scan(a, b, carry, reverse):
        n = LRU_L // SUBLANES
        a = a.reshape(n, SUBLANES, LRU_BLOCK)
        b = b.reshape(n, SUBLANES, LRU_BLOCK)
        for s in (1, 2, 4):
            if reverse:
                keep = row < SUBLANES - s
                shift = SUBLANES - s
            else:
                keep = row >= s
                shift = s
            a_n = jnp.where(keep, pltpu.roll(a, shift, 1), 1.0)
            b_n = jnp.where(keep, pltpu.roll(b, shift, 1), 0.0)
            b = a * b_n + b
            a = a * a_n
        outs = [None] * n
        last = 0 if reverse else SUBLANES - 1
        for j in (range(n - 1, -1, -1) if reverse else range(n)):
            h = b[j] + a[j] * carry
            carry = h[last:last + 1]
            outs[j] = h
        return jnp.concatenate(outs, axis=0), carry

    def fwd_chunk(t0, r0, carry):
        a, b = gates(r0, 0)
        h, carry = chunk_scan(a, b, carry, False)
        hf_ref[pl.ds(t0, LRU_L), :] = h
        return carry

    def bwd_chunk(t0, r0, carry):
        a, b = gates(r0, 1)
        h, carry = chunk_scan(a, b, carry, True)
        return hf_ref[pl.ds(t0, LRU_L), :] + h, carry

    carry = jnp.zeros((1, LRU_BLOCK), F32)
    for c in range(n_c):
        carry = fwd_chunk(c * LRU_L, off_c + c * LRU_L, carry)

    def fwd_body(c, carry):
        t0 = pl.multiple_of(c * LRU_L, LRU_L)
        return fwd_chunk(t_c + t0, off_l + t0, carry)

    lax.fori_loop(0, n_l, fwd_body, carry)

    carry = jnp.zeros((1, LRU_BLOCK), F32)
    for c in range(n_c - 1, -1, -1):
        h, carry = bwd_chunk(c * LRU_L, off_c + c * LRU_L, carry)
        oc_ref[0, c * LRU_L:(c + 1) * LRU_L, :] = h * jax.nn.gelu(gc_ref[0, c * LRU_L:(c + 1) * LRU_L, :])

    def bwd_body(i, carry):
        t0 = pl.multiple_of((n_l - 1 - i) * LRU_L, LRU_L)
        h, carry = bwd_chunk(t_c + t0, off_l + t0, carry)
        ol_ref[0, pl.ds(t0, LRU_L), :] = h * jax.nn.gelu(gl_ref[0, pl.ds(t0, LRU_L), :])
        return carry

    lax.fori_loop(0, n_l, bwd_body, carry)


def _rglru(u_c, u_l, conv_w, conv_b, wa, ba, wx, bx, lam):
    bsz, t_c, _ = u_c.shape
    t_l = u_l.shape[1]
    x_blk = (EV_OUT + 2 * HG_KW + HG_VW) // LRU_BLOCK
    g_blk = (HG_KW + HG_VW) // LRU_BLOCK
    t_pad = t_c + t_l + 3 * LRU_PAD
    vec = lambda: pl.BlockSpec((2, LRU_BLOCK), lambda b, h: (0, h))
    return pl.pallas_call(
        functools.partial(_lru_kernel, (t_c, t_l)),
        grid=(bsz, LRU_HEADS),
        in_specs=[
            pl.BlockSpec((1, t_c, LRU_BLOCK), lambda b, h: (b, 0, x_blk + h)),
            pl.BlockSpec((1, t_c, LRU_BLOCK), lambda b, h: (b, 0, g_blk + h)),
            pl.BlockSpec((1, t_l, LRU_BLOCK), lambda b, h: (b, 0, x_blk + h)),
            pl.BlockSpec((1, t_l, LRU_BLOCK), lambda b, h: (b, 0, g_blk + h)),
            pl.BlockSpec((4, LRU_BLOCK), lambda b, h: (0, h)),
            pl.BlockSpec((1, LRU_BLOCK), lambda b, h: (0, h)),
            pl.BlockSpec((2, 1, LRU_BLOCK, LRU_BLOCK), lambda b, h: (0, h, 0, 0)),
            pl.BlockSpec((2, 1, LRU_BLOCK, LRU_BLOCK), lambda b, h: (0, h, 0, 0)),
            vec(), vec(), vec(),
        ],
        out_specs=[
            pl.BlockSpec((1, t_c, LRU_BLOCK), lambda b, h: (b, 0, h)),
            pl.BlockSpec((1, t_l, LRU_BLOCK), lambda b, h: (b, 0, h)),
        ],
        out_shape=[jax.ShapeDtypeStruct((bsz, t_c, LRU_WIDTH), F32), jax.ShapeDtypeStruct((bsz, t_l, LRU_WIDTH), F32)],
        scratch_shapes=[pltpu.VMEM((t_pad, LRU_BLOCK), F32), pltpu.VMEM((t_c + t_l, LRU_BLOCK), F32)],
        compiler_params=pltpu.CompilerParams(
            dimension_semantics=("parallel", "parallel"), vmem_limit_bytes=V7X_VMEM_LIMIT_BYTES),
        name="rglru",
    )(u_c, u_c, u_l, u_l, conv_w, conv_b[None], wa, wx, ba, bx, lam)


def rmsnorm(x, g):
    xf = x.astype(F32)
    y = xf * lax.rsqrt(jnp.mean(xf * xf, axis=-1, keepdims=True) + EPS)
    return (y * g.astype(F32)).astype(x.dtype)


def heads(t, h):
    return t.reshape(t.shape[0], t.shape[1], h, t.shape[-1] // h)


def flip(t):
    return jnp.flip(t, axis=1)


def dwconv(x, w, b):
    k = w.shape[0]
    y = lax.conv_general_dilated(x, w[:, None, :].astype(x.dtype), window_strides=(1,), padding=[(k // 2, k - 1 - k // 2)], dimension_numbers=('NWC', 'WIO', 'NWC'), feature_group_count=x.shape[-1])
    return y + b.astype(x.dtype)


def axial_rope(rows):
    row = jnp.repeat(jnp.arange(rows, dtype=F32), GRID_W)
    col = jnp.arange(rows * GRID_W) % GRID_W
    n_freq = MLA_ROPE // 4
    inv = ROPE_THETA ** (-jnp.arange(n_freq, dtype=F32) / n_freq)
    ang = jnp.concatenate([row[:, None] * inv, col.astype(F32)[:, None] * inv], axis=-1)
    return jnp.cos(ang), jnp.sin(ang)


def apply_rope(t, cos, sin):
    t1, t2 = jnp.split(t.astype(F32), 2, axis=-1)
    cs, sn = cos[None, :, None, :], sin[None, :, None, :]
    return jnp.concatenate([t1 * cs - t2 * sn, t1 * sn + t2 * cs], axis=-1).astype(t.dtype)


def segsum(x):
    t = x.shape[-1]
    xx = jnp.broadcast_to(x[..., :, None], x.shape + (t,))
    xx = jnp.where(jnp.tril(jnp.ones((t, t), bool), -1), xx, 0.0)
    cs = jnp.cumsum(xx, axis=-2)
    return jnp.where(jnp.tril(jnp.ones((t, t), bool)), cs, -jnp.inf)


def gla_chunked(q, k, v, logf, s0):
    b, t, h, _ = k.shape
    dv = v.shape[-1]
    n = t // HG_CHUNK
    blk = lambda a: a.reshape(b, n, HG_CHUNK, h, a.shape[-1])
    k, v, g = blk(k), blk(v), jnp.cumsum(blk(logf), axis=2)
    g_last = g[:, :, -1:]
    ds = jnp.einsum('bnmhd,bnmhe->bnhde', k * jnp.exp(g_last - g), v)
    decay = jnp.exp(g_last[:, :, 0])

    def step(s, inp):
        dec, d = inp
        return dec[..., None] * s + d, s

    s_fin, s_prev = lax.scan(step, s0, (jnp.moveaxis(decay, 1, 0), jnp.moveaxis(ds, 1, 0)))
    if q is None:
        return None, s_fin
    qg = blk(q) * jnp.exp(g)
    att = jnp.einsum('bnlhd,bnmhd->bnhlm', qg, k * jnp.exp(-g))
    att = jnp.where(jnp.tril(jnp.ones((HG_CHUNK, HG_CHUNK), bool)), att, 0.0)
    o = jnp.einsum('bnhlm,bnmhe->bnlhe', att, v) + jnp.einsum('bnlhd,nbhde->bnlhe', qg, s_prev)
    return o.reshape(b, t, h, dv), s_fin


def hgrn2_scan(q_c, f_c, i_c, q_l, f_l, i_l, lb):
    def gates(fr):
        f = lb + (1.0 - lb) * jax.nn.sigmoid(fr.astype(F32))
        return heads(jnp.log(f), HG_HEADS), heads(1.0 - f, HG_HEADS)

    s0 = jnp.zeros((f_l.shape[0], HG_HEADS, HG_DK, HG_DV), F32)
    lf, k = gates(f_c)
    o_c, s_c = gla_chunked(q_c, k, i_c, lf, s0)
    lf, k = gates(f_l)
    o_l, _ = gla_chunked(q_l, k, i_l, lf, s_c)
    return o_c, o_l


def even_mixer(u_c, u_l, lb_f, lb_b, onorm, conv_w, conv_b, wa, ba, wx, bx, lam, need_ctx):
    def split_state(u):
        f_f, f_b, i, lx = jnp.split(u, [HG_KW, 2 * HG_KW, 2 * HG_KW + HG_VW], axis=-1)
        return f_f, f_b, heads(i, HG_HEADS), lx

    def split_out(u):
        q, g, lg = jnp.split(u, [HG_KW, HG_KW + HG_VW], axis=-1)
        return heads(jax.nn.silu(q), HG_HEADS), g, lg

    assert need_ctx
    lru_c, lru_l = _rglru(u_c, u_l, conv_w, conv_b, wa, ba, wx, bx, lam)
    q_l, g_l, lg_l = split_out(u_l[..., :EV_OUT])
    ff_l, fb_l, i_l, lx_l = split_state(u_l[..., EV_OUT:])
    q_c, g_c, lg_c = split_out(u_c[..., :EV_OUT])
    q_cb = flip(q_c)
    ff_c, fb_c, i_c, lx_c = split_state(u_c[..., EV_OUT:])

    oc_f, ol_f = hgrn2_scan(q_c, ff_c, i_c, q_l, ff_l, i_l, lb_f)
    oc_b, ol_b = hgrn2_scan(q_cb, flip(fb_c), flip(i_c), flip(q_l), flip(fb_l), flip(i_l), lb_b)
    hg_gain = onorm.reshape(HG_HEADS, HG_DV)

    def hg_out(o, g):
        return rmsnorm(o, hg_gain).reshape(g.shape) * jax.nn.silu(g.astype(F32))

    y_l = jnp.concatenate([hg_out(ol_f + flip(ol_b), g_l), lru_l], axis=-1).astype(u_l.dtype)
    y_c = jnp.concatenate([hg_out(oc_f + flip(oc_b), g_c), lru_c], axis=-1).astype(u_l.dtype)
    return y_c, y_l


def block_attention(q, k, v):
    b, t, h, dq = q.shape
    nb = t // ATTN_BLOCK
    scale = dq ** -0.5
    qb = jnp.moveaxis(q.reshape(b, nb, ATTN_BLOCK, h, dq), 1, 0)

    def one(qblk):
        s = jnp.einsum('bqhd,bkhd->bhqk', qblk, k).astype(F32) * scale
        p = jax.nn.softmax(s, axis=-1).astype(v.dtype)
        return jnp.einsum('bhqk,bkhe->bqhe', p, v)

    o = lax.map(one, qb)
    return jnp.moveaxis(o, 0, 1).reshape(b, t, h, v.shape[-1])


def ssd_chunked(x, dt, a, bm, cm, s0, want_y):
    b, t, h, p = x.shape
    n_c = t // SSD_CHUNK
    rep = h // bm.shape[2]
    blk = lambda z: z.reshape(b, n_c, SSD_CHUNK, *z.shape[2:])
    xdt = blk(x * dt[..., None])
    bh = blk(jnp.repeat(bm, rep, axis=2))
    a_dt = jnp.moveaxis(blk(dt * a), 3, 1)
    a_cs = jnp.cumsum(a_dt, axis=-1)
    states = jnp.einsum('bclhn,bhcl,bclhp->bchpn', bh, jnp.exp(a_cs[..., -1:] - a_cs), xdt)
    states = jnp.concatenate([s0[:, None], states], axis=1)
    chunk_decay = jnp.exp(segsum(jnp.pad(a_cs[..., -1], ((0, 0), (0, 0), (1, 0)))))
    new_states = jnp.einsum('bhzc,bchpn->bzhpn', chunk_decay, states)
    if not want_y:
        return None, new_states[:, -1]
    ch = blk(jnp.repeat(cm, rep, axis=2))
    scores = jnp.einsum('bclhn,bcshn->bhcls', ch, bh) * jnp.exp(segsum(a_dt))
    y = jnp.einsum('bhcls,bcshp->bclhp', scores, xdt) + jnp.einsum('bclhn,bchpn,bhcl->bclhp', ch, new_states[:, :-1], jnp.exp(a_cs))
    return y.reshape(b, t, h, p), new_states[:, -1]


def odd_mixer(u_c, u_l, q_norm, w_q_up, kv_norm, w_kv_up, conv_w, conv_b, dt_bias, a_log, d_skip, onorm, cos, sin, need_ctx):
    def mla_q(lo, use_rope):
        qq = heads(_mm(rmsnorm(lo[..., :MLA_Q_LORA], q_norm), w_q_up), MLA_HEADS)
        if use_rope:
            qq = jnp.concatenate([qq[..., :MLA_NOPE], apply_rope(qq[..., MLA_NOPE:], cos, sin)], axis=-1)
        return qq

    def mla_kv(st, use_rope):
        kv = heads(_mm(rmsnorm(st[..., :MLA_KV_LORA], kv_norm), w_kv_up), MLA_HEADS)
        k_nope, v = kv[..., :MLA_NOPE], kv[..., MLA_NOPE:]
        k_r = st[..., MLA_KV_LORA:MLA_KV_LORA + MLA_ROPE][:, :, None, :]
        if use_rope:
            k_r = apply_rope(k_r, cos, sin)
        k_r = jnp.broadcast_to(k_r, k_nope.shape[:3] + (MLA_ROPE,))
        return jnp.concatenate([k_nope, k_r], axis=-1), v

    def ssd_in(st):
        o = MLA_KV_LORA + MLA_ROPE
        xbc = jax.nn.silu(dwconv(st[..., o:o + SSD_XBC], conv_w, conv_b))
        xs = heads(xbc[..., :SSD_WIDTH], SSD_HEADS)
        bm = heads(xbc[..., SSD_WIDTH:SSD_WIDTH + SSD_BC], SSD_GROUPS)
        cm = heads(xbc[..., SSD_WIDTH + SSD_BC:], SSD_GROUPS)
        return [xs, bm, cm, st[..., o + SSD_XBC:].astype(F32)]

    def ssd_dir(inp_c, inp_l, d):
        a = -jnp.exp(a_log[d].astype(F32))

        def run(inp, s0, want_y):
            xs, bm, cm, dtr = inp
            dt = jax.nn.softplus(dtr[..., d * SSD_HEADS:(d + 1) * SSD_HEADS] + dt_bias[d].astype(F32))
            return ssd_chunked(xs, dt, a, bm, cm, s0, want_y)

        s0 = jnp.zeros((inp_l[0].shape[0], SSD_HEADS, SSD_HEADDIM, SSD_STATE), F32)
        y_c, s_c = run(inp_c, s0, need_ctx)
        y_l, _ = run(inp_l, s_c, True)
        return y_c, y_l

    def ssd_out(y, xs, z):
        y = (y + d_skip.astype(F32)[:, None] * xs).reshape(z.shape) * jax.nn.silu(z.astype(F32))
        return rmsnorm(heads(y, SSD_GROUPS), onorm.reshape(SSD_GROUPS, -1)).reshape(z.shape)

    lo_l, st_l = u_l[..., :OD_OUT], u_l[..., OD_OUT:]
    st_c = u_c[..., OD_OUT:] if need_ctx else u_c
    b, s = u_l.shape[:2]

    k_c, v_c = mla_kv(st_c, False)
    k_l, v_l = mla_kv(st_l, True)
    att_l = block_attention(mla_q(lo_l, True), jnp.concatenate([k_c, k_l], axis=1), jnp.concatenate([v_c, v_l], axis=1))

    in_c, in_l = ssd_in(st_c), ssd_in(st_l)
    yc_f, yl_f = ssd_dir(in_c, in_l, 0)
    yc_b, yl_b = ssd_dir([flip(t) for t in in_c], [flip(t) for t in in_l], 1)

    y_l = jnp.concatenate([att_l.reshape(b, s, -1), ssd_out(yl_f + flip(yl_b), in_l[0], lo_l[..., MLA_Q_LORA:])], axis=-1).astype(u_l.dtype)
    if not need_ctx:
        return None, y_l
    lo_c = u_c[..., :OD_OUT]
    att_c = block_attention(mla_q(lo_c, False), k_c, v_c)
    y_c = jnp.concatenate([att_c.reshape(b, u_c.shape[1], -1), ssd_out(yc_f + flip(yc_b), in_c[0], lo_c[..., MLA_Q_LORA:])], axis=-1).astype(u_l.dtype)
    return y_c, y_l


def ec_moe(h, w_router, w_gate, w_up, w_down):
    b, t, d = h.shape
    cap = CAPACITY_FACTOR * t // N_EXPERTS
    aff = jax.nn.softmax((h @ w_router).astype(F32), axis=-1)
    gate, idx = lax.top_k(jnp.swapaxes(aff, 1, 2), cap)
    bidx = jnp.arange(b)[:, None, None]
    xe = h[bidx, idx].reshape(b * N_EXPERTS, cap, d)
    hid = jax.nn.silu(_bmm(xe, w_gate)) * _bmm(xe, w_up)
    ye = _bmm(hid, w_down).reshape(b, N_EXPERTS, cap, d) * gate[..., None].astype(h.dtype)
    return jnp.zeros_like(h).at[bidx, idx].add(ye)


def kernel(x, c, ctx, c_ctx, w_mod, b_mod, norm_mix, norm_ffn, w_out, ev_w_in, hg_lb, hg_onorm, lru_conv_w, lru_conv_b, lru_wa, lru_ba, lru_wx, lru_bx, lru_lambda, od_w_in, mla_q_norm, mla_w_q_up, mla_kv_norm, mla_w_kv_up, ssd_conv_w, ssd_conv_b, ssd_dt_bias, ssd_a_log, ssd_d, ssd_onorm, moe_router, moe_w_gate, moe_w_up, moe_w_down, final_norm):
    rows = x.shape[1] // GRID_W
    cos, sin = axial_rope(rows)
    lb_all = jnp.cumsum(jax.nn.softmax(hg_lb.astype(F32), axis=1), axis=1)
    for l in range(DEPTH):
        need_ctx = l < DEPTH - 1
        j = l // 2
        mod = jax.nn.silu(c) @ w_mod[l] + b_mod[l]
        mod_c = jax.nn.silu(c_ctx) @ w_mod[l] + b_mod[l]
        sh1, sc1, g1, sh2, sc2, g2 = jnp.split(mod[:, None, :], 6, axis=-1)
        csh1, csc1, cg1, csh2, csc2, cg2 = jnp.split(mod_c, 6, axis=-1)
        h_l = rmsnorm(x, norm_mix[l]) * (1.0 + sc1) + sh1
        h_c = rmsnorm(ctx, norm_mix[l]) * (1.0 + csc1) + csh1
        wg, wu, wd = moe_w_gate[l].astype(BF16), moe_w_up[l].astype(BF16), moe_w_down[l].astype(BF16)
        if l % 2 == 0:
            w_in = ev_w_in[j]
            u_c = _mm(h_c, w_in if need_ctx else w_in[:, EV_OUT:])
            o_c, o_l = even_mixer(u_c, _mm(h_l, w_in), lb_all[0, j], lb_all[1, j], hg_onorm[j], lru_conv_w[j], lru_conv_b[j], lru_wa[j], lru_ba[j], lru_wx[j], lru_bx[j], lru_lambda[j], need_ctx)
        else:
            w_in = od_w_in[j]
            u_c = _mm(h_c, w_in if need_ctx else w_in[:, OD_OUT:])
            o_c, o_l = odd_mixer(u_c, _mm(h_l, w_in), mla_q_norm[j], mla_w_q_up[j], mla_kv_norm[j], mla_w_kv_up[j], ssd_conv_w[j], ssd_conv_b[j], ssd_dt_bias[j], ssd_a_log[j], ssd_d[j], ssd_onorm[j], cos, sin, need_ctx)
        x = x + g1 * _mm(o_l, w_out[l])
        x = x + g2 * ec_moe(rmsnorm(x, norm_ffn[l]) * (1.0 + sc2) + sh2, moe_router[l], wg, wu, wd)
        if need_ctx:
            ctx = ctx + cg1 * _mm(o_c, w_out[l])
            ctx = ctx + cg2 * ec_moe(rmsnorm(ctx, norm_ffn[l]) * (1.0 + csc2) + csh2, moe_router[l], wg, wu, wd)
    return rmsnorm(x, final_norm)
```

```python
import functools
import math

import jax
import jax.numpy as jnp
from jax import lax
from jax.experimental import pallas as pl
from jax.experimental.pallas import tpu as pltpu

D_MODEL = 2048
DEPTH = 2
CTX_LEN = 256
GRID_W = 64
EPS = 1e-6
F32 = jnp.float32
BF16 = jnp.bfloat16

HG_HEADS = 8
HG_DK = 128
HG_DV = 128
HG_KW = HG_HEADS * HG_DK
HG_VW = HG_HEADS * HG_DV
HG_CHUNK = 64

LRU_WIDTH = 1024
LRU_HEADS = 8
LRU_C = 8.0

MLA_HEADS = 8
MLA_Q_LORA = 512
MLA_KV_LORA = 512
MLA_NOPE = 128
MLA_ROPE = 64
MLA_V = 128
ROPE_THETA = 10000.0
ATTN_BLOCK = 128

SSD_WIDTH = 1024
SSD_HEADDIM = 64
SSD_HEADS = SSD_WIDTH // SSD_HEADDIM
SSD_GROUPS = 2
SSD_STATE = 128
SSD_BC = SSD_GROUPS * SSD_STATE
SSD_XBC = SSD_WIDTH + 2 * SSD_BC
SSD_CHUNK = 128

N_EXPERTS = 16
CAPACITY_FACTOR = 2

EV_OUT = HG_KW + HG_VW + LRU_WIDTH
OD_OUT = MLA_Q_LORA + SSD_WIDTH

V7X_VMEM_LIMIT_BYTES = 56 * 1024 * 1024


def _mm_kernel(a_ref, b_ref, o_ref):
    o_ref[0] = jnp.dot(a_ref[0].astype(BF16), b_ref[0].astype(BF16), preferred_element_type=F32)


def _pick_tile(n, target):
    if n <= target:
        return n
    t = target
    while n % t:
        t //= 2
    return t


def _lane_tile(n, target=1536):
    k = n // 128
    return 128 * max(d for d in range(1, k + 1) if k % d == 0 and 128 * d <= target)


def _bmm(a, b, tm=512, keep_pad=False):
    g, m, k = a.shape
    gb, _, n = b.shape
    tm = _pick_tile(m, tm)
    n_real = n
    cands = [(-(-n // 128) + i) * 128 for i in range(4)]
    n = next((c for c in cands if _lane_tile(c) >= 1024), max(cands, key=_lane_tile))
    if n != n_real:
        b = jnp.pad(b, ((0, 0), (0, 0), (0, n - n_real)))
    tn = _lane_tile(n)
    out = _bmm_call(a, b, g, gb, m, k, n, tm, tn)
    return out if keep_pad or n == n_real else out[..., :n_real]


def _bmm_call(a, b, g, gb, m, k, n, tm, tn):
    return pl.pallas_call(
        _mm_kernel,
        grid=(g, m // tm, n // tn),
        in_specs=[
            pl.BlockSpec((1, tm, k), lambda gi, i, j: (gi, i, 0)),
            pl.BlockSpec((1, k, tn), lambda gi, i, j: (gi % gb, 0, j)),
        ],
        out_specs=pl.BlockSpec((1, tm, tn), lambda gi, i, j: (gi, i, j)),
        out_shape=jax.ShapeDtypeStruct((g, m, n), F32),
        compiler_params=pltpu.CompilerParams(
            dimension_semantics=("parallel", "parallel", "arbitrary"),
            vmem_limit_bytes=V7X_VMEM_LIMIT_BYTES),
    )(a, b)


def _mm(a, b, keep_pad=False):
    lead = a.shape[:-1]
    m = math.prod(lead)
    out = _bmm(a.reshape(1, m, a.shape[-1]), b[None].astype(BF16), keep_pad=keep_pad)
    return out.reshape(*lead, out.shape[-1])


LRU_BLOCK = LRU_WIDTH // LRU_HEADS
LRU_L = 128
SUBLANES = 8
LRU_PAD = SUBLANES


def _lru_kernel(seq, xc_ref, gc_ref, xl_ref, gl_ref, cw_ref, cb_ref, wa_ref, wx_ref, ba_ref, bx_ref, lam_ref,
                oc_ref, ol_ref, xs_ref, hf_ref):
    t_c, t_l = seq
    n_c, n_l = t_c // LRU_L, t_l // LRU_L
    off_c, off_l = LRU_PAD, 2 * LRU_PAD + t_c
    zeros = jnp.zeros((LRU_PAD, LRU_BLOCK), F32)
    xs_ref[0:LRU_PAD] = zeros
    xs_ref[off_c + t_c:off_l] = zeros
    xs_ref[off_l + t_l:off_l + t_l + LRU_PAD] = zeros
    xs_ref[off_c:off_c + t_c] = xc_ref[0]
    xs_ref[off_l:off_l + t_l] = xl_ref[0]

    cw = cw_ref[...]
    cb = cb_ref[...]
    row = lax.broadcasted_iota(jnp.int32, (LRU_L // SUBLANES, SUBLANES, LRU_BLOCK), 1)

    def gates(r0, d):
        win = xs_ref[pl.ds(r0 - LRU_PAD, LRU_L + 2 * LRU_PAD), :]
        x = cb
        for k in range(4):
            x = x + cw[k:k + 1] * win[LRU_PAD - 2 + k:LRU_PAD - 2 + k + LRU_L]
        xb = x.astype(BF16)
        r = jax.nn.sigmoid(jnp.dot(xb, wa_ref[d, 0].astype(BF16), preferred_element_type=F32) + ba_ref[d:d + 1])
        ig = jax.nn.sigmoid(jnp.dot(xb, wx_ref[d, 0].astype(BF16), preferred_element_type=F32) + bx_ref[d:d + 1])
        log_a = -LRU_C * r * jax.nn.softplus(-lam_ref[d:d + 1])
        a = jnp.exp(log_a)
        return a, jnp.sqrt(1.0 - a * a) * ig * x

    def chunk_scan(a, b, carry, reverse):
        n = LRU_L // SUBLANES
        a = a.reshape(n, SUBLANES, LRU_BLOCK)
        b = b.reshape(n, SUBLANES, LRU_BLOCK)
        for s in (1, 2, 4):
            if reverse:
                keep = row < SUBLANES - s
                shift = SUBLANES - s
            else:
                keep = row >= s
                shift = s
            a_n = jnp.where(keep, pltpu.roll(a, shift, 1), 1.0)
            b_n = jnp.where(keep, pltpu.roll(b, shift, 1), 0.0)
            b = a * b_n + b
            a = a * a_n
        outs = [None] * n
        last = 0 if reverse else SUBLANES - 1
        for j in (range(n - 1, -1, -1) if reverse else range(n)):
            h = b[j] + a[j] * carry
            carry = h[last:last + 1]
            outs[j] = h
        return jnp.concatenate(outs, axis=0), carry

    def fwd_chunk(t0, r0, carry):
        a, b = gates(r0, 0)
        h, carry = chunk_scan(a, b, carry, False)
        hf_ref[pl.ds(t0, LRU_L), :] = h
        return carry

    def bwd_chunk(t0, r0, carry):
        a, b = gates(r0, 1)
        h, carry = chunk_scan(a, b, carry, True)
        return hf_ref[pl.ds(t0, LRU_L), :] + h, carry

    carry = jnp.zeros((1, LRU_BLOCK), F32)
    for c in range(n_c):
        carry = fwd_chunk(c * LRU_L, off_c + c * LRU_L, carry)

    def fwd_body(c, carry):
        t0 = pl.multiple_of(c * LRU_L, LRU_L)
        return fwd_chunk(t_c + t0, off_l + t0, carry)

    lax.fori_loop(0, n_l, fwd_body, carry)

    carry = jnp.zeros((1, LRU_BLOCK), F32)
    for c in range(n_c - 1, -1, -1):
        h, carry = bwd_chunk(c * LRU_L, off_c + c * LRU_L, carry)
        oc_ref[0, c * LRU_L:(c + 1) * LRU_L, :] = h * jax.nn.gelu(gc_ref[0, c * LRU_L:(c + 1) * LRU_L, :])

    def bwd_body(i, carry):
        t0 = pl.multiple_of((n_l - 1 - i) * LRU_L, LRU_L)
        h, carry = bwd_chunk(t_c + t0, off_l + t0, carry)
        ol_ref[0, pl.ds(t0, LRU_L), :] = h * jax.nn.gelu(gl_ref[0, pl.ds(t0, LRU_L), :])
        return carry

    lax.fori_loop(0, n_l, bwd_body, carry)


def _rglru(u_c, u_l, conv_w, conv_b, wa, ba, wx, bx, lam):
    bsz, t_c, _ = u_c.shape
    t_l = u_l.shape[1]
    x_blk = (EV_OUT + 2 * HG_KW + HG_VW) // LRU_BLOCK
    g_blk = (HG_KW + HG_VW) // LRU_BLOCK
    t_pad = t_c + t_l + 3 * LRU_PAD
    vec = lambda: pl.BlockSpec((2, LRU_BLOCK), lambda b, h: (0, h))
    return pl.pallas_call(
        functools.partial(_lru_kernel, (t_c, t_l)),
        grid=(bsz, LRU_HEADS),
        in_specs=[
            pl.BlockSpec((1, t_c, LRU_BLOCK), lambda b, h: (b, 0, x_blk + h)),
            pl.BlockSpec((1, t_c, LRU_BLOCK), lambda b, h: (b, 0, g_blk + h)),
            pl.BlockSpec((1, t_l, LRU_BLOCK), lambda b, h: (b, 0, x_blk + h)),
            pl.BlockSpec((1, t_l, LRU_BLOCK), lambda b, h: (b, 0, g_blk + h)),
            pl.BlockSpec((4, LRU_BLOCK), lambda b, h: (0, h)),
            pl.BlockSpec((1, LRU_BLOCK), lambda b, h: (0, h)),
            pl.BlockSpec((2, 1, LRU_BLOCK, LRU_BLOCK), lambda b, h: (0, h, 0, 0)),
            pl.BlockSpec((2, 1, LRU_BLOCK, LRU_BLOCK), lambda b, h: (0, h, 0, 0)),
            vec(), vec(), vec(),
        ],
        out_specs=[
            pl.BlockSpec((1, t_c, LRU_BLOCK), lambda b, h: (b, 0, h)),
            pl.BlockSpec((1, t_l, LRU_BLOCK), lambda b, h: (b, 0, h)),
        ],
        out_shape=[jax.ShapeDtypeStruct((bsz, t_c, LRU_WIDTH), F32), jax.ShapeDtypeStruct((bsz, t_l, LRU_WIDTH), F32)],
        scratch_shapes=[pltpu.VMEM((t_pad, LRU_BLOCK), F32), pltpu.VMEM((t_c + t_l, LRU_BLOCK), F32)],
        compiler_params=pltpu.CompilerParams(
            dimension_semantics=("parallel", "parallel"), vmem_limit_bytes=V7X_VMEM_LIMIT_BYTES),
        name="rglru",
    )(u_c, u_c, u_l, u_l, conv_w, conv_b[None], wa, wx, ba, bx, lam)


HG_GROUP = 8


def _hgrn2_kernel(seq, qc_ref, gc_ref, ffc_ref, fbc_ref, vc_ref, ql_ref, gl_ref, ffl_ref, fbl_ref, vl_ref,
                  lbf_ref, lbb_ref, gain_ref, oc_ref, ol_ref, of_ref):
    t_c, t_l = seq
    L = HG_CHUNK
    R = HG_GROUP * L
    n_gc = t_c // L
    r_i = lax.broadcasted_iota(jnp.int32, (R, R), 0)
    c_i = lax.broadcasted_iota(jnp.int32, (R, R), 1)
    same = (r_i // L) == (c_i // L)
    causal = (same & (c_i <= r_i), same & (c_i >= r_i))
    tri = tuple(jnp.where(m, 1.0, 0.0).astype(BF16) for m in causal)
    lb = (lbf_ref[...], lbb_ref[...])
    gain = gain_ref[...]

    def group(d, n, q_raw, f_raw, v, st):
        rows = n * L
        f = lb[d] + (1.0 - lb[d]) * jax.nn.sigmoid(f_raw)
        logf = jnp.log(f)
        k = 1.0 - f
        hi = logf.astype(BF16)
        r1 = logf - hi.astype(F32)
        mid = r1.astype(BF16)
        lo = (r1 - mid.astype(F32)).astype(BF16)
        g3 = jnp.dot(tri[d][:rows, :rows], jnp.concatenate([hi, mid, lo], axis=1), preferred_element_type=F32)
        g = g3[:, :HG_DK] + g3[:, HG_DK:2 * HG_DK] + g3[:, 2 * HG_DK:]
        g4 = g.reshape(n, L, HG_DK)
        g_last = g4[:, 0:1] if d else g4[:, L - 1:L]
        g_rest = (jnp.broadcast_to(g_last, (n, L, HG_DK)) - g4).reshape(rows, HG_DK)
        qg = (jax.nn.silu(q_raw) * jnp.exp(g)).astype(BF16)
        kg = (k * jnp.exp(-g)).astype(BF16)
        kd = (k * jnp.exp(g_rest)).astype(BF16)
        vb = v.astype(BF16)
        att = lax.dot_general(qg, kg, (((1,), (1,)), ((), ())), preferred_element_type=F32)
        att = jnp.where(causal[d][:rows, :rows], att, 0.0).astype(BF16)
        o = jnp.dot(att, vb, preferred_element_type=F32)
        dec = jnp.exp(g_last)
        sl = [slice(j * L, (j + 1) * L) for j in range(n)]
        ds_t = [lax.dot_general(vb[s], kd[s], (((0,), (0,)), ((), ())), preferred_element_type=F32) for s in sl]
        entering = [None] * n
        for j in (range(n - 1, -1, -1) if d else range(n)):
            entering[j] = st.astype(BF16)
            st = dec[j] * st + ds_t[j]
        outs = [o[s] + lax.dot_general(qg[s], e, (((1,), (1,)), ((), ())), preferred_element_type=F32)
                for s, e in zip(sl, entering)]
        return jnp.concatenate(outs, axis=0), st

    def finish(o, gate):
        return _rms(o, gain) * jax.nn.silu(gate)

    def lat_rows(i):
        return pl.ds(pl.multiple_of(i * R, R), R), pl.ds(pl.multiple_of(t_c + i * R, math.gcd(t_c, R)), R)

    st = jnp.zeros((HG_DV, HG_DK), F32)
    o, st = group(0, n_gc, qc_ref[0], ffc_ref[0], vc_ref[0], st)
    of_ref[0:t_c, :] = o

    def fwd_body(i, st):
        rows, srows = lat_rows(i)
        o, st = group(0, HG_GROUP, ql_ref[0, rows, :], ffl_ref[0, rows, :], vl_ref[0, rows, :], st)
        of_ref[srows, :] = o
        return st

    lax.fori_loop(0, t_l // R, fwd_body, st)

    st = jnp.zeros((HG_DV, HG_DK), F32)
    o, st = group(1, n_gc, qc_ref[0], fbc_ref[0], vc_ref[0], st)
    oc_ref[0] = finish(of_ref[0:t_c, :] + o, gc_ref[0])

    def bwd_body(i, st):
        rows, srows = lat_rows(t_l // R - 1 - i)
        o, st = group(1, HG_GROUP, ql_ref[0, rows, :], fbl_ref[0, rows, :], vl_ref[0, rows, :], st)
        ol_ref[0, rows, :] = finish(of_ref[srows, :] + o, gl_ref[0, rows, :])
        return st

    lax.fori_loop(0, t_l // R, bwd_body, st)


def _hgrn2(u_c, u_l, lb_f, lb_b, onorm):
    bsz, t_c, _ = u_c.shape
    t_l = u_l.shape[1]
    nb = HG_KW // HG_DK
    col = lambda t, blk: pl.BlockSpec((1, t, HG_DK), lambda b, h: (b, 0, blk + h))
    seq_cols = lambda t: [col(t, 0), col(t, nb), col(t, EV_OUT // HG_DK), col(t, EV_OUT // HG_DK + nb), col(t, EV_OUT // HG_DK + 2 * nb)]
    vec = lambda: pl.BlockSpec((1, HG_DK), lambda b, h: (0, h))
    return pl.pallas_call(
        functools.partial(_hgrn2_kernel, (t_c, t_l)),
        grid=(bsz, HG_HEADS),
        in_specs=seq_cols(t_c) + seq_cols(t_l) + [vec(), vec(), vec()],
        out_specs=[
            pl.BlockSpec((1, t_c, HG_DV), lambda b, h: (b, 0, h)),
            pl.BlockSpec((1, t_l, HG_DV), lambda b, h: (b, 0, h)),
        ],
        out_shape=[jax.ShapeDtypeStruct((bsz, t_c, HG_VW), F32), jax.ShapeDtypeStruct((bsz, t_l, HG_VW), F32)],
        scratch_shapes=[pltpu.VMEM((t_c + t_l, HG_DV), F32)],
        compiler_params=pltpu.CompilerParams(
            dimension_semantics=("parallel", "parallel"), vmem_limit_bytes=V7X_VMEM_LIMIT_BYTES),
        name="hgrn2",
    )(*([u_c] * 5), *([u_l] * 5), lb_f[None], lb_b[None], onorm[None])


LANES = 128
MLA_HW = 2 * LANES
MLA_TM = 512
MLA_TQ = 512
MLA_TK = 512
MLA_SCALE = (MLA_NOPE + MLA_ROPE) ** -0.5


def _rms(x, gain):
    return x * lax.rsqrt(jnp.mean(x * x, axis=-1, keepdims=True) + EPS) * gain


def _rope_tile(r, cos_t, sin_t):
    return r * cos_t + (pltpu.roll(r, MLA_ROPE // 2, 1) + pltpu.roll(r, LANES - MLA_ROPE // 2, 1)) * sin_t


def _mla_q_kernel(u_ref, gain_ref, w_ref, cos_ref, sin_ref, q_ref):
    q = jnp.dot(_rms(u_ref[0], gain_ref[...]).astype(BF16), w_ref[...], preferred_element_type=F32)
    for h in range(MLA_HEADS):
        lo = h * MLA_HW
        q_ref[0, :, lo:lo + LANES] = q[:, lo:lo + LANES].astype(BF16)
        q_ref[0, :, lo + LANES:lo + MLA_HW] = _rope_tile(q[:, lo + LANES:lo + MLA_HW], cos_ref[...], sin_ref[...]).astype(BF16)


def _mla_kv_kernel(u_ref, kr_ref, gain_ref, w_ref, cos_ref, sin_ref, k_ref, v_ref):
    kv = jnp.dot(_rms(u_ref[0], gain_ref[...]).astype(BF16), w_ref[...], preferred_element_type=F32)
    kr = _rope_tile(kr_ref[0], cos_ref[...], sin_ref[...]).astype(BF16)
    for h in range(MLA_HEADS):
        k_ref[0, :, h * MLA_HW:h * MLA_HW + LANES] = kv[:, h * LANES:(h + 1) * LANES].astype(BF16)
        k_ref[0, :, h * MLA_HW + LANES:(h + 1) * MLA_HW] = kr
    v_ref[0] = kv[:, MLA_HEADS * LANES:].astype(BF16)


def _mla_q(u, gain, w, cos_t, sin_t):
    bsz, t, _ = u.shape
    tm = min(MLA_TM, t)
    return pl.pallas_call(
        _mla_q_kernel,
        grid=(bsz, t // tm),
        in_specs=[
            pl.BlockSpec((1, tm, MLA_Q_LORA), lambda b, i: (b, i, 0)),
            pl.BlockSpec((1, MLA_Q_LORA), lambda b, i: (0, 0)),
            pl.BlockSpec((MLA_Q_LORA, MLA_HEADS * MLA_HW), lambda b, i: (0, 0)),
            pl.BlockSpec((tm, LANES), lambda b, i: (i, 0)),
            pl.BlockSpec((tm, LANES), lambda b, i: (i, 0)),
        ],
        out_specs=pl.BlockSpec((1, tm, MLA_HEADS * MLA_HW), lambda b, i: (b, i, 0)),
        out_shape=jax.ShapeDtypeStruct((bsz, t, MLA_HEADS * MLA_HW), BF16),
        compiler_params=pltpu.CompilerParams(
            dimension_semantics=("parallel", "parallel"), vmem_limit_bytes=V7X_VMEM_LIMIT_BYTES),
        name="mla_q_up",
    )(u, gain[None], w, cos_t, sin_t)


def _mla_kv(u, c_blk, r_blk, gain, w, cos_t, sin_t):
    bsz, t, _ = u.shape
    tm = min(MLA_TM, t)
    return pl.pallas_call(
        _mla_kv_kernel,
        grid=(bsz, t // tm),
        in_specs=[
            pl.BlockSpec((1, tm, MLA_KV_LORA), lambda b, i: (b, i, c_blk)),
            pl.BlockSpec((1, tm, LANES), lambda b, i: (b, i, r_blk)),
            pl.BlockSpec((1, MLA_KV_LORA), lambda b, i: (0, 0)),
            pl.BlockSpec((MLA_KV_LORA, MLA_HEADS * (LANES + MLA_V)), lambda b, i: (0, 0)),
            pl.BlockSpec((tm, LANES), lambda b, i: (i, 0)),
            pl.BlockSpec((tm, LANES), lambda b, i: (i, 0)),
        ],
        out_specs=[
            pl.BlockSpec((1, tm, MLA_HEADS * MLA_HW), lambda b, i: (b, i, 0)),
            pl.BlockSpec((1, tm, MLA_HEADS * MLA_V), lambda b, i: (b, i, 0)),
        ],
        out_shape=[jax.ShapeDtypeStruct((bsz, t, MLA_HEADS * MLA_HW), BF16),
                   jax.ShapeDtypeStruct((bsz, t, MLA_HEADS * MLA_V), BF16)],
        compiler_params=pltpu.CompilerParams(
            dimension_semantics=("parallel", "parallel"), vmem_limit_bytes=V7X_VMEM_LIMIT_BYTES),
        name="mla_kv_up",
    )(u, u, gain[None], w, cos_t, sin_t)


def _flash_kernel(t_l, q_ref, kc_ref, vc_ref, kl_ref, vl_ref, o_ref):
    q = q_ref[0]

    def step(k, v, carry):
        m, l, acc = carry
        s = lax.dot_general(q, k, (((1,), (1,)), ((), ())), preferred_element_type=F32) * MLA_SCALE
        m_new = jnp.maximum(m, jnp.max(s, axis=-1, keepdims=True))
        alpha = jnp.exp(m - m_new)
        p = jnp.exp(s - m_new)
        l = alpha * l + jnp.sum(p, axis=-1, keepdims=True)
        acc = alpha * acc + jnp.dot(p.astype(BF16), v, preferred_element_type=F32)
        return m_new, l, acc

    tq = q.shape[0]
    carry = (jnp.full((tq, 1), -jnp.inf, F32), jnp.zeros((tq, 1), F32), jnp.zeros((tq, MLA_V), F32))
    carry = step(kc_ref[0], vc_ref[0], carry)

    def body(i, carry):
        r0 = pl.multiple_of(i * MLA_TK, MLA_TK)
        return step(kl_ref[0, pl.ds(r0, MLA_TK), :], vl_ref[0, pl.ds(r0, MLA_TK), :], carry)

    _, l, acc = lax.fori_loop(0, t_l // MLA_TK, body, carry)
    o_ref[0] = acc / l


def _flash(q, k_c, v_c, k_l, v_l):
    bsz, t, _ = q.shape
    t_c, t_l = k_c.shape[1], k_l.shape[1]
    return pl.pallas_call(
        functools.partial(_flash_kernel, t_l),
        grid=(bsz, MLA_HEADS, t // MLA_TQ),
        in_specs=[
            pl.BlockSpec((1, MLA_TQ, MLA_HW), lambda b, h, i: (b, i, h)),
            pl.BlockSpec((1, t_c, MLA_HW), lambda b, h, i: (b, 0, h)),
            pl.BlockSpec((1, t_c, MLA_V), lambda b, h, i: (b, 0, h)),
            pl.BlockSpec((1, t_l, MLA_HW), lambda b, h, i: (b, 0, h)),
            pl.BlockSpec((1, t_l, MLA_V), lambda b, h, i: (b, 0, h)),
        ],
        out_specs=pl.BlockSpec((1, MLA_TQ, MLA_V), lambda b, h, i: (b, i, h)),
        out_shape=jax.ShapeDtypeStruct((bsz, t, MLA_HEADS * MLA_V), F32),
        compiler_params=pltpu.CompilerParams(
            dimension_semantics=("parallel", "parallel", "arbitrary"), vmem_limit_bytes=V7X_VMEM_LIMIT_BYTES),
        name="mla_flash",
    )(q, k_c, v_c, k_l, v_l)


def _mla(u_c, u_l, q_norm, w_q_up, kv_norm, w_kv_up, cos, sin):
    t_c, t_l = u_c.shape[1], u_l.shape[1]
    zeros = jnp.zeros((t_l, LANES - MLA_ROPE), F32)
    cos_l = jnp.concatenate([cos, cos, zeros], axis=-1)
    sin_l = jnp.concatenate([-sin, sin, zeros], axis=-1)
    cos_c = jnp.concatenate([jnp.ones((t_c, MLA_ROPE), F32), zeros[:t_c]], axis=-1)
    sin_c = jnp.zeros((t_c, LANES), F32)
    wq = w_q_up.reshape(MLA_Q_LORA, MLA_HEADS, MLA_NOPE + MLA_ROPE)
    wq = jnp.pad(wq, ((0, 0), (0, 0), (0, MLA_HW - MLA_NOPE - MLA_ROPE))).reshape(MLA_Q_LORA, -1).astype(BF16)
    wkv = w_kv_up.reshape(MLA_KV_LORA, MLA_HEADS, MLA_NOPE + MLA_V)
    wkv = jnp.concatenate([wkv[..., :MLA_NOPE].reshape(MLA_KV_LORA, -1), wkv[..., MLA_NOPE:].reshape(MLA_KV_LORA, -1)], axis=-1).astype(BF16)
    q = _mla_q(u_l, q_norm, wq, cos_l, sin_l)
    k_l, v_l = _mla_kv(u_l, OD_OUT // MLA_KV_LORA, (OD_OUT + MLA_KV_LORA) // LANES, kv_norm, wkv, cos_l, sin_l)
    k_c, v_c = _mla_kv(u_c, 0, MLA_KV_LORA // LANES, kv_norm, wkv, cos_c, sin_c)
    return _flash(q, k_c, v_c, k_l, v_l)


def rmsnorm(x, g):
    xf = x.astype(F32)
    y = xf * lax.rsqrt(jnp.mean(xf * xf, axis=-1, keepdims=True) + EPS)
    return (y * g.astype(F32)).astype(x.dtype)


def heads(t, h):
    return t.reshape(t.shape[0], t.shape[1], h, t.shape[-1] // h)


def flip(t):
    return jnp.flip(t, axis=1)


def dwconv(x, w, b):
    k = w.shape[0]
    y = lax.conv_general_dilated(x, w[:, None, :].astype(x.dtype), window_strides=(1,), padding=[(k // 2, k - 1 - k // 2)], dimension_numbers=('NWC', 'WIO', 'NWC'), feature_group_count=x.shape[-1])
    return y + b.astype(x.dtype)


def axial_rope(rows):
    row = jnp.repeat(jnp.arange(rows, dtype=F32), GRID_W)
    col = jnp.arange(rows * GRID_W) % GRID_W
    n_freq = MLA_ROPE // 4
    inv = ROPE_THETA ** (-jnp.arange(n_freq, dtype=F32) / n_freq)
    ang = jnp.concatenate([row[:, None] * inv, col.astype(F32)[:, None] * inv], axis=-1)
    return jnp.cos(ang), jnp.sin(ang)


def apply_rope(t, cos, sin):
    t1, t2 = jnp.split(t.astype(F32), 2, axis=-1)
    cs, sn = cos[None, :, None, :], sin[None, :, None, :]
    return jnp.concatenate([t1 * cs - t2 * sn, t1 * sn + t2 * cs], axis=-1).astype(t.dtype)


def segsum(x):
    t = x.shape[-1]
    xx = jnp.broadcast_to(x[..., :, None], x.shape + (t,))
    xx = jnp.where(jnp.tril(jnp.ones((t, t), bool), -1), xx, 0.0)
    cs = jnp.cumsum(xx, axis=-2)
    return jnp.where(jnp.tril(jnp.ones((t, t), bool)), cs, -jnp.inf)


def gla_chunked(q, k, v, logf, s0):
    b, t, h, _ = k.shape
    dv = v.shape[-1]
    n = t // HG_CHUNK
    blk = lambda a: a.reshape(b, n, HG_CHUNK, h, a.shape[-1])
    k, v, g = blk(k), blk(v), jnp.cumsum(blk(logf), axis=2)
    g_last = g[:, :, -1:]
    ds = jnp.einsum('bnmhd,bnmhe->bnhde', k * jnp.exp(g_last - g), v)
    decay = jnp.exp(g_last[:, :, 0])

    def step(s, inp):
        dec, d = inp
        return dec[..., None] * s + d, s

    s_fin, s_prev = lax.scan(step, s0, (jnp.moveaxis(decay, 1, 0), jnp.moveaxis(ds, 1, 0)))
    if q is None:
        return None, s_fin
    qg = blk(q) * jnp.exp(g)
    att = jnp.einsum('bnlhd,bnmhd->bnhlm', qg, k * jnp.exp(-g))
    att = jnp.where(jnp.tril(jnp.ones((HG_CHUNK, HG_CHUNK), bool)), att, 0.0)
    o = jnp.einsum('bnhlm,bnmhe->bnlhe', att, v) + jnp.einsum('bnlhd,nbhde->bnlhe', qg, s_prev)
    return o.reshape(b, t, h, dv), s_fin


def hgrn2_scan(q_c, f_c, i_c, q_l, f_l, i_l, lb):
    def gates(fr):
        f = lb + (1.0 - lb) * jax.nn.sigmoid(fr.astype(F32))
        return heads(jnp.log(f), HG_HEADS), heads(1.0 - f, HG_HEADS)

    s0 = jnp.zeros((f_l.shape[0], HG_HEADS, HG_DK, HG_DV), F32)
    lf, k = gates(f_c)
    o_c, s_c = gla_chunked(q_c, k, i_c, lf, s0)
    lf, k = gates(f_l)
    o_l, _ = gla_chunked(q_l, k, i_l, lf, s_c)
    return o_c, o_l


def even_mixer(u_c, u_l, lb_f, lb_b, onorm, conv_w, conv_b, wa, ba, wx, bx, lam, need_ctx):
    assert need_ctx
    lru_c, lru_l = _rglru(u_c, u_l, conv_w, conv_b, wa, ba, wx, bx, lam)
    hg_c, hg_l = _hgrn2(u_c, u_l, lb_f, lb_b, onorm)
    return jnp.concatenate([hg_c, lru_c], axis=-1), jnp.concatenate([hg_l, lru_l], axis=-1)


def block_attention(q, k, v):
    b, t, h, dq = q.shape
    nb = t // ATTN_BLOCK
    scale = dq ** -0.5
    qb = jnp.moveaxis(q.reshape(b, nb, ATTN_BLOCK, h, dq), 1, 0)

    def one(qblk):
        s = jnp.einsum('bqhd,bkhd->bhqk', qblk, k).astype(F32) * scale
        p = jax.nn.softmax(s, axis=-1).astype(v.dtype)
        return jnp.einsum('bhqk,bkhe->bqhe', p, v)

    o = lax.map(one, qb)
    return jnp.moveaxis(o, 0, 1).reshape(b, t, h, v.shape[-1])


def ssd_chunked(x, dt, a, bm, cm, s0, want_y):
    b, t, h, p = x.shape
    n_c = t // SSD_CHUNK
    rep = h // bm.shape[2]
    blk = lambda z: z.reshape(b, n_c, SSD_CHUNK, *z.shape[2:])
    xdt = blk(x * dt[..., None])
    bh = blk(jnp.repeat(bm, rep, axis=2))
    a_dt = jnp.moveaxis(blk(dt * a), 3, 1)
    a_cs = jnp.cumsum(a_dt, axis=-1)
    states = jnp.einsum('bclhn,bhcl,bclhp->bchpn', bh, jnp.exp(a_cs[..., -1:] - a_cs), xdt)
    states = jnp.concatenate([s0[:, None], states], axis=1)
    chunk_decay = jnp.exp(segsum(jnp.pad(a_cs[..., -1], ((0, 0), (0, 0), (1, 0)))))
    new_states = jnp.einsum('bhzc,bchpn->bzhpn', chunk_decay, states)
    if not want_y:
        return None, new_states[:, -1]
    ch = blk(jnp.repeat(cm, rep, axis=2))
    scores = jnp.einsum('bclhn,bcshn->bhcls', ch, bh) * jnp.exp(segsum(a_dt))
    y = jnp.einsum('bhcls,bcshp->bclhp', scores, xdt) + jnp.einsum('bclhn,bchpn,bhcl->bclhp', ch, new_states[:, :-1], jnp.exp(a_cs))
    return y.reshape(b, t, h, p), new_states[:, -1]


def odd_mixer(u_c, u_l, q_norm, w_q_up, kv_norm, w_kv_up, conv_w, conv_b, dt_bias, a_log, d_skip, onorm, cos, sin, need_ctx):
    def ssd_in(st):
        o = MLA_KV_LORA + MLA_ROPE
        xbc = jax.nn.silu(dwconv(st[..., o:o + SSD_XBC], conv_w, conv_b))
        xs = heads(xbc[..., :SSD_WIDTH], SSD_HEADS)
        bm = heads(xbc[..., SSD_WIDTH:SSD_WIDTH + SSD_BC], SSD_GROUPS)
        cm = heads(xbc[..., SSD_WIDTH + SSD_BC:], SSD_GROUPS)
        return [xs, bm, cm, st[..., o + SSD_XBC:o + SSD_XBC + 2 * SSD_HEADS].astype(F32)]

    def ssd_dir(inp_c, inp_l, d):
        a = -jnp.exp(a_log[d].astype(F32))

        def run(inp, s0, want_y):
            xs, bm, cm, dtr = inp
            dt = jax.nn.softplus(dtr[..., d * SSD_HEADS:(d + 1) * SSD_HEADS] + dt_bias[d].astype(F32))
            return ssd_chunked(xs, dt, a, bm, cm, s0, want_y)

        s0 = jnp.zeros((inp_l[0].shape[0], SSD_HEADS, SSD_HEADDIM, SSD_STATE), F32)
        y_c, s_c = run(inp_c, s0, need_ctx)
        y_l, _ = run(inp_l, s_c, True)
        return y_c, y_l

    def ssd_out(y, xs, z):
        y = (y + d_skip.astype(F32)[:, None] * xs).reshape(z.shape) * jax.nn.silu(z.astype(F32))
        return rmsnorm(heads(y, SSD_GROUPS), onorm.reshape(SSD_GROUPS, -1)).reshape(z.shape)

    assert not need_ctx
    lo_l, st_l = u_l[..., :OD_OUT], u_l[..., OD_OUT:]
    st_c = u_c
    b, s = u_l.shape[:2]

    att_l = _mla(u_c, u_l, q_norm, w_q_up, kv_norm, w_kv_up, cos, sin)

    in_c, in_l = ssd_in(st_c), ssd_in(st_l)
    yc_f, yl_f = ssd_dir(in_c, in_l, 0)
    yc_b, yl_b = ssd_dir([flip(t) for t in in_c], [flip(t) for t in in_l], 1)

    y_l = jnp.concatenate([att_l, ssd_out(yl_f + flip(yl_b), in_l[0], lo_l[..., MLA_Q_LORA:])], axis=-1).astype(u_l.dtype)
    return None, y_l


def ec_moe(h, w_router, w_gate, w_up, w_down):
    b, t, d = h.shape
    cap = CAPACITY_FACTOR * t // N_EXPERTS
    aff = jax.nn.softmax((h @ w_router).astype(F32), axis=-1)
    gate, idx = lax.top_k(jnp.swapaxes(aff, 1, 2), cap)
    bidx = jnp.arange(b)[:, None, None]
    xe = h[bidx, idx].reshape(b * N_EXPERTS, cap, d)
    hid = jax.nn.silu(_bmm(xe, w_gate)) * _bmm(xe, w_up)
    ye = _bmm(hid, w_down).reshape(b, N_EXPERTS, cap, d) * gate[..., None].astype(h.dtype)
    return jnp.zeros_like(h).at[bidx, idx].add(ye)


def kernel(x, c, ctx, c_ctx, w_mod, b_mod, norm_mix, norm_ffn, w_out, ev_w_in, hg_lb, hg_onorm, lru_conv_w, lru_conv_b, lru_wa, lru_ba, lru_wx, lru_bx, lru_lambda, od_w_in, mla_q_norm, mla_w_q_up, mla_kv_norm, mla_w_kv_up, ssd_conv_w, ssd_conv_b, ssd_dt_bias, ssd_a_log, ssd_d, ssd_onorm, moe_router, moe_w_gate, moe_w_up, moe_w_down, final_norm):
    rows = x.shape[1] // GRID_W
    cos, sin = axial_rope(rows)
    lb_all = jnp.cumsum(jax.nn.softmax(hg_lb.astype(F32), axis=1), axis=1)
    for l in range(DEPTH):
        need_ctx = l < DEPTH - 1
        j = l // 2
        mod = jax.nn.silu(c) @ w_mod[l] + b_mod[l]
        mod_c = jax.nn.silu(c_ctx) @ w_mod[l] + b_mod[l]
        sh1, sc1, g1, sh2, sc2, g2 = jnp.split(mod[:, None, :], 6, axis=-1)
        csh1, csc1, cg1, csh2, csc2, cg2 = jnp.split(mod_c, 6, axis=-1)
        h_l = rmsnorm(x, norm_mix[l]) * (1.0 + sc1) + sh1
        h_c = rmsnorm(ctx, norm_mix[l]) * (1.0 + csc1) + csh1
        wg, wu, wd = moe_w_gate[l].astype(BF16), moe_w_up[l].astype(BF16), moe_w_down[l].astype(BF16)
        if l % 2 == 0:
            w_in = ev_w_in[j]
            u_c = _mm(h_c, w_in if need_ctx else w_in[:, EV_OUT:])
            o_c, o_l = even_mixer(u_c, _mm(h_l, w_in), lb_all[0, j], lb_all[1, j], hg_onorm[j], lru_conv_w[j], lru_conv_b[j], lru_wa[j], lru_ba[j], lru_wx[j], lru_bx[j], lru_lambda[j], need_ctx)
        else:
            w_in = od_w_in[j]
            u_c = _mm(h_c, w_in[:, OD_OUT:], keep_pad=True)
            o_c, o_l = odd_mixer(u_c, _mm(h_l, w_in, keep_pad=True), mla_q_norm[j], mla_w_q_up[j], mla_kv_norm[j], mla_w_kv_up[j], ssd_conv_w[j], ssd_conv_b[j], ssd_dt_bias[j], ssd_a_log[j], ssd_d[j], ssd_onorm[j], cos, sin, need_ctx)
        x = x + g1 * _mm(o_l, w_out[l])
        x = x + g2 * ec_moe(rmsnorm(x, norm_ffn[l]) * (1.0 + sc2) + sh2, moe_router[l], wg, wu, wd)
        if need_ctx:
            ctx = ctx + cg1 * _mm(o_c, w_out[l])
            ctx = ctx + cg2 * ec_moe(rmsnorm(ctx, norm_ffn[l]) * (1.0 + csc2) + csh2, moe_router[l], wg, wu, wd)
    return rmsnorm(x, final_norm)
```

```python
import functools
import math

import jax
import jax.numpy as jnp
from jax import lax
from jax.experimental import pallas as pl
from jax.experimental.pallas import tpu as pltpu

D_MODEL = 2048
DEPTH = 2
CTX_LEN = 256
GRID_W = 64
EPS = 1e-6
F32 = jnp.float32
BF16 = jnp.bfloat16

HG_HEADS = 8
HG_DK = 128
HG_DV = 128
HG_KW = HG_HEADS * HG_DK
HG_VW = HG_HEADS * HG_DV
HG_CHUNK = 64

LRU_WIDTH = 1024
LRU_HEADS = 8
LRU_C = 8.0

MLA_HEADS = 8
MLA_Q_LORA = 512
MLA_KV_LORA = 512
MLA_NOPE = 128
MLA_ROPE = 64
MLA_V = 128
ROPE_THETA = 10000.0
ATTN_BLOCK = 128

SSD_WIDTH = 1024
SSD_HEADDIM = 64
SSD_HEADS = SSD_WIDTH // SSD_HEADDIM
SSD_GROUPS = 2
SSD_STATE = 128
SSD_BC = SSD_GROUPS * SSD_STATE
SSD_XBC = SSD_WIDTH + 2 * SSD_BC
SSD_CHUNK = 128

N_EXPERTS = 16
CAPACITY_FACTOR = 2

EV_OUT = HG_KW + HG_VW + LRU_WIDTH
OD_OUT = MLA_Q_LORA + SSD_WIDTH

V7X_VMEM_LIMIT_BYTES = 56 * 1024 * 1024


def _mm_kernel(a_ref, b_ref, o_ref):
    o_ref[0] = jnp.dot(a_ref[0].astype(BF16), b_ref[0].astype(BF16), preferred_element_type=F32)


def _pick_tile(n, target):
    if n <= target:
        return n
    t = target
    while n % t:
        t //= 2
    return t


def _lane_tile(n, target=1536):
    k = n // 128
    return 128 * max(d for d in range(1, k + 1) if k % d == 0 and 128 * d <= target)


def _bmm(a, b, tm=512, keep_pad=False):
    g, m, k = a.shape
    gb, _, n = b.shape
    tm = _pick_tile(m, tm)
    n_real = n
    cands = [(-(-n // 128) + i) * 128 for i in range(4)]
    n = next((c for c in cands if _lane_tile(c) >= 1024), max(cands, key=_lane_tile))
    if n != n_real:
        b = jnp.pad(b, ((0, 0), (0, 0), (0, n - n_real)))
    tn = _lane_tile(n)
    out = _bmm_call(a, b, g, gb, m, k, n, tm, tn)
    return out if keep_pad or n == n_real else out[..., :n_real]


def _bmm_call(a, b, g, gb, m, k, n, tm, tn):
    return pl.pallas_call(
        _mm_kernel,
        grid=(g, m // tm, n // tn),
        in_specs=[
            pl.BlockSpec((1, tm, k), lambda gi, i, j: (gi, i, 0)),
            pl.BlockSpec((1, k, tn), lambda gi, i, j: (gi % gb, 0, j)),
        ],
        out_specs=pl.BlockSpec((1, tm, tn), lambda gi, i, j: (gi, i, j)),
        out_shape=jax.ShapeDtypeStruct((g, m, n), F32),
        compiler_params=pltpu.CompilerParams(
            dimension_semantics=("parallel", "parallel", "arbitrary"),
            vmem_limit_bytes=V7X_VMEM_LIMIT_BYTES),
    )(a, b)


def _mm(a, b, keep_pad=False):
    lead = a.shape[:-1]
    m = math.prod(lead)
    out = _bmm(a.reshape(1, m, a.shape[-1]), b[None].astype(BF16), keep_pad=keep_pad)
    return out.reshape(*lead, out.shape[-1])


PROJ_TM = 512


def _proj_in_kernel(x_ref, gain_ref, sc_ref, sh_ref, w_ref, o_ref, h_ref):
    @pl.when(pl.program_id(2) == 0)
    def _():
        x = x_ref[0]
        y = x * lax.rsqrt(jnp.mean(x * x, axis=-1, keepdims=True) + EPS) * gain_ref[...]
        h_ref[...] = (y * (1.0 + sc_ref[0]) + sh_ref[0]).astype(BF16)

    o_ref[0] = jnp.dot(h_ref[...], w_ref[...], preferred_element_type=F32)


def _proj_in(x, gain, sc, sh, w):
    bsz, t, dm = x.shape
    n = w.shape[1]
    tm, tn = min(PROJ_TM, t), _lane_tile(n)
    return pl.pallas_call(
        _proj_in_kernel,
        grid=(bsz, t // tm, n // tn),
        in_specs=[
            pl.BlockSpec((1, tm, dm), lambda b, i, j: (b, i, 0)),
            pl.BlockSpec((1, dm), lambda b, i, j: (0, 0)),
            pl.BlockSpec((1, 1, dm), lambda b, i, j: (b, 0, 0)),
            pl.BlockSpec((1, 1, dm), lambda b, i, j: (b, 0, 0)),
            pl.BlockSpec((dm, tn), lambda b, i, j: (0, j)),
        ],
        out_specs=pl.BlockSpec((1, tm, tn), lambda b, i, j: (b, i, j)),
        out_shape=jax.ShapeDtypeStruct((bsz, t, n), F32),
        scratch_shapes=[pltpu.VMEM((tm, dm), BF16)],
        compiler_params=pltpu.CompilerParams(
            dimension_semantics=("parallel", "parallel", "arbitrary"), vmem_limit_bytes=V7X_VMEM_LIMIT_BYTES),
        name="proj_in",
    )(x, gain[None], sc[:, None], sh[:, None], w)


def _proj_out_kernel(y_ref, w_ref, x_ref, g_ref, o_ref):
    o_ref[0] = x_ref[0] + g_ref[0] * jnp.dot(y_ref[0].astype(BF16), w_ref[...], preferred_element_type=F32)


def _proj_out(y, w, x, g):
    bsz, t, k = y.shape
    dm = w.shape[1]
    tm, tn = min(PROJ_TM, t), _lane_tile(dm)
    return pl.pallas_call(
        _proj_out_kernel,
        grid=(bsz, t // tm, dm // tn),
        in_specs=[
            pl.BlockSpec((1, tm, k), lambda b, i, j: (b, i, 0)),
            pl.BlockSpec((k, tn), lambda b, i, j: (0, j)),
            pl.BlockSpec((1, tm, tn), lambda b, i, j: (b, i, j)),
            pl.BlockSpec((1, 1, tn), lambda b, i, j: (b, 0, j)),
        ],
        out_specs=pl.BlockSpec((1, tm, tn), lambda b, i, j: (b, i, j)),
        out_shape=jax.ShapeDtypeStruct((bsz, t, dm), F32),
        compiler_params=pltpu.CompilerParams(
            dimension_semantics=("parallel", "parallel", "arbitrary"), vmem_limit_bytes=V7X_VMEM_LIMIT_BYTES),
        name="proj_out",
    )(y, w, x, g[:, None])


LRU_BLOCK = LRU_WIDTH // LRU_HEADS
LRU_L = 128
SUBLANES = 8
LRU_PAD = SUBLANES


def _lru_kernel(seq, xc_ref, gc_ref, xl_ref, gl_ref, cw_ref, cb_ref, wa_ref, wx_ref, ba_ref, bx_ref, lam_ref,
                oc_ref, ol_ref, xs_ref, hf_ref):
    t_c, t_l = seq
    n_c, n_l = t_c // LRU_L, t_l // LRU_L
    off_c, off_l = LRU_PAD, 2 * LRU_PAD + t_c
    zeros = jnp.zeros((LRU_PAD, LRU_BLOCK), F32)
    xs_ref[0:LRU_PAD] = zeros
    xs_ref[off_c + t_c:off_l] = zeros
    xs_ref[off_l + t_l:off_l + t_l + LRU_PAD] = zeros
    xs_ref[off_c:off_c + t_c] = xc_ref[0]
    xs_ref[off_l:off_l + t_l] = xl_ref[0]

    cw = cw_ref[...]
    cb = cb_ref[...]
    row = lax.broadcasted_iota(jnp.int32, (LRU_L // SUBLANES, SUBLANES, LRU_BLOCK), 1)

    def gates(r0, d):
        win = xs_ref[pl.ds(r0 - LRU_PAD, LRU_L + 2 * LRU_PAD), :]
        x = cb
        for k in range(4):
            x = x + cw[k:k + 1] * win[LRU_PAD - 2 + k:LRU_PAD - 2 + k + LRU_L]
        xb = x.astype(BF16)
        r = jax.nn.sigmoid(jnp.dot(xb, wa_ref[d, 0].astype(BF16), preferred_element_type=F32) + ba_ref[d:d + 1])
        ig = jax.nn.sigmoid(jnp.dot(xb, wx_ref[d, 0].astype(BF16), preferred_element_type=F32) + bx_ref[d:d + 1])
        log_a = -LRU_C * r * jax.nn.softplus(-lam_ref[d:d + 1])
        a = jnp.exp(log_a)
        return a, jnp.sqrt(1.0 - a * a) * ig * x

    def chunk_scan(a, b, carry, reverse):
        n = LRU_L // SUBLANES
        a = a.reshape(n, SUBLANES, LRU_BLOCK)
        b = b.reshape(n, SUBLANES, LRU_BLOCK)
        for s in (1, 2, 4):
            if reverse:
                keep = row < SUBLANES - s
                shift = SUBLANES - s
            else:
                keep = row >= s
                shift = s
            a_n = jnp.where(keep, pltpu.roll(a, shift, 1), 1.0)
            b_n = jnp.where(keep, pltpu.roll(b, shift, 1), 0.0)
            b = a * b_n + b
            a = a * a_n
        outs = [None] * n
        last = 0 if reverse else SUBLANES - 1
        for j in (range(n - 1, -1, -1) if reverse else range(n)):
            h = b[j] + a[j] * carry
            carry = h[last:last + 1]
            outs[j] = h
        return jnp.concatenate(outs, axis=0), carry

    def fwd_chunk(t0, r0, carry):
        a, b = gates(r0, 0)
        h, carry = chunk_scan(a, b, carry, False)
        hf_ref[pl.ds(t0, LRU_L), :] = h
        return carry

    def bwd_chunk(t0, r0, carry):
        a, b = gates(r0, 1)
        h, carry = chunk_scan(a, b, carry, True)
        return hf_ref[pl.ds(t0, LRU_L), :] + h, carry

    carry = jnp.zeros((1, LRU_BLOCK), F32)
    for c in range(n_c):
        carry = fwd_chunk(c * LRU_L, off_c + c * LRU_L, carry)

    def fwd_body(c, carry):
        t0 = pl.multiple_of(c * LRU_L, LRU_L)
        return fwd_chunk(t_c + t0, off_l + t0, carry)

    lax.fori_loop(0, n_l, fwd_body, carry)

    carry = jnp.zeros((1, LRU_BLOCK), F32)
    for c in range(n_c - 1, -1, -1):
        h, carry = bwd_chunk(c * LRU_L, off_c + c * LRU_L, carry)
        oc_ref[0, c * LRU_L:(c + 1) * LRU_L, :] = h * jax.nn.gelu(gc_ref[0, c * LRU_L:(c + 1) * LRU_L, :])

    def bwd_body(i, carry):
        t0 = pl.multiple_of((n_l - 1 - i) * LRU_L, LRU_L)
        h, carry = bwd_chunk(t_c + t0, off_l + t0, carry)
        ol_ref[0, pl.ds(t0, LRU_L), :] = h * jax.nn.gelu(gl_ref[0, pl.ds(t0, LRU_L), :])
        return carry

    lax.fori_loop(0, n_l, bwd_body, carry)


def _rglru(u_c, u_l, conv_w, conv_b, wa, ba, wx, bx, lam):
    bsz, t_c, _ = u_c.shape
    t_l = u_l.shape[1]
    x_blk = (EV_OUT + 2 * HG_KW + HG_VW) // LRU_BLOCK
    g_blk = (HG_KW + HG_VW) // LRU_BLOCK
    t_pad = t_c + t_l + 3 * LRU_PAD
    vec = lambda: pl.BlockSpec((2, LRU_BLOCK), lambda b, h: (0, h))
    return pl.pallas_call(
        functools.partial(_lru_kernel, (t_c, t_l)),
        grid=(bsz, LRU_HEADS),
        in_specs=[
            pl.BlockSpec((1, t_c, LRU_BLOCK), lambda b, h: (b, 0, x_blk + h)),
            pl.BlockSpec((1, t_c, LRU_BLOCK), lambda b, h: (b, 0, g_blk + h)),
            pl.BlockSpec((1, t_l, LRU_BLOCK), lambda b, h: (b, 0, x_blk + h)),
            pl.BlockSpec((1, t_l, LRU_BLOCK), lambda b, h: (b, 0, g_blk + h)),
            pl.BlockSpec((4, LRU_BLOCK), lambda b, h: (0, h)),
            pl.BlockSpec((1, LRU_BLOCK), lambda b, h: (0, h)),
            pl.BlockSpec((2, 1, LRU_BLOCK, LRU_BLOCK), lambda b, h: (0, h, 0, 0)),
            pl.BlockSpec((2, 1, LRU_BLOCK, LRU_BLOCK), lambda b, h: (0, h, 0, 0)),
            vec(), vec(), vec(),
        ],
        out_specs=[
            pl.BlockSpec((1, t_c, LRU_BLOCK), lambda b, h: (b, 0, h)),
            pl.BlockSpec((1, t_l, LRU_BLOCK), lambda b, h: (b, 0, h)),
        ],
        out_shape=[jax.ShapeDtypeStruct((bsz, t_c, LRU_WIDTH), F32), jax.ShapeDtypeStruct((bsz, t_l, LRU_WIDTH), F32)],
        scratch_shapes=[pltpu.VMEM((t_pad, LRU_BLOCK), F32), pltpu.VMEM((t_c + t_l, LRU_BLOCK), F32)],
        compiler_params=pltpu.CompilerParams(
            dimension_semantics=("parallel", "parallel"), vmem_limit_bytes=V7X_VMEM_LIMIT_BYTES),
        name="rglru",
    )(u_c, u_c, u_l, u_l, conv_w, conv_b[None], wa, wx, ba, bx, lam)


HG_GROUP = 8


def _hgrn2_kernel(seq, qc_ref, gc_ref, ffc_ref, fbc_ref, vc_ref, ql_ref, gl_ref, ffl_ref, fbl_ref, vl_ref,
                  lbf_ref, lbb_ref, gain_ref, oc_ref, ol_ref, of_ref):
    t_c, t_l = seq
    L = HG_CHUNK
    R = HG_GROUP * L
    n_gc = t_c // L
    r_i = lax.broadcasted_iota(jnp.int32, (R, R), 0)
    c_i = lax.broadcasted_iota(jnp.int32, (R, R), 1)
    same = (r_i // L) == (c_i // L)
    causal = (same & (c_i <= r_i), same & (c_i >= r_i))
    tri = tuple(jnp.where(m, 1.0, 0.0).astype(BF16) for m in causal)
    lb = (lbf_ref[...], lbb_ref[...])
    gain = gain_ref[...]

    def group(d, n, q_raw, f_raw, v, st):
        rows = n * L
        f = lb[d] + (1.0 - lb[d]) * jax.nn.sigmoid(f_raw)
        logf = jnp.log(f)
        k = 1.0 - f
        hi = logf.astype(BF16)
        r1 = logf - hi.astype(F32)
        mid = r1.astype(BF16)
        lo = (r1 - mid.astype(F32)).astype(BF16)
        g3 = jnp.dot(tri[d][:rows, :rows], jnp.concatenate([hi, mid, lo], axis=1), preferred_element_type=F32)
        g = g3[:, :HG_DK] + g3[:, HG_DK:2 * HG_DK] + g3[:, 2 * HG_DK:]
        g4 = g.reshape(n, L, HG_DK)
        g_last = g4[:, 0:1] if d else g4[:, L - 1:L]
        g_rest = (jnp.broadcast_to(g_last, (n, L, HG_DK)) - g4).reshape(rows, HG_DK)
        qg = (jax.nn.silu(q_raw) * jnp.exp(g)).astype(BF16)
        kg = (k * jnp.exp(-g)).astype(BF16)
        kd = (k * jnp.exp(g_rest)).astype(BF16)
        vb = v.astype(BF16)
        att = lax.dot_general(qg, kg, (((1,), (1,)), ((), ())), preferred_element_type=F32)
        att = jnp.where(causal[d][:rows, :rows], att, 0.0).astype(BF16)
        o = jnp.dot(att, vb, preferred_element_type=F32)
        dec = jnp.exp(g_last)
        sl = [slice(j * L, (j + 1) * L) for j in range(n)]
        ds_t = [lax.dot_general(vb[s], kd[s], (((0,), (0,)), ((), ())), preferred_element_type=F32) for s in sl]
        entering = [None] * n
        for j in (range(n - 1, -1, -1) if d else range(n)):
            entering[j] = st.astype(BF16)
            st = dec[j] * st + ds_t[j]
        outs = [o[s] + lax.dot_general(qg[s], e, (((1,), (1,)), ((), ())), preferred_element_type=F32)
                for s, e in zip(sl, entering)]
        return jnp.concatenate(outs, axis=0), st

    def finish(o, gate):
        return _rms(o, gain) * jax.nn.silu(gate)

    def lat_rows(i):
        return pl.ds(pl.multiple_of(i * R, R), R), pl.ds(pl.multiple_of(t_c + i * R, math.gcd(t_c, R)), R)

    st = jnp.zeros((HG_DV, HG_DK), F32)
    o, st = group(0, n_gc, qc_ref[0], ffc_ref[0], vc_ref[0], st)
    of_ref[0:t_c, :] = o

    def fwd_body(i, st):
        rows, srows = lat_rows(i)
        o, st = group(0, HG_GROUP, ql_ref[0, rows, :], ffl_ref[0, rows, :], vl_ref[0, rows, :], st)
        of_ref[srows, :] = o
        return st

    lax.fori_loop(0, t_l // R, fwd_body, st)

    st = jnp.zeros((HG_DV, HG_DK), F32)
    o, st = group(1, n_gc, qc_ref[0], fbc_ref[0], vc_ref[0], st)
    oc_ref[0] = finish(of_ref[0:t_c, :] + o, gc_ref[0])

    def bwd_body(i, st):
        rows, srows = lat_rows(t_l // R - 1 - i)
        o, st = group(1, HG_GROUP, ql_ref[0, rows, :], fbl_ref[0, rows, :], vl_ref[0, rows, :], st)
        ol_ref[0, rows, :] = finish(of_ref[srows, :] + o, gl_ref[0, rows, :])
        return st

    lax.fori_loop(0, t_l // R, bwd_body, st)


def _hgrn2(u_c, u_l, lb_f, lb_b, onorm):
    bsz, t_c, _ = u_c.shape
    t_l = u_l.shape[1]
    nb = HG_KW // HG_DK
    col = lambda t, blk: pl.BlockSpec((1, t, HG_DK), lambda b, h: (b, 0, blk + h))
    seq_cols = lambda t: [col(t, 0), col(t, nb), col(t, EV_OUT // HG_DK), col(t, EV_OUT // HG_DK + nb), col(t, EV_OUT // HG_DK + 2 * nb)]
    vec = lambda: pl.BlockSpec((1, HG_DK), lambda b, h: (0, h))
    return pl.pallas_call(
        functools.partial(_hgrn2_kernel, (t_c, t_l)),
        grid=(bsz, HG_HEADS),
        in_specs=seq_cols(t_c) + seq_cols(t_l) + [vec(), vec(), vec()],
        out_specs=[
            pl.BlockSpec((1, t_c, HG_DV), lambda b, h: (b, 0, h)),
            pl.BlockSpec((1, t_l, HG_DV), lambda b, h: (b, 0, h)),
        ],
        out_shape=[jax.ShapeDtypeStruct((bsz, t_c, HG_VW), F32), jax.ShapeDtypeStruct((bsz, t_l, HG_VW), F32)],
        scratch_shapes=[pltpu.VMEM((t_c + t_l, HG_DV), F32)],
        compiler_params=pltpu.CompilerParams(
            dimension_semantics=("parallel", "parallel"), vmem_limit_bytes=V7X_VMEM_LIMIT_BYTES),
        name="hgrn2",
    )(*([u_c] * 5), *([u_l] * 5), lb_f[None], lb_b[None], onorm[None])


LANES = 128
MLA_HW = 2 * LANES
MLA_TM = 512
MLA_TQ = 512
MLA_TK = 512
MLA_UNROLL = 2
MLA_VW = MLA_V + LANES
MLA_SCALE_LOG2E = (MLA_NOPE + MLA_ROPE) ** -0.5 * math.log2(math.e)


def _rms(x, gain):
    return x * lax.rsqrt(jnp.mean(x * x, axis=-1, keepdims=True) + EPS) * gain


def _rope_tile(r, cos_t, sin_t):
    return r * cos_t + (pltpu.roll(r, MLA_ROPE // 2, 1) + pltpu.roll(r, LANES - MLA_ROPE // 2, 1)) * sin_t


def _mla_q_kernel(u_ref, gain_ref, w_ref, cos_ref, sin_ref, q_ref):
    q = jnp.dot(_rms(u_ref[0], gain_ref[...]).astype(BF16), w_ref[...], preferred_element_type=F32)
    for h in range(MLA_HEADS):
        lo = h * MLA_HW
        q_ref[0, :, lo:lo + LANES] = q[:, lo:lo + LANES].astype(BF16)
        q_ref[0, :, lo + LANES:lo + MLA_HW] = _rope_tile(q[:, lo + LANES:lo + MLA_HW], cos_ref[...], sin_ref[...]).astype(BF16)


def _mla_kv_kernel(u_ref, kr_ref, gain_ref, w_ref, cos_ref, sin_ref, k_ref, v_ref):
    kv = jnp.dot(_rms(u_ref[0], gain_ref[...]).astype(BF16), w_ref[...], preferred_element_type=F32)
    kr = _rope_tile(kr_ref[0], cos_ref[...], sin_ref[...]).astype(BF16)
    for h in range(MLA_HEADS):
        k_ref[0, :, h * MLA_HW:h * MLA_HW + LANES] = kv[:, h * LANES:(h + 1) * LANES].astype(BF16)
        k_ref[0, :, h * MLA_HW + LANES:(h + 1) * MLA_HW] = kr
    lane = lax.broadcasted_iota(jnp.int32, (kv.shape[0], LANES), 1)
    one_hot = jnp.where(lane == 0, 1.0, 0.0).astype(BF16)
    for h in range(MLA_HEADS):
        v_ref[0, :, h * MLA_VW:h * MLA_VW + MLA_V] = kv[:, (MLA_HEADS + h) * LANES:(MLA_HEADS + h + 1) * LANES].astype(BF16)
        v_ref[0, :, h * MLA_VW + MLA_V:(h + 1) * MLA_VW] = one_hot


def _mla_q(u, gain, w, cos_t, sin_t):
    bsz, t, _ = u.shape
    tm = min(MLA_TM, t)
    return pl.pallas_call(
        _mla_q_kernel,
        grid=(bsz, t // tm),
        in_specs=[
            pl.BlockSpec((1, tm, MLA_Q_LORA), lambda b, i: (b, i, 0)),
            pl.BlockSpec((1, MLA_Q_LORA), lambda b, i: (0, 0)),
            pl.BlockSpec((MLA_Q_LORA, MLA_HEADS * MLA_HW), lambda b, i: (0, 0)),
            pl.BlockSpec((tm, LANES), lambda b, i: (i, 0)),
            pl.BlockSpec((tm, LANES), lambda b, i: (i, 0)),
        ],
        out_specs=pl.BlockSpec((1, tm, MLA_HEADS * MLA_HW), lambda b, i: (b, i, 0)),
        out_shape=jax.ShapeDtypeStruct((bsz, t, MLA_HEADS * MLA_HW), BF16),
        compiler_params=pltpu.CompilerParams(
            dimension_semantics=("parallel", "parallel"), vmem_limit_bytes=V7X_VMEM_LIMIT_BYTES),
        name="mla_q_up",
    )(u, gain[None], w, cos_t, sin_t)


def _mla_kv(u, c_blk, r_blk, gain, w, cos_t, sin_t):
    bsz, t, _ = u.shape
    tm = min(MLA_TM, t)
    return pl.pallas_call(
        _mla_kv_kernel,
        grid=(bsz, t // tm),
        in_specs=[
            pl.BlockSpec((1, tm, MLA_KV_LORA), lambda b, i: (b, i, c_blk)),
            pl.BlockSpec((1, tm, LANES), lambda b, i: (b, i, r_blk)),
            pl.BlockSpec((1, MLA_KV_LORA), lambda b, i: (0, 0)),
            pl.BlockSpec((MLA_KV_LORA, MLA_HEADS * (LANES + MLA_V)), lambda b, i: (0, 0)),
            pl.BlockSpec((tm, LANES), lambda b, i: (i, 0)),
            pl.BlockSpec((tm, LANES), lambda b, i: (i, 0)),
        ],
        out_specs=[
            pl.BlockSpec((1, tm, MLA_HEADS * MLA_HW), lambda b, i: (b, i, 0)),
            pl.BlockSpec((1, tm, MLA_HEADS * MLA_VW), lambda b, i: (b, i, 0)),
        ],
        out_shape=[jax.ShapeDtypeStruct((bsz, t, MLA_HEADS * MLA_HW), BF16),
                   jax.ShapeDtypeStruct((bsz, t, MLA_HEADS * MLA_VW), BF16)],
        compiler_params=pltpu.CompilerParams(
            dimension_semantics=("parallel", "parallel"), vmem_limit_bytes=V7X_VMEM_LIMIT_BYTES),
        name="mla_kv_up",
    )(u, u, gain[None], w, cos_t, sin_t)


def _flash_kernel(t_l, q_ref, kc_ref, vc_ref, kl_ref, vl_ref, o_ref):
    q = q_ref[0]

    tiles = [(kc_ref, vc_ref, slice(None))] + [(kl_ref, vl_ref, slice(i * MLA_TK, (i + 1) * MLA_TK)) for i in range(t_l // MLA_TK)]

    def scores(tile):
        k_ref, _, rows = tile
        return lax.dot_general(q, k_ref[0, rows, :], (((1,), (1,)), ((), ())), preferred_element_type=F32)

    tq = q.shape[0]
    m = jnp.full((tq, 1), -jnp.inf, F32)
    acc = jnp.zeros((tq, MLA_VW), F32)
    s_next = scores(tiles[0])
    for i, (_, v_ref, rows) in enumerate(tiles):
        s = s_next * MLA_SCALE_LOG2E
        if i + 1 < len(tiles):
            s_next = scores(tiles[i + 1])
        m_new = jnp.maximum(m, jnp.max(s, axis=-1, keepdims=True))
        p = jnp.exp2(s - m_new)
        acc = jnp.exp2(m - m_new) * acc + jnp.dot(p.astype(BF16), v_ref[0, rows, :], preferred_element_type=F32)
        m = m_new
    o_ref[0] = acc[:, :MLA_V] / acc[:, MLA_V:MLA_V + 1]


def _flash(q, k_c, v_c, k_l, v_l):
    bsz, t, _ = q.shape
    t_c, t_l = k_c.shape[1], k_l.shape[1]
    return pl.pallas_call(
        functools.partial(_flash_kernel, t_l),
        grid=(bsz, MLA_HEADS, t // MLA_TQ),
        in_specs=[
            pl.BlockSpec((1, MLA_TQ, MLA_HW), lambda b, h, i: (b, i, h)),
            pl.BlockSpec((1, t_c, MLA_HW), lambda b, h, i: (b, 0, h)),
            pl.BlockSpec((1, t_c, MLA_VW), lambda b, h, i: (b, 0, h)),
            pl.BlockSpec((1, t_l, MLA_HW), lambda b, h, i: (b, 0, h)),
            pl.BlockSpec((1, t_l, MLA_VW), lambda b, h, i: (b, 0, h)),
        ],
        out_specs=pl.BlockSpec((1, MLA_TQ, MLA_V), lambda b, h, i: (b, i, h)),
        out_shape=jax.ShapeDtypeStruct((bsz, t, MLA_HEADS * MLA_V), F32),
        compiler_params=pltpu.CompilerParams(
            dimension_semantics=("parallel", "parallel", "arbitrary"), vmem_limit_bytes=V7X_VMEM_LIMIT_BYTES),
        name="mla_flash",
    )(q, k_c, v_c, k_l, v_l)


def _mla(u_c, u_l, q_norm, w_q_up, kv_norm, w_kv_up, cos, sin):
    t_c, t_l = u_c.shape[1], u_l.shape[1]
    zeros = jnp.zeros((t_l, LANES - MLA_ROPE), F32)
    cos_l = jnp.concatenate([cos, cos, zeros], axis=-1)
    sin_l = jnp.concatenate([-sin, sin, zeros], axis=-1)
    cos_c = jnp.concatenate([jnp.ones((t_c, MLA_ROPE), F32), zeros[:t_c]], axis=-1)
    sin_c = jnp.zeros((t_c, LANES), F32)
    wq = w_q_up.reshape(MLA_Q_LORA, MLA_HEADS, MLA_NOPE + MLA_ROPE)
    wq = jnp.pad(wq, ((0, 0), (0, 0), (0, MLA_HW - MLA_NOPE - MLA_ROPE))).reshape(MLA_Q_LORA, -1).astype(BF16)
    wkv = w_kv_up.reshape(MLA_KV_LORA, MLA_HEADS, MLA_NOPE + MLA_V)
    wkv = jnp.concatenate([wkv[..., :MLA_NOPE].reshape(MLA_KV_LORA, -1), wkv[..., MLA_NOPE:].reshape(MLA_KV_LORA, -1)], axis=-1).astype(BF16)
    q = _mla_q(u_l, q_norm, wq, cos_l, sin_l)
    k_l, v_l = _mla_kv(u_l, OD_OUT // MLA_KV_LORA, (OD_OUT + MLA_KV_LORA) // LANES, kv_norm, wkv, cos_l, sin_l)
    k_c, v_c = _mla_kv(u_c, 0, MLA_KV_LORA // LANES, kv_norm, wkv, cos_c, sin_c)
    return _flash(q, k_c, v_c, k_l, v_l)


OD_BLK_Z = MLA_Q_LORA // LANES
OD_BLK_CKV = OD_OUT // LANES
OD_BLK_KR = OD_BLK_CKV + MLA_KV_LORA // LANES
OD_BLK_XBC = OD_BLK_KR + 1
SSD_XBC_BLKS = SSD_XBC // LANES
SSD_ACT_BLKS = SSD_XBC_BLKS + SSD_GROUPS
SSD_GH = SSD_HEADS // SSD_GROUPS
SSD_GW = SSD_WIDTH // SSD_GROUPS
SSD_TILE = 2 * SSD_CHUNK
SSD_CONV_ROWS = 512


def _od_weight_layout(w_in, ctx_only):
    k = w_in.shape[0]
    z = lambda n: jnp.zeros((k, n), w_in.dtype)
    o = OD_OUT + MLA_KV_LORA
    xbc0 = o + MLA_ROPE
    dt0 = xbc0 + SSD_XBC
    cols = [] if ctx_only else [w_in[:, :OD_OUT]]
    cols += [w_in[:, OD_OUT:o], w_in[:, o:xbc0], z(LANES - MLA_ROPE), w_in[:, xbc0:dt0]]
    for g in range(SSD_GROUPS):
        cols += [w_in[:, dt0 + g * SSD_GH:dt0 + (g + 1) * SSD_GH],
                 w_in[:, dt0 + SSD_HEADS + g * SSD_GH:dt0 + SSD_HEADS + (g + 1) * SSD_GH], z(LANES - 2 * SSD_GH)]
    w = jnp.concatenate(cols, axis=1)
    return jnp.pad(w, ((0, 0), (0, -w.shape[1] % (10 * LANES) if ctx_only else -w.shape[1] % (8 * LANES))))


def _ssd_group_lanes(v):
    rows = [jnp.concatenate([v[0, g * SSD_GH:(g + 1) * SSD_GH], v[1, g * SSD_GH:(g + 1) * SSD_GH],
                             jnp.zeros((LANES - 2 * SSD_GH,), v.dtype)]) for g in range(SSD_GROUPS)]
    return jnp.stack(rows)


def _ssd_act_kernel(seq, xc_ref, xl_ref, cw_ref, cb_ref, o_ref, xs_ref):
    t_c, t_l = seq
    j = pl.program_id(1)

    @pl.when(j < SSD_XBC_BLKS)
    def _():
        off_c, off_l = LRU_PAD, 2 * LRU_PAD + t_c
        zeros = jnp.zeros((LRU_PAD, LANES), F32)
        xs_ref[0:LRU_PAD] = zeros
        xs_ref[off_c + t_c:off_l] = zeros
        xs_ref[off_l + t_l:off_l + t_l + LRU_PAD] = zeros
        xs_ref[off_c:off_c + t_c] = xc_ref[0]
        xs_ref[off_l:off_l + t_l] = xl_ref[0]
        cw = cw_ref[...]
        cb = cb_ref[...]

        def conv(r0, rows):
            win = xs_ref[pl.ds(r0 - LRU_PAD, rows + 2 * LRU_PAD), :]
            y = cb
            for k in range(4):
                y = y + cw[k:k + 1] * win[LRU_PAD - 2 + k:LRU_PAD - 2 + k + rows]
            return jax.nn.silu(y)

        o_ref[0, 0:t_c, :] = conv(off_c, t_c)

        def body(i, _):
            r0 = pl.multiple_of(i * SSD_CONV_ROWS, SSD_CONV_ROWS)
            o_ref[0, pl.ds(t_c + r0, SSD_CONV_ROWS), :] = conv(off_l + r0, SSD_CONV_ROWS)
            return 0

        lax.fori_loop(0, t_l // SSD_CONV_ROWS, body, 0)

    @pl.when(j >= SSD_XBC_BLKS)
    def _():
        o_ref[0, 0:t_c, :] = jax.nn.softplus(xc_ref[0] + cb_ref[...])
        o_ref[0, t_c:t_c + t_l, :] = jax.nn.softplus(xl_ref[0] + cb_ref[...])


def _ssd_act(u_c, u_l, conv_w, conv_b, dt_bias):
    bsz, t_c, _ = u_c.shape
    t_l = u_l.shape[1]
    cw = jnp.pad(conv_w, ((0, 0), (0, SSD_GROUPS * LANES)))
    cb = jnp.concatenate([conv_b, _ssd_group_lanes(dt_bias).reshape(-1)])[None]
    ctx_blk = OD_BLK_XBC - OD_BLK_CKV
    return pl.pallas_call(
        functools.partial(_ssd_act_kernel, (t_c, t_l)),
        grid=(bsz, SSD_ACT_BLKS),
        in_specs=[
            pl.BlockSpec((1, t_c, LANES), lambda b, j: (b, 0, ctx_blk + j)),
            pl.BlockSpec((1, t_l, LANES), lambda b, j: (b, 0, OD_BLK_XBC + j)),
            pl.BlockSpec((4, LANES), lambda b, j: (0, j)),
            pl.BlockSpec((1, LANES), lambda b, j: (0, j)),
        ],
        out_specs=pl.BlockSpec((1, t_c + t_l, LANES), lambda b, j: (b, 0, j)),
        out_shape=jax.ShapeDtypeStruct((bsz, t_c + t_l, SSD_ACT_BLKS * LANES), F32),
        scratch_shapes=[pltpu.VMEM((t_c + t_l + 3 * LRU_PAD, LANES), F32)],
        compiler_params=pltpu.CompilerParams(
            dimension_semantics=("parallel", "parallel"), vmem_limit_bytes=V7X_VMEM_LIMIT_BYTES),
        name="ssd_act",
    )(u_c, u_l, cw, cb)


def _split3(x):
    hi = x.astype(BF16)
    r1 = x - hi.astype(F32)
    mid = r1.astype(BF16)
    return hi, mid, (r1 - mid.astype(F32)).astype(BF16)


def _ssd_scan_kernel(d, n_tiles, *refs):
    if d == 0:
        x_ref, b_ref, c_ref, dt_ref, alog_ref, y_ref, s_ref = refs
    else:
        x_ref, b_ref, c_ref, dt_ref, alog_ref, yf_ref, z_ref, dskip_ref, gain_ref, y_ref, s_ref = refs
    L = SSD_CHUNK
    P = SSD_HEADDIM

    @pl.when(pl.program_id(2) == 0)
    def _():
        s_ref[...] = jnp.zeros_like(s_ref)

    r_i = lax.broadcasted_iota(jnp.int32, (L, L), 0)
    c_i = lax.broadcasted_iota(jnp.int32, (L, L), 1)
    causal = (c_i >= r_i) if d else (c_i <= r_i)
    tri = jnp.where(causal, 1.0, 0.0).astype(BF16)
    low_half = lax.broadcasted_iota(jnp.int32, (L, LANES), 1) < P
    a_lane = -jnp.exp(alog_ref[...])

    def pair_cols(cols, p):
        return jnp.where(low_half, cols[2 * p], cols[2 * p + 1])

    for ch in ((1, 0) if d else (0, 1)):
        rows = slice(ch * L, (ch + 1) * L)
        xs = x_ref[0, rows, :]
        bm = b_ref[0, rows, :].astype(BF16)
        cm = c_ref[0, rows, :].astype(BF16)
        dt = dt_ref[0, rows, :]
        a3 = _split3(dt * a_lane)
        cs3 = jnp.dot(tri, jnp.concatenate(a3, axis=1), preferred_element_type=F32)
        cs_col = cs3[:, :LANES] + cs3[:, LANES:2 * LANES] + cs3[:, 2 * LANES:]
        cs_row = sum(lax.dot_general(t, tri, (((0,), (1,)), ((), ())), preferred_element_type=F32) for t in a3)
        cb = lax.dot_general(cm, bm, (((1,), (1,)), ((), ())), preferred_element_type=F32)
        s_in = s_ref[...].astype(BF16)
        y_off = lax.dot_general(cm, s_in, (((1,), (1,)), ((), ())), preferred_element_type=F32)
        last = 0 if d else L - 1
        cols, dts, lasts = [], [], []
        for i in range(SSD_GH):
            lane = SSD_GH * d + i
            cols.append(cs_col[:, lane:lane + 1])
            dts.append(dt[:, lane:lane + 1])
            lasts.append(cs_col[last:last + 1, lane:lane + 1])
        ys, xdecs = [], []
        for p in range(SSD_GH // 2):
            xp = xs[:, p * LANES:(p + 1) * LANES]
            col2 = pair_cols(cols, p)
            xdt = xp * pair_cols(dts, p)
            xdt_b = xdt.astype(BF16)
            y2 = []
            for e in range(2):
                i = 2 * p + e
                lane = SSD_GH * d + i
                decay = jnp.where(causal, jnp.exp(cols[i] - cs_row[lane:lane + 1, :]), 0.0)
                y2.append(jnp.dot((cb * decay).astype(BF16), xdt_b, preferred_element_type=F32))
            ys.append(jnp.where(low_half, y2[0], y2[1]) + y_off[:, p * LANES:(p + 1) * LANES] * jnp.exp(col2))
            last2 = jnp.where(low_half, lasts[2 * p], lasts[2 * p + 1])
            xdecs.append((xdt * jnp.exp(last2 - col2)).astype(BF16))
        y = jnp.concatenate(ys, axis=1)
        ds = lax.dot_general(jnp.concatenate(xdecs, axis=1), bm, (((0,), (0,)), ((), ())), preferred_element_type=F32)
        dec = jnp.concatenate([jnp.broadcast_to(jnp.exp(lasts[i]), (P, SSD_STATE)) for i in range(SSD_GH)], axis=0)
        s_ref[...] = dec * s_ref[...] + ds
        if d == 0:
            y_ref[0, rows, :] = y
        else:
            out = (yf_ref[0, rows, :] + y + dskip_ref[...] * xs) * jax.nn.silu(z_ref[0, rows, :])
            y_ref[0, rows, :] = _rms(out, gain_ref[...])


def _ssd_scan(d, act, t_c, a_lanes, extra=()):
    bsz, t, _ = act.shape
    n_tiles = t // SSD_TILE
    n_ctx = t_c // SSD_TILE
    assert n_ctx == 1
    if d == 0:
        tile = lambda i: i
    else:
        tile = lambda i: jnp.where(i < n_ctx, n_ctx - 1 - i, n_tiles + n_ctx - 1 - i)
    in_specs = [
        pl.BlockSpec((1, SSD_TILE, SSD_GW), lambda b, g, i: (b, tile(i), g)),
        pl.BlockSpec((1, SSD_TILE, LANES), lambda b, g, i: (b, tile(i), SSD_WIDTH // LANES + g)),
        pl.BlockSpec((1, SSD_TILE, LANES), lambda b, g, i: (b, tile(i), (SSD_WIDTH + SSD_BC) // LANES + g)),
        pl.BlockSpec((1, SSD_TILE, LANES), lambda b, g, i: (b, tile(i), SSD_XBC_BLKS + g)),
        pl.BlockSpec((1, LANES), lambda b, g, i: (0, g)),
    ]
    args = [act, act, act, act, a_lanes.reshape(1, -1)]
    if d:
        y_f, u_l, d_skip, onorm = extra
        in_specs += [
            pl.BlockSpec((1, SSD_TILE, SSD_GW), lambda b, g, i: (b, tile(i), g)),
            pl.BlockSpec((1, SSD_TILE, SSD_GW), lambda b, g, i: (b, jnp.maximum(tile(i) - n_ctx, 0), OD_BLK_Z * LANES // SSD_GW + g)),
            pl.BlockSpec((1, SSD_GW), lambda b, g, i: (0, g)),
            pl.BlockSpec((1, SSD_GW), lambda b, g, i: (0, g)),
        ]
        args += [y_f, u_l, jnp.repeat(d_skip, SSD_HEADDIM)[None], onorm[None]]
    return pl.pallas_call(
        functools.partial(_ssd_scan_kernel, d, n_tiles),
        grid=(bsz, SSD_GROUPS, n_tiles),
        in_specs=in_specs,
        out_specs=pl.BlockSpec((1, SSD_TILE, SSD_GW), lambda b, g, i: (b, tile(i), g)),
        out_shape=jax.ShapeDtypeStruct((bsz, t, SSD_WIDTH), F32),
        scratch_shapes=[pltpu.VMEM((SSD_GW, SSD_STATE), F32)],
        compiler_params=pltpu.CompilerParams(
            dimension_semantics=("parallel", "parallel", "arbitrary"), vmem_limit_bytes=V7X_VMEM_LIMIT_BYTES),
        name="ssd_scan_bwd" if d else "ssd_scan_fwd",
    )(*args)


def _ssd(u_c, u_l, conv_w, conv_b, dt_bias, a_log, d_skip, onorm):
    act = _ssd_act(u_c, u_l, conv_w, conv_b, dt_bias)
    a_lanes = _ssd_group_lanes(a_log)
    y_f = _ssd_scan(0, act, u_c.shape[1], a_lanes)
    return _ssd_scan(1, act, u_c.shape[1], a_lanes, (y_f, u_l, d_skip, onorm))


def rmsnorm(x, g):
    xf = x.astype(F32)
    y = xf * lax.rsqrt(jnp.mean(xf * xf, axis=-1, keepdims=True) + EPS)
    return (y * g.astype(F32)).astype(x.dtype)


def heads(t, h):
    return t.reshape(t.shape[0], t.shape[1], h, t.shape[-1] // h)


def flip(t):
    return jnp.flip(t, axis=1)


def dwconv(x, w, b):
    k = w.shape[0]
    y = lax.conv_general_dilated(x, w[:, None, :].astype(x.dtype), window_strides=(1,), padding=[(k // 2, k - 1 - k // 2)], dimension_numbers=('NWC', 'WIO', 'NWC'), feature_group_count=x.shape[-1])
    return y + b.astype(x.dtype)


def axial_rope(rows):
    row = jnp.repeat(jnp.arange(rows, dtype=F32), GRID_W)
    col = jnp.arange(rows * GRID_W) % GRID_W
    n_freq = MLA_ROPE // 4
    inv = ROPE_THETA ** (-jnp.arange(n_freq, dtype=F32) / n_freq)
    ang = jnp.concatenate([row[:, None] * inv, col.astype(F32)[:, None] * inv], axis=-1)
    return jnp.cos(ang), jnp.sin(ang)


def apply_rope(t, cos, sin):
    t1, t2 = jnp.split(t.astype(F32), 2, axis=-1)
    cs, sn = cos[None, :, None, :], sin[None, :, None, :]
    return jnp.concatenate([t1 * cs - t2 * sn, t1 * sn + t2 * cs], axis=-1).astype(t.dtype)


def segsum(x):
    t = x.shape[-1]
    xx = jnp.broadcast_to(x[..., :, None], x.shape + (t,))
    xx = jnp.where(jnp.tril(jnp.ones((t, t), bool), -1), xx, 0.0)
    cs = jnp.cumsum(xx, axis=-2)
    return jnp.where(jnp.tril(jnp.ones((t, t), bool)), cs, -jnp.inf)


def gla_chunked(q, k, v, logf, s0):
    b, t, h, _ = k.shape
    dv = v.shape[-1]
    n = t // HG_CHUNK
    blk = lambda a: a.reshape(b, n, HG_CHUNK, h, a.shape[-1])
    k, v, g = blk(k), blk(v), jnp.cumsum(blk(logf), axis=2)
    g_last = g[:, :, -1:]
    ds = jnp.einsum('bnmhd,bnmhe->bnhde', k * jnp.exp(g_last - g), v)
    decay = jnp.exp(g_last[:, :, 0])

    def step(s, inp):
        dec, d = inp
        return dec[..., None] * s + d, s

    s_fin, s_prev = lax.scan(step, s0, (jnp.moveaxis(decay, 1, 0), jnp.moveaxis(ds, 1, 0)))
    if q is None:
        return None, s_fin
    qg = blk(q) * jnp.exp(g)
    att = jnp.einsum('bnlhd,bnmhd->bnhlm', qg, k * jnp.exp(-g))
    att = jnp.where(jnp.tril(jnp.ones((HG_CHUNK, HG_CHUNK), bool)), att, 0.0)
    o = jnp.einsum('bnhlm,bnmhe->bnlhe', att, v) + jnp.einsum('bnlhd,nbhde->bnlhe', qg, s_prev)
    return o.reshape(b, t, h, dv), s_fin


def hgrn2_scan(q_c, f_c, i_c, q_l, f_l, i_l, lb):
    def gates(fr):
        f = lb + (1.0 - lb) * jax.nn.sigmoid(fr.astype(F32))
        return heads(jnp.log(f), HG_HEADS), heads(1.0 - f, HG_HEADS)

    s0 = jnp.zeros((f_l.shape[0], HG_HEADS, HG_DK, HG_DV), F32)
    lf, k = gates(f_c)
    o_c, s_c = gla_chunked(q_c, k, i_c, lf, s0)
    lf, k = gates(f_l)
    o_l, _ = gla_chunked(q_l, k, i_l, lf, s_c)
    return o_c, o_l


def even_mixer(u_c, u_l, lb_f, lb_b, onorm, conv_w, conv_b, wa, ba, wx, bx, lam, need_ctx):
    assert need_ctx
    lru_c, lru_l = _rglru(u_c, u_l, conv_w, conv_b, wa, ba, wx, bx, lam)
    hg_c, hg_l = _hgrn2(u_c, u_l, lb_f, lb_b, onorm)
    return jnp.concatenate([hg_c, lru_c], axis=-1), jnp.concatenate([hg_l, lru_l], axis=-1)


def block_attention(q, k, v):
    b, t, h, dq = q.shape
    nb = t // ATTN_BLOCK
    scale = dq ** -0.5
    qb = jnp.moveaxis(q.reshape(b, nb, ATTN_BLOCK, h, dq), 1, 0)

    def one(qblk):
        s = jnp.einsum('bqhd,bkhd->bhqk', qblk, k).astype(F32) * scale
        p = jax.nn.softmax(s, axis=-1).astype(v.dtype)
        return jnp.einsum('bhqk,bkhe->bqhe', p, v)

    o = lax.map(one, qb)
    return jnp.moveaxis(o, 0, 1).reshape(b, t, h, v.shape[-1])


def ssd_chunked(x, dt, a, bm, cm, s0, want_y):
    b, t, h, p = x.shape
    n_c = t // SSD_CHUNK
    rep = h // bm.shape[2]
    blk = lambda z: z.reshape(b, n_c, SSD_CHUNK, *z.shape[2:])
    xdt = blk(x * dt[..., None])
    bh = blk(jnp.repeat(bm, rep, axis=2))
    a_dt = jnp.moveaxis(blk(dt * a), 3, 1)
    a_cs = jnp.cumsum(a_dt, axis=-1)
    states = jnp.einsum('bclhn,bhcl,bclhp->bchpn', bh, jnp.exp(a_cs[..., -1:] - a_cs), xdt)
    states = jnp.concatenate([s0[:, None], states], axis=1)
    chunk_decay = jnp.exp(segsum(jnp.pad(a_cs[..., -1], ((0, 0), (0, 0), (1, 0)))))
    new_states = jnp.einsum('bhzc,bchpn->bzhpn', chunk_decay, states)
    if not want_y:
        return None, new_states[:, -1]
    ch = blk(jnp.repeat(cm, rep, axis=2))
    scores = jnp.einsum('bclhn,bcshn->bhcls', ch, bh) * jnp.exp(segsum(a_dt))
    y = jnp.einsum('bhcls,bcshp->bclhp', scores, xdt) + jnp.einsum('bclhn,bchpn,bhcl->bclhp', ch, new_states[:, :-1], jnp.exp(a_cs))
    return y.reshape(b, t, h, p), new_states[:, -1]


def odd_mixer(u_c, u_l, q_norm, w_q_up, kv_norm, w_kv_up, conv_w, conv_b, dt_bias, a_log, d_skip, onorm, cos, sin, need_ctx):
    assert not need_ctx
    att_l = _mla(u_c, u_l, q_norm, w_q_up, kv_norm, w_kv_up, cos, sin)
    ssd = _ssd(u_c, u_l, conv_w, conv_b, dt_bias, a_log, d_skip, onorm)
    return None, jnp.concatenate([att_l, ssd[:, u_c.shape[1]:]], axis=-1)


def ec_moe(h, w_router, w_gate, w_up, w_down):
    b, t, d = h.shape
    cap = CAPACITY_FACTOR * t // N_EXPERTS
    aff = jax.nn.softmax((h @ w_router).astype(F32), axis=-1)
    gate, idx = lax.top_k(jnp.swapaxes(aff, 1, 2), cap)
    bidx = jnp.arange(b)[:, None, None]
    xe = h[bidx, idx].reshape(b * N_EXPERTS, cap, d)
    hid = jax.nn.silu(_bmm(xe, w_gate)) * _bmm(xe, w_up)
    ye = _bmm(hid, w_down).reshape(b, N_EXPERTS, cap, d) * gate[..., None].astype(h.dtype)
    return jnp.zeros_like(h).at[bidx, idx].add(ye)


def kernel(x, c, ctx, c_ctx, w_mod, b_mod, norm_mix, norm_ffn, w_out, ev_w_in, hg_lb, hg_onorm, lru_conv_w, lru_conv_b, lru_wa, lru_ba, lru_wx, lru_bx, lru_lambda, od_w_in, mla_q_norm, mla_w_q_up, mla_kv_norm, mla_w_kv_up, ssd_conv_w, ssd_conv_b, ssd_dt_bias, ssd_a_log, ssd_d, ssd_onorm, moe_router, moe_w_gate, moe_w_up, moe_w_down, final_norm):
    rows = x.shape[1] // GRID_W
    cos, sin = axial_rope(rows)
    lb_all = jnp.cumsum(jax.nn.softmax(hg_lb.astype(F32), axis=1), axis=1)
    for l in range(DEPTH):
        need_ctx = l < DEPTH - 1
        j = l // 2
        mod = jax.nn.silu(c) @ w_mod[l] + b_mod[l]
        mod_c = jax.nn.silu(c_ctx) @ w_mod[l] + b_mod[l]
        sh1, sc1, g1, sh2, sc2, g2 = jnp.split(mod, 6, axis=-1)
        csh1, csc1, cg1, csh2, csc2, cg2 = (jnp.broadcast_to(v, (x.shape[0], D_MODEL)) for v in jnp.split(mod_c, 6, axis=-1))
        wg, wu, wd = moe_w_gate[l].astype(BF16), moe_w_up[l].astype(BF16), moe_w_down[l].astype(BF16)
        wo = w_out[l].astype(BF16)
        if l % 2 == 0:
            w_in = ev_w_in[j].astype(BF16)
            u_c = _proj_in(ctx, norm_mix[l], csc1, csh1, w_in)
            u_l = _proj_in(x, norm_mix[l], sc1, sh1, w_in)
            o_c, o_l = even_mixer(u_c, u_l, lb_all[0, j], lb_all[1, j], hg_onorm[j], lru_conv_w[j], lru_conv_b[j], lru_wa[j], lru_ba[j], lru_wx[j], lru_bx[j], lru_lambda[j], need_ctx)
        else:
            w_in = od_w_in[j]
            u_c = _proj_in(ctx, norm_mix[l], csc1, csh1, _od_weight_layout(w_in, True).astype(BF16))
            u_l = _proj_in(x, norm_mix[l], sc1, sh1, _od_weight_layout(w_in, False).astype(BF16))
            o_c, o_l = odd_mixer(u_c, u_l, mla_q_norm[j], mla_w_q_up[j], mla_kv_norm[j], mla_w_kv_up[j], ssd_conv_w[j], ssd_conv_b[j], ssd_dt_bias[j], ssd_a_log[j], ssd_d[j], ssd_onorm[j], cos, sin, need_ctx)
        x = _proj_out(o_l, wo, x, g1)
        x = x + g2[:, None] * ec_moe(rmsnorm(x, norm_ffn[l]) * (1.0 + sc2[:, None]) + sh2[:, None], moe_router[l], wg, wu, wd)
        if need_ctx:
            ctx = _proj_out(o_c, wo, ctx, cg1)
            ctx = ctx + cg2[:, None] * ec_moe(rmsnorm(ctx, norm_ffn[l]) * (1.0 + csc2[:, None]) + csh2[:, None], moe_router[l], wg, wu, wd)
    return rmsnorm(x, final_norm)
```

```python
import functools
import math

import jax
import jax.numpy as jnp
from jax import lax
from jax.experimental import pallas as pl
from jax.experimental.pallas import tpu as pltpu

D_MODEL = 2048
DEPTH = 2
CTX_LEN = 256
GRID_W = 64
EPS = 1e-6
F32 = jnp.float32
BF16 = jnp.bfloat16

HG_HEADS = 8
HG_DK = 128
HG_DV = 128
HG_KW = HG_HEADS * HG_DK
HG_VW = HG_HEADS * HG_DV
HG_CHUNK = 64

LRU_WIDTH = 1024
LRU_HEADS = 8
LRU_C = 8.0

MLA_HEADS = 8
MLA_Q_LORA = 512
MLA_KV_LORA = 512
MLA_NOPE = 128
MLA_ROPE = 64
MLA_V = 128
ROPE_THETA = 10000.0
ATTN_BLOCK = 128

SSD_WIDTH = 1024
SSD_HEADDIM = 64
SSD_HEADS = SSD_WIDTH // SSD_HEADDIM
SSD_GROUPS = 2
SSD_STATE = 128
SSD_BC = SSD_GROUPS * SSD_STATE
SSD_XBC = SSD_WIDTH + 2 * SSD_BC
SSD_CHUNK = 128

N_EXPERTS = 16
CAPACITY_FACTOR = 2

EV_OUT = HG_KW + HG_VW + LRU_WIDTH
OD_OUT = MLA_Q_LORA + SSD_WIDTH

V7X_VMEM_LIMIT_BYTES = 56 * 1024 * 1024


def _mm_kernel(a_ref, b_ref, o_ref):
    o_ref[0] = jnp.dot(a_ref[0].astype(BF16), b_ref[0].astype(BF16), preferred_element_type=F32)


def _pick_tile(n, target):
    if n <= target:
        return n
    t = target
    while n % t:
        t //= 2
    return t


def _lane_tile(n, target=1536):
    k = n // 128
    return 128 * max(d for d in range(1, k + 1) if k % d == 0 and 128 * d <= target)


def _bmm(a, b, tm=512, keep_pad=False):
    g, m, k = a.shape
    gb, _, n = b.shape
    tm = _pick_tile(m, tm)
    n_real = n
    cands = [(-(-n // 128) + i) * 128 for i in range(4)]
    n = next((c for c in cands if _lane_tile(c) >= 1024), max(cands, key=_lane_tile))
    if n != n_real:
        b = jnp.pad(b, ((0, 0), (0, 0), (0, n - n_real)))
    tn = _lane_tile(n)
    out = _bmm_call(a, b, g, gb, m, k, n, tm, tn)
    return out if keep_pad or n == n_real else out[..., :n_real]


def _bmm_call(a, b, g, gb, m, k, n, tm, tn):
    return pl.pallas_call(
        _mm_kernel,
        grid=(g, m // tm, n // tn),
        in_specs=[
            pl.BlockSpec((1, tm, k), lambda gi, i, j: (gi, i, 0)),
            pl.BlockSpec((1, k, tn), lambda gi, i, j: (gi % gb, 0, j)),
        ],
        out_specs=pl.BlockSpec((1, tm, tn), lambda gi, i, j: (gi, i, j)),
        out_shape=jax.ShapeDtypeStruct((g, m, n), F32),
        compiler_params=pltpu.CompilerParams(
            dimension_semantics=("parallel", "parallel", "arbitrary"),
            vmem_limit_bytes=V7X_VMEM_LIMIT_BYTES),
    )(a, b)


def _mm(a, b, keep_pad=False):
    lead = a.shape[:-1]
    m = math.prod(lead)
    out = _bmm(a.reshape(1, m, a.shape[-1]), b[None].astype(BF16), keep_pad=keep_pad)
    return out.reshape(*lead, out.shape[-1])


PROJ_TM = 512


def _proj_in_kernel(x_ref, gain_ref, sc_ref, sh_ref, w_ref, o_ref, h_ref):
    @pl.when(pl.program_id(2) == 0)
    def _():
        x = x_ref[0]
        y = x * lax.rsqrt(jnp.mean(x * x, axis=-1, keepdims=True) + EPS) * gain_ref[...]
        h_ref[...] = (y * (1.0 + sc_ref[0]) + sh_ref[0]).astype(BF16)

    o_ref[0] = jnp.dot(h_ref[...], w_ref[...], preferred_element_type=F32)


def _proj_in(x, gain, sc, sh, w):
    bsz, t, dm = x.shape
    n = w.shape[1]
    tm, tn = min(PROJ_TM, t), _lane_tile(n)
    return pl.pallas_call(
        _proj_in_kernel,
        grid=(bsz, t // tm, n // tn),
        in_specs=[
            pl.BlockSpec((1, tm, dm), lambda b, i, j: (b, i, 0)),
            pl.BlockSpec((1, dm), lambda b, i, j: (0, 0)),
            pl.BlockSpec((1, 1, dm), lambda b, i, j: (b, 0, 0)),
            pl.BlockSpec((1, 1, dm), lambda b, i, j: (b, 0, 0)),
            pl.BlockSpec((dm, tn), lambda b, i, j: (0, j)),
        ],
        out_specs=pl.BlockSpec((1, tm, tn), lambda b, i, j: (b, i, j)),
        out_shape=jax.ShapeDtypeStruct((bsz, t, n), F32),
        scratch_shapes=[pltpu.VMEM((tm, dm), BF16)],
        compiler_params=pltpu.CompilerParams(
            dimension_semantics=("parallel", "parallel", "arbitrary"), vmem_limit_bytes=V7X_VMEM_LIMIT_BYTES),
        name="proj_in",
    )(x, gain[None], sc[:, None], sh[:, None], w)


def _proj_out_kernel(y_ref, w_ref, x_ref, g_ref, o_ref):
    o_ref[0] = x_ref[0] + g_ref[0] * jnp.dot(y_ref[0].astype(BF16), w_ref[...], preferred_element_type=F32)


def _proj_out(y, w, x, g):
    bsz, t, k = y.shape
    dm = w.shape[1]
    tm, tn = min(PROJ_TM, t), _lane_tile(dm)
    return pl.pallas_call(
        _proj_out_kernel,
        grid=(bsz, t // tm, dm // tn),
        in_specs=[
            pl.BlockSpec((1, tm, k), lambda b, i, j: (b, i, 0)),
            pl.BlockSpec((k, tn), lambda b, i, j: (0, j)),
            pl.BlockSpec((1, tm, tn), lambda b, i, j: (b, i, j)),
            pl.BlockSpec((1, 1, tn), lambda b, i, j: (b, 0, j)),
        ],
        out_specs=pl.BlockSpec((1, tm, tn), lambda b, i, j: (b, i, j)),
        out_shape=jax.ShapeDtypeStruct((bsz, t, dm), F32),
        compiler_params=pltpu.CompilerParams(
            dimension_semantics=("parallel", "parallel", "arbitrary"), vmem_limit_bytes=V7X_VMEM_LIMIT_BYTES),
        name="proj_out",
    )(y, w, x, g[:, None])


MOD_ROWS = 8
MOD_TN = 1024


def _mod_kernel(c_ref, w_ref, b_ref, o_ref):
    o_ref[...] = jnp.dot(jax.nn.silu(c_ref[...]).astype(BF16), w_ref[...].astype(BF16), preferred_element_type=F32) + b_ref[...]


def _modulation(c_rows, w, b):
    dm, n = w.shape
    return pl.pallas_call(
        _mod_kernel,
        grid=(n // MOD_TN,),
        in_specs=[
            pl.BlockSpec((MOD_ROWS, dm), lambda j: (0, 0)),
            pl.BlockSpec((dm, MOD_TN), lambda j: (0, j)),
            pl.BlockSpec((1, MOD_TN), lambda j: (0, j)),
        ],
        out_specs=pl.BlockSpec((MOD_ROWS, MOD_TN), lambda j: (0, j)),
        out_shape=jax.ShapeDtypeStruct((MOD_ROWS, n), F32),
        compiler_params=pltpu.CompilerParams(dimension_semantics=("parallel",), vmem_limit_bytes=V7X_VMEM_LIMIT_BYTES),
        name="modulation",
    )(c_rows, w, b[None])


def _final_norm_kernel(x_ref, g_ref, o_ref):
    x = x_ref[0]
    o_ref[0] = x * lax.rsqrt(jnp.mean(x * x, axis=-1, keepdims=True) + EPS) * g_ref[...]


def _final_norm(x, gain):
    bsz, t, dm = x.shape
    return pl.pallas_call(
        _final_norm_kernel,
        grid=(bsz, t // PROJ_TM),
        in_specs=[pl.BlockSpec((1, PROJ_TM, dm), lambda b, i: (b, i, 0)), pl.BlockSpec((1, dm), lambda b, i: (0, 0))],
        out_specs=pl.BlockSpec((1, PROJ_TM, dm), lambda b, i: (b, i, 0)),
        out_shape=jax.ShapeDtypeStruct(x.shape, F32),
        compiler_params=pltpu.CompilerParams(dimension_semantics=("parallel", "parallel"), vmem_limit_bytes=V7X_VMEM_LIMIT_BYTES),
        name="final_norm",
    )(x, gain[None])


def _router_kernel(x_ref, gain_ref, sc_ref, sh_ref, wr_ref, h_ref, aff_ref):
    x = x_ref[0]
    y = x * lax.rsqrt(jnp.mean(x * x, axis=-1, keepdims=True) + EPS) * gain_ref[...]
    h = (y * (1.0 + sc_ref[0]) + sh_ref[0]).astype(BF16)
    h_ref[0] = h
    logits = lax.dot_general(wr_ref[...], h, (((1,), (1,)), ((), ())), preferred_element_type=F32)
    e = jnp.exp(logits - jnp.max(logits, axis=0, keepdims=True))
    aff_ref[0] = e / jnp.sum(e, axis=0, keepdims=True)


def _router(x, gain, sc, sh, w_router):
    bsz, t, dm = x.shape
    tm = min(PROJ_TM, t)
    return pl.pallas_call(
        _router_kernel,
        grid=(bsz, t // tm),
        in_specs=[
            pl.BlockSpec((1, tm, dm), lambda b, i: (b, i, 0)),
            pl.BlockSpec((1, dm), lambda b, i: (0, 0)),
            pl.BlockSpec((1, 1, dm), lambda b, i: (b, 0, 0)),
            pl.BlockSpec((1, 1, dm), lambda b, i: (b, 0, 0)),
            pl.BlockSpec((N_EXPERTS, dm), lambda b, i: (0, 0)),
        ],
        out_specs=[
            pl.BlockSpec((1, tm, dm), lambda b, i: (b, i, 0)),
            pl.BlockSpec((1, N_EXPERTS, tm), lambda b, i: (b, 0, i)),
        ],
        out_shape=[jax.ShapeDtypeStruct((bsz, t, dm), BF16), jax.ShapeDtypeStruct((bsz, N_EXPERTS, t), F32)],
        compiler_params=pltpu.CompilerParams(dimension_semantics=("parallel", "parallel"), vmem_limit_bytes=V7X_VMEM_LIMIT_BYTES),
        name="moe_router",
    )(x, gain[None], sc[:, None], sh[:, None], w_router.T.astype(BF16))


def _ffn_kernel(x_ref, gate_ref, wg_ref, wu_ref, wd_ref, o_ref):
    x = x_ref[0, 0]
    a = jnp.dot(x, wg_ref[0], preferred_element_type=F32)
    u = jnp.dot(x, wu_ref[0], preferred_element_type=F32)
    hid = (jax.nn.silu(a) * u).astype(BF16)
    o_ref[0, 0] = jnp.dot(hid, wd_ref[0], preferred_element_type=F32) * gate_ref[0, 0]


def _expert_ffn(xe, gate, w_gate, w_up, w_down):
    s, e, c, dm = xe.shape
    ff = w_gate.shape[-1]
    return pl.pallas_call(
        _ffn_kernel,
        grid=(e, s),
        in_specs=[
            pl.BlockSpec((1, 1, c, dm), lambda ei, si: (si, ei, 0, 0)),
            pl.BlockSpec((1, 1, c, 1), lambda ei, si: (si, ei, 0, 0)),
            pl.BlockSpec((1, dm, ff), lambda ei, si: (ei, 0, 0)),
            pl.BlockSpec((1, dm, ff), lambda ei, si: (ei, 0, 0)),
            pl.BlockSpec((1, ff, dm), lambda ei, si: (ei, 0, 0)),
        ],
        out_specs=pl.BlockSpec((1, 1, c, dm), lambda ei, si: (si, ei, 0, 0)),
        out_shape=jax.ShapeDtypeStruct((s, e, c, dm), F32),
        compiler_params=pltpu.CompilerParams(dimension_semantics=("parallel", "arbitrary"), vmem_limit_bytes=V7X_VMEM_LIMIT_BYTES),
        name="moe_ffn",
    )(xe, gate[..., None], w_gate, w_up, w_down)


LRU_BLOCK = LRU_WIDTH // LRU_HEADS
LRU_L = 128
SUBLANES = 8
LRU_PAD = SUBLANES


def _lru_kernel(seq, xc_ref, gc_ref, xl_ref, gl_ref, cw_ref, cb_ref, wa_ref, wx_ref, ba_ref, bx_ref, lam_ref,
                oc_ref, ol_ref, xs_ref, hf_ref):
    t_c, t_l = seq
    n_c, n_l = t_c // LRU_L, t_l // LRU_L
    off_c, off_l = LRU_PAD, 2 * LRU_PAD + t_c
    zeros = jnp.zeros((LRU_PAD, LRU_BLOCK), F32)
    xs_ref[0:LRU_PAD] = zeros
    xs_ref[off_c + t_c:off_l] = zeros
    xs_ref[off_l + t_l:off_l + t_l + LRU_PAD] = zeros
    xs_ref[off_c:off_c + t_c] = xc_ref[0]
    xs_ref[off_l:off_l + t_l] = xl_ref[0]

    cw = cw_ref[...]
    cb = cb_ref[...]
    row = lax.broadcasted_iota(jnp.int32, (LRU_L // SUBLANES, SUBLANES, LRU_BLOCK), 1)

    def gates(r0, d):
        win = xs_ref[pl.ds(r0 - LRU_PAD, LRU_L + 2 * LRU_PAD), :]
        x = cb
        for k in range(4):
            x = x + cw[k:k + 1] * win[LRU_PAD - 2 + k:LRU_PAD - 2 + k + LRU_L]
        xb = x.astype(BF16)
        r = jax.nn.sigmoid(jnp.dot(xb, wa_ref[d, 0].astype(BF16), preferred_element_type=F32) + ba_ref[d:d + 1])
        ig = jax.nn.sigmoid(jnp.dot(xb, wx_ref[d, 0].astype(BF16), preferred_element_type=F32) + bx_ref[d:d + 1])
        log_a = -LRU_C * r * jax.nn.softplus(-lam_ref[d:d + 1])
        a = jnp.exp(log_a)
        return a, jnp.sqrt(1.0 - a * a) * ig * x

    def chunk_scan(a, b, carry, reverse):
        n = LRU_L // SUBLANES
        a = a.reshape(n, SUBLANES, LRU_BLOCK)
        b = b.reshape(n, SUBLANES, LRU_BLOCK)
        for s in (1, 2, 4):
            if reverse:
                keep = row < SUBLANES - s
                shift = SUBLANES - s
            else:
                keep = row >= s
                shift = s
            a_n = jnp.where(keep, pltpu.roll(a, shift, 1), 1.0)
            b_n = jnp.where(keep, pltpu.roll(b, shift, 1), 0.0)
            b = a * b_n + b
            a = a * a_n
        outs = [None] * n
        last = 0 if reverse else SUBLANES - 1
        for j in (range(n - 1, -1, -1) if reverse else range(n)):
            h = b[j] + a[j] * carry
            carry = h[last:last + 1]
            outs[j] = h
        return jnp.concatenate(outs, axis=0), carry

    def fwd_chunk(t0, r0, carry):
        a, b = gates(r0, 0)
        h, carry = chunk_scan(a, b, carry, False)
        hf_ref[pl.ds(t0, LRU_L), :] = h
        return carry

    def bwd_chunk(t0, r0, carry):
        a, b = gates(r0, 1)
        h, carry = chunk_scan(a, b, carry, True)
        return hf_ref[pl.ds(t0, LRU_L), :] + h, carry

    carry = jnp.zeros((1, LRU_BLOCK), F32)
    for c in range(n_c):
        carry = fwd_chunk(c * LRU_L, off_c + c * LRU_L, carry)

    def fwd_body(c, carry):
        t0 = pl.multiple_of(c * LRU_L, LRU_L)
        return fwd_chunk(t_c + t0, off_l + t0, carry)

    lax.fori_loop(0, n_l, fwd_body, carry)

    carry = jnp.zeros((1, LRU_BLOCK), F32)
    for c in range(n_c - 1, -1, -1):
        h, carry = bwd_chunk(c * LRU_L, off_c + c * LRU_L, carry)
        oc_ref[0, c * LRU_L:(c + 1) * LRU_L, :] = h * jax.nn.gelu(gc_ref[0, c * LRU_L:(c + 1) * LRU_L, :])

    def bwd_body(i, carry):
        t0 = pl.multiple_of((n_l - 1 - i) * LRU_L, LRU_L)
        h, carry = bwd_chunk(t_c + t0, off_l + t0, carry)
        ol_ref[0, pl.ds(t0, LRU_L), :] = h * jax.nn.gelu(gl_ref[0, pl.ds(t0, LRU_L), :])
        return carry

    lax.fori_loop(0, n_l, bwd_body, carry)


def _rglru(u_c, u_l, conv_w, conv_b, wa, ba, wx, bx, lam):
    bsz, t_c, _ = u_c.shape
    t_l = u_l.shape[1]
    x_blk = (EV_OUT + 2 * HG_KW + HG_VW) // LRU_BLOCK
    g_blk = (HG_KW + HG_VW) // LRU_BLOCK
    t_pad = t_c + t_l + 3 * LRU_PAD
    vec = lambda: pl.BlockSpec((2, LRU_BLOCK), lambda b, h: (0, h))
    return pl.pallas_call(
        functools.partial(_lru_kernel, (t_c, t_l)),
        grid=(bsz, LRU_HEADS),
        in_specs=[
            pl.BlockSpec((1, t_c, LRU_BLOCK), lambda b, h: (b, 0, x_blk + h)),
            pl.BlockSpec((1, t_c, LRU_BLOCK), lambda b, h: (b, 0, g_blk + h)),
            pl.BlockSpec((1, t_l, LRU_BLOCK), lambda b, h: (b, 0, x_blk + h)),
            pl.BlockSpec((1, t_l, LRU_BLOCK), lambda b, h: (b, 0, g_blk + h)),
            pl.BlockSpec((4, LRU_BLOCK), lambda b, h: (0, h)),
            pl.BlockSpec((1, LRU_BLOCK), lambda b, h: (0, h)),
            pl.BlockSpec((2, 1, LRU_BLOCK, LRU_BLOCK), lambda b, h: (0, h, 0, 0)),
            pl.BlockSpec((2, 1, LRU_BLOCK, LRU_BLOCK), lambda b, h: (0, h, 0, 0)),
            vec(), vec(), vec(),
        ],
        out_specs=[
            pl.BlockSpec((1, t_c, LRU_BLOCK), lambda b, h: (b, 0, h)),
            pl.BlockSpec((1, t_l, LRU_BLOCK), lambda b, h: (b, 0, h)),
        ],
        out_shape=[jax.ShapeDtypeStruct((bsz, t_c, LRU_WIDTH), F32), jax.ShapeDtypeStruct((bsz, t_l, LRU_WIDTH), F32)],
        scratch_shapes=[pltpu.VMEM((t_pad, LRU_BLOCK), F32), pltpu.VMEM((t_c + t_l, LRU_BLOCK), F32)],
        compiler_params=pltpu.CompilerParams(
            dimension_semantics=("parallel", "parallel"), vmem_limit_bytes=V7X_VMEM_LIMIT_BYTES),
        name="rglru",
    )(u_c, u_c, u_l, u_l, conv_w, conv_b[None], wa, wx, ba, bx, lam)


HG_GROUP = 8


def _hgrn2_kernel(seq, qc_ref, gc_ref, ffc_ref, fbc_ref, vc_ref, ql_ref, gl_ref, ffl_ref, fbl_ref, vl_ref,
                  lbf_ref, lbb_ref, gain_ref, oc_ref, ol_ref, of_ref):
    t_c, t_l = seq
    L = HG_CHUNK
    R = HG_GROUP * L
    n_gc = t_c // L
    r_i = lax.broadcasted_iota(jnp.int32, (R, R), 0)
    c_i = lax.broadcasted_iota(jnp.int32, (R, R), 1)
    same = (r_i // L) == (c_i // L)
    causal = (same & (c_i <= r_i), same & (c_i >= r_i))
    tri = tuple(jnp.where(m, 1.0, 0.0).astype(BF16) for m in causal)
    lb = (lbf_ref[...], lbb_ref[...])
    gain = gain_ref[...]

    def group(d, n, q_raw, f_raw, v, st):
        rows = n * L
        f = lb[d] + (1.0 - lb[d]) * jax.nn.sigmoid(f_raw)
        logf = jnp.log(f)
        k = 1.0 - f
        hi = logf.astype(BF16)
        r1 = logf - hi.astype(F32)
        mid = r1.astype(BF16)
        lo = (r1 - mid.astype(F32)).astype(BF16)
        g3 = jnp.dot(tri[d][:rows, :rows], jnp.concatenate([hi, mid, lo], axis=1), preferred_element_type=F32)
        g = g3[:, :HG_DK] + g3[:, HG_DK:2 * HG_DK] + g3[:, 2 * HG_DK:]
        g4 = g.reshape(n, L, HG_DK)
        g_last = g4[:, 0:1] if d else g4[:, L - 1:L]
        g_rest = (jnp.broadcast_to(g_last, (n, L, HG_DK)) - g4).reshape(rows, HG_DK)
        qg = (jax.nn.silu(q_raw) * jnp.exp(g)).astype(BF16)
        kg = (k * jnp.exp(-g)).astype(BF16)
        kd = (k * jnp.exp(g_rest)).astype(BF16)
        vb = v.astype(BF16)
        att = lax.dot_general(qg, kg, (((1,), (1,)), ((), ())), preferred_element_type=F32)
        att = jnp.where(causal[d][:rows, :rows], att, 0.0).astype(BF16)
        o = jnp.dot(att, vb, preferred_element_type=F32)
        dec = jnp.exp(g_last)
        sl = [slice(j * L, (j + 1) * L) for j in range(n)]
        ds_t = [lax.dot_general(vb[s], kd[s], (((0,), (0,)), ((), ())), preferred_element_type=F32) for s in sl]
        entering = [None] * n
        for j in (range(n - 1, -1, -1) if d else range(n)):
            entering[j] = st.astype(BF16)
            st = dec[j] * st + ds_t[j]
        outs = [o[s] + lax.dot_general(qg[s], e, (((1,), (1,)), ((), ())), preferred_element_type=F32)
                for s, e in zip(sl, entering)]
        return jnp.concatenate(outs, axis=0), st

    def finish(o, gate):
        return _rms(o, gain) * jax.nn.silu(gate)

    def lat_rows(i):
        return pl.ds(pl.multiple_of(i * R, R), R), pl.ds(pl.multiple_of(t_c + i * R, math.gcd(t_c, R)), R)

    st = jnp.zeros((HG_DV, HG_DK), F32)
    o, st = group(0, n_gc, qc_ref[0], ffc_ref[0], vc_ref[0], st)
    of_ref[0:t_c, :] = o

    def fwd_body(i, st):
        rows, srows = lat_rows(i)
        o, st = group(0, HG_GROUP, ql_ref[0, rows, :], ffl_ref[0, rows, :], vl_ref[0, rows, :], st)
        of_ref[srows, :] = o
        return st

    lax.fori_loop(0, t_l // R, fwd_body, st)

    st = jnp.zeros((HG_DV, HG_DK), F32)
    o, st = group(1, n_gc, qc_ref[0], fbc_ref[0], vc_ref[0], st)
    oc_ref[0] = finish(of_ref[0:t_c, :] + o, gc_ref[0])

    def bwd_body(i, st):
        rows, srows = lat_rows(t_l // R - 1 - i)
        o, st = group(1, HG_GROUP, ql_ref[0, rows, :], fbl_ref[0, rows, :], vl_ref[0, rows, :], st)
        ol_ref[0, rows, :] = finish(of_ref[srows, :] + o, gl_ref[0, rows, :])
        return st

    lax.fori_loop(0, t_l // R, bwd_body, st)


def _hgrn2(u_c, u_l, lb_f, lb_b, onorm):
    bsz, t_c, _ = u_c.shape
    t_l = u_l.shape[1]
    nb = HG_KW // HG_DK
    col = lambda t, blk: pl.BlockSpec((1, t, HG_DK), lambda b, h: (b, 0, blk + h))
    seq_cols = lambda t: [col(t, 0), col(t, nb), col(t, EV_OUT // HG_DK), col(t, EV_OUT // HG_DK + nb), col(t, EV_OUT // HG_DK + 2 * nb)]
    vec = lambda: pl.BlockSpec((1, HG_DK), lambda b, h: (0, h))
    return pl.pallas_call(
        functools.partial(_hgrn2_kernel, (t_c, t_l)),
        grid=(bsz, HG_HEADS),
        in_specs=seq_cols(t_c) + seq_cols(t_l) + [vec(), vec(), vec()],
        out_specs=[
            pl.BlockSpec((1, t_c, HG_DV), lambda b, h: (b, 0, h)),
            pl.BlockSpec((1, t_l, HG_DV), lambda b, h: (b, 0, h)),
        ],
        out_shape=[jax.ShapeDtypeStruct((bsz, t_c, HG_VW), F32), jax.ShapeDtypeStruct((bsz, t_l, HG_VW), F32)],
        scratch_shapes=[pltpu.VMEM((t_c + t_l, HG_DV), F32)],
        compiler_params=pltpu.CompilerParams(
            dimension_semantics=("parallel", "parallel"), vmem_limit_bytes=V7X_VMEM_LIMIT_BYTES),
        name="hgrn2",
    )(*([u_c] * 5), *([u_l] * 5), lb_f[None], lb_b[None], onorm[None])


LANES = 128
MLA_HW = 2 * LANES
MLA_TM = 512
MLA_TQ = 512
MLA_TK = 512
MLA_UNROLL = 2
MLA_VW = MLA_V + LANES
MLA_SCALE_LOG2E = (MLA_NOPE + MLA_ROPE) ** -0.5 * math.log2(math.e)


def _rms(x, gain):
    return x * lax.rsqrt(jnp.mean(x * x, axis=-1, keepdims=True) + EPS) * gain


def _rope_tile(r, cos_t, sin_t):
    return r * cos_t + (pltpu.roll(r, MLA_ROPE // 2, 1) + pltpu.roll(r, LANES - MLA_ROPE // 2, 1)) * sin_t


def _mla_q_kernel(u_ref, gain_ref, w_ref, cos_ref, sin_ref, q_ref):
    q = jnp.dot(_rms(u_ref[0], gain_ref[...]).astype(BF16), w_ref[...], preferred_element_type=F32)
    for h in range(MLA_HEADS):
        lo = h * MLA_HW
        q_ref[0, :, lo:lo + LANES] = q[:, lo:lo + LANES].astype(BF16)
        q_ref[0, :, lo + LANES:lo + MLA_HW] = _rope_tile(q[:, lo + LANES:lo + MLA_HW], cos_ref[...], sin_ref[...]).astype(BF16)


def _mla_kv_kernel(u_ref, kr_ref, gain_ref, w_ref, cos_ref, sin_ref, k_ref, v_ref):
    kv = jnp.dot(_rms(u_ref[0], gain_ref[...]).astype(BF16), w_ref[...], preferred_element_type=F32)
    kr = _rope_tile(kr_ref[0], cos_ref[...], sin_ref[...]).astype(BF16)
    for h in range(MLA_HEADS):
        k_ref[0, :, h * MLA_HW:h * MLA_HW + LANES] = kv[:, h * LANES:(h + 1) * LANES].astype(BF16)
        k_ref[0, :, h * MLA_HW + LANES:(h + 1) * MLA_HW] = kr
    lane = lax.broadcasted_iota(jnp.int32, (kv.shape[0], LANES), 1)
    one_hot = jnp.where(lane == 0, 1.0, 0.0).astype(BF16)
    for h in range(MLA_HEADS):
        v_ref[0, :, h * MLA_VW:h * MLA_VW + MLA_V] = kv[:, (MLA_HEADS + h) * LANES:(MLA_HEADS + h + 1) * LANES].astype(BF16)
        v_ref[0, :, h * MLA_VW + MLA_V:(h + 1) * MLA_VW] = one_hot


def _mla_q(u, gain, w, cos_t, sin_t):
    bsz, t, _ = u.shape
    tm = min(MLA_TM, t)
    return pl.pallas_call(
        _mla_q_kernel,
        grid=(bsz, t // tm),
        in_specs=[
            pl.BlockSpec((1, tm, MLA_Q_LORA), lambda b, i: (b, i, 0)),
            pl.BlockSpec((1, MLA_Q_LORA), lambda b, i: (0, 0)),
            pl.BlockSpec((MLA_Q_LORA, MLA_HEADS * MLA_HW), lambda b, i: (0, 0)),
            pl.BlockSpec((tm, LANES), lambda b, i: (i, 0)),
            pl.BlockSpec((tm, LANES), lambda b, i: (i, 0)),
        ],
        out_specs=pl.BlockSpec((1, tm, MLA_HEADS * MLA_HW), lambda b, i: (b, i, 0)),
        out_shape=jax.ShapeDtypeStruct((bsz, t, MLA_HEADS * MLA_HW), BF16),
        compiler_params=pltpu.CompilerParams(
            dimension_semantics=("parallel", "parallel"), vmem_limit_bytes=V7X_VMEM_LIMIT_BYTES),
        name="mla_q_up",
    )(u, gain[None], w, cos_t, sin_t)


def _mla_kv(u, c_blk, r_blk, gain, w, cos_t, sin_t):
    bsz, t, _ = u.shape
    tm = min(MLA_TM, t)
    return pl.pallas_call(
        _mla_kv_kernel,
        grid=(bsz, t // tm),
        in_specs=[
            pl.BlockSpec((1, tm, MLA_KV_LORA), lambda b, i: (b, i, c_blk)),
            pl.BlockSpec((1, tm, LANES), lambda b, i: (b, i, r_blk)),
            pl.BlockSpec((1, MLA_KV_LORA), lambda b, i: (0, 0)),
            pl.BlockSpec((MLA_KV_LORA, MLA_HEADS * (LANES + MLA_V)), lambda b, i: (0, 0)),
            pl.BlockSpec((tm, LANES), lambda b, i: (i, 0)),
            pl.BlockSpec((tm, LANES), lambda b, i: (i, 0)),
        ],
        out_specs=[
            pl.BlockSpec((1, tm, MLA_HEADS * MLA_HW), lambda b, i: (b, i, 0)),
            pl.BlockSpec((1, tm, MLA_HEADS * MLA_VW), lambda b, i: (b, i, 0)),
        ],
        out_shape=[jax.ShapeDtypeStruct((bsz, t, MLA_HEADS * MLA_HW), BF16),
                   jax.ShapeDtypeStruct((bsz, t, MLA_HEADS * MLA_VW), BF16)],
        compiler_params=pltpu.CompilerParams(
            dimension_semantics=("parallel", "parallel"), vmem_limit_bytes=V7X_VMEM_LIMIT_BYTES),
        name="mla_kv_up",
    )(u, u, gain[None], w, cos_t, sin_t)


def _flash_kernel(t_l, q_ref, kc_ref, vc_ref, kl_ref, vl_ref, o_ref):
    q = q_ref[0]

    tiles = [(kc_ref, vc_ref, slice(None))] + [(kl_ref, vl_ref, slice(i * MLA_TK, (i + 1) * MLA_TK)) for i in range(t_l // MLA_TK)]

    def scores(tile):
        k_ref, _, rows = tile
        return lax.dot_general(q, k_ref[0, rows, :], (((1,), (1,)), ((), ())), preferred_element_type=F32)

    tq = q.shape[0]
    m = jnp.full((tq, 1), -jnp.inf, F32)
    acc = jnp.zeros((tq, MLA_VW), F32)
    s_next = scores(tiles[0])
    for i, (_, v_ref, rows) in enumerate(tiles):
        s = s_next * MLA_SCALE_LOG2E
        if i + 1 < len(tiles):
            s_next = scores(tiles[i + 1])
        m_new = jnp.maximum(m, jnp.max(s, axis=-1, keepdims=True))
        p = jnp.exp2(s - m_new)
        acc = jnp.exp2(m - m_new) * acc + jnp.dot(p.astype(BF16), v_ref[0, rows, :], preferred_element_type=F32)
        m = m_new
    o_ref[0] = acc[:, :MLA_V] / acc[:, MLA_V:MLA_V + 1]


def _flash(q, k_c, v_c, k_l, v_l):
    bsz, t, _ = q.shape
    t_c, t_l = k_c.shape[1], k_l.shape[1]
    return pl.pallas_call(
        functools.partial(_flash_kernel, t_l),
        grid=(bsz, MLA_HEADS, t // MLA_TQ),
        in_specs=[
            pl.BlockSpec((1, MLA_TQ, MLA_HW), lambda b, h, i: (b, i, h)),
            pl.BlockSpec((1, t_c, MLA_HW), lambda b, h, i: (b, 0, h)),
            pl.BlockSpec((1, t_c, MLA_VW), lambda b, h, i: (b, 0, h)),
            pl.BlockSpec((1, t_l, MLA_HW), lambda b, h, i: (b, 0, h)),
            pl.BlockSpec((1, t_l, MLA_VW), lambda b, h, i: (b, 0, h)),
        ],
        out_specs=pl.BlockSpec((1, MLA_TQ, MLA_V), lambda b, h, i: (b, i, h)),
        out_shape=jax.ShapeDtypeStruct((bsz, t, MLA_HEADS * MLA_V), F32),
        compiler_params=pltpu.CompilerParams(
            dimension_semantics=("parallel", "parallel", "arbitrary"), vmem_limit_bytes=V7X_VMEM_LIMIT_BYTES),
        name="mla_flash",
    )(q, k_c, v_c, k_l, v_l)


def _mla(u_c, u_l, q_norm, w_q_up, kv_norm, w_kv_up, cos, sin):
    t_c, t_l = u_c.shape[1], u_l.shape[1]
    zeros = jnp.zeros((t_l, LANES - MLA_ROPE), F32)
    cos_l = jnp.concatenate([cos, cos, zeros], axis=-1)
    sin_l = jnp.concatenate([-sin, sin, zeros], axis=-1)
    cos_c = jnp.concatenate([jnp.ones((t_c, MLA_ROPE), F32), zeros[:t_c]], axis=-1)
    sin_c = jnp.zeros((t_c, LANES), F32)
    wq = w_q_up.reshape(MLA_Q_LORA, MLA_HEADS, MLA_NOPE + MLA_ROPE)
    wq = jnp.pad(wq, ((0, 0), (0, 0), (0, MLA_HW - MLA_NOPE - MLA_ROPE))).reshape(MLA_Q_LORA, -1).astype(BF16)
    wkv = w_kv_up.reshape(MLA_KV_LORA, MLA_HEADS, MLA_NOPE + MLA_V)
    wkv = jnp.concatenate([wkv[..., :MLA_NOPE].reshape(MLA_KV_LORA, -1), wkv[..., MLA_NOPE:].reshape(MLA_KV_LORA, -1)], axis=-1).astype(BF16)
    q = _mla_q(u_l, q_norm, wq, cos_l, sin_l)
    k_l, v_l = _mla_kv(u_l, OD_OUT // MLA_KV_LORA, (OD_OUT + MLA_KV_LORA) // LANES, kv_norm, wkv, cos_l, sin_l)
    k_c, v_c = _mla_kv(u_c, 0, MLA_KV_LORA // LANES, kv_norm, wkv, cos_c, sin_c)
    return _flash(q, k_c, v_c, k_l, v_l)


OD_BLK_Z = MLA_Q_LORA // LANES
OD_BLK_CKV = OD_OUT // LANES
OD_BLK_KR = OD_BLK_CKV + MLA_KV_LORA // LANES
OD_BLK_XBC = OD_BLK_KR + 1
SSD_XBC_BLKS = SSD_XBC // LANES
SSD_ACT_BLKS = SSD_XBC_BLKS + SSD_GROUPS
SSD_GH = SSD_HEADS // SSD_GROUPS
SSD_GW = SSD_WIDTH // SSD_GROUPS
SSD_TILE = 2 * SSD_CHUNK
SSD_CONV_ROWS = 512


def _od_weight_layout(w_in, ctx_only):
    k = w_in.shape[0]
    z = lambda n: jnp.zeros((k, n), w_in.dtype)
    o = OD_OUT + MLA_KV_LORA
    xbc0 = o + MLA_ROPE
    dt0 = xbc0 + SSD_XBC
    cols = [] if ctx_only else [w_in[:, :OD_OUT]]
    cols += [w_in[:, OD_OUT:o], w_in[:, o:xbc0], z(LANES - MLA_ROPE), w_in[:, xbc0:dt0]]
    for g in range(SSD_GROUPS):
        cols += [w_in[:, dt0 + g * SSD_GH:dt0 + (g + 1) * SSD_GH],
                 w_in[:, dt0 + SSD_HEADS + g * SSD_GH:dt0 + SSD_HEADS + (g + 1) * SSD_GH], z(LANES - 2 * SSD_GH)]
    w = jnp.concatenate(cols, axis=1)
    return jnp.pad(w, ((0, 0), (0, -w.shape[1] % (10 * LANES) if ctx_only else -w.shape[1] % (8 * LANES))))


def _ssd_group_lanes(v):
    rows = [jnp.concatenate([v[0, g * SSD_GH:(g + 1) * SSD_GH], v[1, g * SSD_GH:(g + 1) * SSD_GH],
                             jnp.zeros((LANES - 2 * SSD_GH,), v.dtype)]) for g in range(SSD_GROUPS)]
    return jnp.stack(rows)


def _ssd_act_kernel(seq, xc_ref, xl_ref, cw_ref, cb_ref, o_ref, xs_ref):
    t_c, t_l = seq
    j = pl.program_id(1)

    @pl.when(j < SSD_XBC_BLKS)
    def _():
        off_c, off_l = LRU_PAD, 2 * LRU_PAD + t_c
        zeros = jnp.zeros((LRU_PAD, LANES), F32)
        xs_ref[0:LRU_PAD] = zeros
        xs_ref[off_c + t_c:off_l] = zeros
        xs_ref[off_l + t_l:off_l + t_l + LRU_PAD] = zeros
        xs_ref[off_c:off_c + t_c] = xc_ref[0]
        xs_ref[off_l:off_l + t_l] = xl_ref[0]
        cw = cw_ref[...]
        cb = cb_ref[...]

        def conv(r0, rows):
            win = xs_ref[pl.ds(r0 - LRU_PAD, rows + 2 * LRU_PAD), :]
            y = cb
            for k in range(4):
                y = y + cw[k:k + 1] * win[LRU_PAD - 2 + k:LRU_PAD - 2 + k + rows]
            return jax.nn.silu(y)

        o_ref[0, 0:t_c, :] = conv(off_c, t_c)

        def body(i, _):
            r0 = pl.multiple_of(i * SSD_CONV_ROWS, SSD_CONV_ROWS)
            o_ref[0, pl.ds(t_c + r0, SSD_CONV_ROWS), :] = conv(off_l + r0, SSD_CONV_ROWS)
            return 0

        lax.fori_loop(0, t_l // SSD_CONV_ROWS, body, 0)

    @pl.when(j >= SSD_XBC_BLKS)
    def _():
        o_ref[0, 0:t_c, :] = jax.nn.softplus(xc_ref[0] + cb_ref[...])
        o_ref[0, t_c:t_c + t_l, :] = jax.nn.softplus(xl_ref[0] + cb_ref[...])


def _ssd_act(u_c, u_l, conv_w, conv_b, dt_bias):
    bsz, t_c, _ = u_c.shape
    t_l = u_l.shape[1]
    cw = jnp.pad(conv_w, ((0, 0), (0, SSD_GROUPS * LANES)))
    cb = jnp.concatenate([conv_b, _ssd_group_lanes(dt_bias).reshape(-1)])[None]
    ctx_blk = OD_BLK_XBC - OD_BLK_CKV
    return pl.pallas_call(
        functools.partial(_ssd_act_kernel, (t_c, t_l)),
        grid=(bsz, SSD_ACT_BLKS),
        in_specs=[
            pl.BlockSpec((1, t_c, LANES), lambda b, j: (b, 0, ctx_blk + j)),
            pl.BlockSpec((1, t_l, LANES), lambda b, j: (b, 0, OD_BLK_XBC + j)),
            pl.BlockSpec((4, LANES), lambda b, j: (0, j)),
            pl.BlockSpec((1, LANES), lambda b, j: (0, j)),
        ],
        out_specs=pl.BlockSpec((1, t_c + t_l, LANES), lambda b, j: (b, 0, j)),
        out_shape=jax.ShapeDtypeStruct((bsz, t_c + t_l, SSD_ACT_BLKS * LANES), F32),
        scratch_shapes=[pltpu.VMEM((t_c + t_l + 3 * LRU_PAD, LANES), F32)],
        compiler_params=pltpu.CompilerParams(
            dimension_semantics=("parallel", "parallel"), vmem_limit_bytes=V7X_VMEM_LIMIT_BYTES),
        name="ssd_act",
    )(u_c, u_l, cw, cb)


def _split3(x):
    hi = x.astype(BF16)
    r1 = x - hi.astype(F32)
    mid = r1.astype(BF16)
    return hi, mid, (r1 - mid.astype(F32)).astype(BF16)


def _ssd_scan_kernel(d, n_tiles, *refs):
    if d == 0:
        x_ref, b_ref, c_ref, dt_ref, alog_ref, y_ref, s_ref = refs
    else:
        x_ref, b_ref, c_ref, dt_ref, alog_ref, yf_ref, z_ref, dskip_ref, gain_ref, y_ref, s_ref = refs
    L = SSD_CHUNK
    P = SSD_HEADDIM

    @pl.when(pl.program_id(2) == 0)
    def _():
        s_ref[...] = jnp.zeros_like(s_ref)

    r_i = lax.broadcasted_iota(jnp.int32, (L, L), 0)
    c_i = lax.broadcasted_iota(jnp.int32, (L, L), 1)
    causal = (c_i >= r_i) if d else (c_i <= r_i)
    tri = jnp.where(causal, 1.0, 0.0).astype(BF16)
    low_half = lax.broadcasted_iota(jnp.int32, (L, LANES), 1) < P
    a_lane = -jnp.exp(alog_ref[...])

    def pair_cols(cols, p):
        return jnp.where(low_half, cols[2 * p], cols[2 * p + 1])

    for ch in ((1, 0) if d else (0, 1)):
        rows = slice(ch * L, (ch + 1) * L)
        xs = x_ref[0, rows, :]
        bm = b_ref[0, rows, :].astype(BF16)
        cm = c_ref[0, rows, :].astype(BF16)
        dt = dt_ref[0, rows, :]
        a3 = _split3(dt * a_lane)
        cs3 = jnp.dot(tri, jnp.concatenate(a3, axis=1), preferred_element_type=F32)
        cs_col = cs3[:, :LANES] + cs3[:, LANES:2 * LANES] + cs3[:, 2 * LANES:]
        cs_row = sum(lax.dot_general(t, tri, (((0,), (1,)), ((), ())), preferred_element_type=F32) for t in a3)
        cb = lax.dot_general(cm, bm, (((1,), (1,)), ((), ())), preferred_element_type=F32)
        s_in = s_ref[...].astype(BF16)
        y_off = lax.dot_general(cm, s_in, (((1,), (1,)), ((), ())), preferred_element_type=F32)
        last = 0 if d else L - 1
        cols, dts, lasts = [], [], []
        for i in range(SSD_GH):
            lane = SSD_GH * d + i
            cols.append(cs_col[:, lane:lane + 1])
            dts.append(dt[:, lane:lane + 1])
            lasts.append(cs_col[last:last + 1, lane:lane + 1])
        ys, xdecs = [], []
        for p in range(SSD_GH // 2):
            xp = xs[:, p * LANES:(p + 1) * LANES]
            col2 = pair_cols(cols, p)
            xdt = xp * pair_cols(dts, p)
            xdt_b = xdt.astype(BF16)
            y2 = []
            for e in range(2):
                i = 2 * p + e
                lane = SSD_GH * d + i
                decay = jnp.where(causal, jnp.exp(cols[i] - cs_row[lane:lane + 1, :]), 0.0)
                y2.append(jnp.dot((cb * decay).astype(BF16), xdt_b, preferred_element_type=F32))
            ys.append(jnp.where(low_half, y2[0], y2[1]) + y_off[:, p * LANES:(p + 1) * LANES] * jnp.exp(col2))
            last2 = jnp.where(low_half, lasts[2 * p], lasts[2 * p + 1])
            xdecs.append((xdt * jnp.exp(last2 - col2)).astype(BF16))
        y = jnp.concatenate(ys, axis=1)
        ds = lax.dot_general(jnp.concatenate(xdecs, axis=1), bm, (((0,), (0,)), ((), ())), preferred_element_type=F32)
        dec = jnp.concatenate([jnp.broadcast_to(jnp.exp(lasts[i]), (P, SSD_STATE)) for i in range(SSD_GH)], axis=0)
        s_ref[...] = dec * s_ref[...] + ds
        if d == 0:
            y_ref[0, rows, :] = y
        else:
            out = (yf_ref[0, rows, :] + y + dskip_ref[...] * xs) * jax.nn.silu(z_ref[0, rows, :])
            y_ref[0, rows, :] = _rms(out, gain_ref[...])


def _ssd_scan(d, act, t_c, a_lanes, extra=()):
    bsz, t, _ = act.shape
    n_tiles = t // SSD_TILE
    n_ctx = t_c // SSD_TILE
    assert n_ctx == 1
    if d == 0:
        tile = lambda i: i
    else:
        tile = lambda i: jnp.where(i < n_ctx, n_ctx - 1 - i, n_tiles + n_ctx - 1 - i)
    in_specs = [
        pl.BlockSpec((1, SSD_TILE, SSD_GW), lambda b, g, i: (b, tile(i), g)),
        pl.BlockSpec((1, SSD_TILE, LANES), lambda b, g, i: (b, tile(i), SSD_WIDTH // LANES + g)),
        pl.BlockSpec((1, SSD_TILE, LANES), lambda b, g, i: (b, tile(i), (SSD_WIDTH + SSD_BC) // LANES + g)),
        pl.BlockSpec((1, SSD_TILE, LANES), lambda b, g, i: (b, tile(i), SSD_XBC_BLKS + g)),
        pl.BlockSpec((1, LANES), lambda b, g, i: (0, g)),
    ]
    args = [act, act, act, act, a_lanes.reshape(1, -1)]
    if d:
        y_f, u_l, d_skip, onorm = extra
        in_specs += [
            pl.BlockSpec((1, SSD_TILE, SSD_GW), lambda b, g, i: (b, tile(i), g)),
            pl.BlockSpec((1, SSD_TILE, SSD_GW), lambda b, g, i: (b, jnp.maximum(tile(i) - n_ctx, 0), OD_BLK_Z * LANES // SSD_GW + g)),
            pl.BlockSpec((1, SSD_GW), lambda b, g, i: (0, g)),
            pl.BlockSpec((1, SSD_GW), lambda b, g, i: (0, g)),
        ]
        args += [y_f, u_l, jnp.repeat(d_skip, SSD_HEADDIM)[None], onorm[None]]
    return pl.pallas_call(
        functools.partial(_ssd_scan_kernel, d, n_tiles),
        grid=(bsz, SSD_GROUPS, n_tiles),
        in_specs=in_specs,
        out_specs=pl.BlockSpec((1, SSD_TILE, SSD_GW), lambda b, g, i: (b, tile(i), g)),
        out_shape=jax.ShapeDtypeStruct((bsz, t, SSD_WIDTH), F32),
        scratch_shapes=[pltpu.VMEM((SSD_GW, SSD_STATE), F32)],
        compiler_params=pltpu.CompilerParams(
            dimension_semantics=("parallel", "parallel", "arbitrary"), vmem_limit_bytes=V7X_VMEM_LIMIT_BYTES),
        name="ssd_scan_bwd" if d else "ssd_scan_fwd",
    )(*args)


def _ssd(u_c, u_l, conv_w, conv_b, dt_bias, a_log, d_skip, onorm):
    act = _ssd_act(u_c, u_l, conv_w, conv_b, dt_bias)
    a_lanes = _ssd_group_lanes(a_log)
    y_f = _ssd_scan(0, act, u_c.shape[1], a_lanes)
    return _ssd_scan(1, act, u_c.shape[1], a_lanes, (y_f, u_l, d_skip, onorm))


def rmsnorm(x, g):
    xf = x.astype(F32)
    y = xf * lax.rsqrt(jnp.mean(xf * xf, axis=-1, keepdims=True) + EPS)
    return (y * g.astype(F32)).astype(x.dtype)


def heads(t, h):
    return t.reshape(t.shape[0], t.shape[1], h, t.shape[-1] // h)


def flip(t):
    return jnp.flip(t, axis=1)


def dwconv(x, w, b):
    k = w.shape[0]
    y = lax.conv_general_dilated(x, w[:, None, :].astype(x.dtype), window_strides=(1,), padding=[(k // 2, k - 1 - k // 2)], dimension_numbers=('NWC', 'WIO', 'NWC'), feature_group_count=x.shape[-1])
    return y + b.astype(x.dtype)


def axial_rope(rows):
    row = jnp.repeat(jnp.arange(rows, dtype=F32), GRID_W)
    col = jnp.arange(rows * GRID_W) % GRID_W
    n_freq = MLA_ROPE // 4
    inv = ROPE_THETA ** (-jnp.arange(n_freq, dtype=F32) / n_freq)
    ang = jnp.concatenate([row[:, None] * inv, col.astype(F32)[:, None] * inv], axis=-1)
    return jnp.cos(ang), jnp.sin(ang)


def apply_rope(t, cos, sin):
    t1, t2 = jnp.split(t.astype(F32), 2, axis=-1)
    cs, sn = cos[None, :, None, :], sin[None, :, None, :]
    return jnp.concatenate([t1 * cs - t2 * sn, t1 * sn + t2 * cs], axis=-1).astype(t.dtype)


def segsum(x):
    t = x.shape[-1]
    xx = jnp.broadcast_to(x[..., :, None], x.shape + (t,))
    xx = jnp.where(jnp.tril(jnp.ones((t, t), bool), -1), xx, 0.0)
    cs = jnp.cumsum(xx, axis=-2)
    return jnp.where(jnp.tril(jnp.ones((t, t), bool)), cs, -jnp.inf)


def gla_chunked(q, k, v, logf, s0):
    b, t, h, _ = k.shape
    dv = v.shape[-1]
    n = t // HG_CHUNK
    blk = lambda a: a.reshape(b, n, HG_CHUNK, h, a.shape[-1])
    k, v, g = blk(k), blk(v), jnp.cumsum(blk(logf), axis=2)
    g_last = g[:, :, -1:]
    ds = jnp.einsum('bnmhd,bnmhe->bnhde', k * jnp.exp(g_last - g), v)
    decay = jnp.exp(g_last[:, :, 0])

    def step(s, inp):
        dec, d = inp
        return dec[..., None] * s + d, s

    s_fin, s_prev = lax.scan(step, s0, (jnp.moveaxis(decay, 1, 0), jnp.moveaxis(ds, 1, 0)))
    if q is None:
        return None, s_fin
    qg = blk(q) * jnp.exp(g)
    att = jnp.einsum('bnlhd,bnmhd->bnhlm', qg, k * jnp.exp(-g))
    att = jnp.where(jnp.tril(jnp.ones((HG_CHUNK, HG_CHUNK), bool)), att, 0.0)
    o = jnp.einsum('bnhlm,bnmhe->bnlhe', att, v) + jnp.einsum('bnlhd,nbhde->bnlhe', qg, s_prev)
    return o.reshape(b, t, h, dv), s_fin


def hgrn2_scan(q_c, f_c, i_c, q_l, f_l, i_l, lb):
    def gates(fr):
        f = lb + (1.0 - lb) * jax.nn.sigmoid(fr.astype(F32))
        return heads(jnp.log(f), HG_HEADS), heads(1.0 - f, HG_HEADS)

    s0 = jnp.zeros((f_l.shape[0], HG_HEADS, HG_DK, HG_DV), F32)
    lf, k = gates(f_c)
    o_c, s_c = gla_chunked(q_c, k, i_c, lf, s0)
    lf, k = gates(f_l)
    o_l, _ = gla_chunked(q_l, k, i_l, lf, s_c)
    return o_c, o_l


def even_mixer(u_c, u_l, lb_f, lb_b, onorm, conv_w, conv_b, wa, ba, wx, bx, lam, need_ctx):
    assert need_ctx
    lru_c, lru_l = _rglru(u_c, u_l, conv_w, conv_b, wa, ba, wx, bx, lam)
    hg_c, hg_l = _hgrn2(u_c, u_l, lb_f, lb_b, onorm)
    return jnp.concatenate([hg_c, lru_c], axis=-1), jnp.concatenate([hg_l, lru_l], axis=-1)


def block_attention(q, k, v):
    b, t, h, dq = q.shape
    nb = t // ATTN_BLOCK
    scale = dq ** -0.5
    qb = jnp.moveaxis(q.reshape(b, nb, ATTN_BLOCK, h, dq), 1, 0)

    def one(qblk):
        s = jnp.einsum('bqhd,bkhd->bhqk', qblk, k).astype(F32) * scale
        p = jax.nn.softmax(s, axis=-1).astype(v.dtype)
        return jnp.einsum('bhqk,bkhe->bqhe', p, v)

    o = lax.map(one, qb)
    return jnp.moveaxis(o, 0, 1).reshape(b, t, h, v.shape[-1])


def ssd_chunked(x, dt, a, bm, cm, s0, want_y):
    b, t, h, p = x.shape
    n_c = t // SSD_CHUNK
    rep = h // bm.shape[2]
    blk = lambda z: z.reshape(b, n_c, SSD_CHUNK, *z.shape[2:])
    xdt = blk(x * dt[..., None])
    bh = blk(jnp.repeat(bm, rep, axis=2))
    a_dt = jnp.moveaxis(blk(dt * a), 3, 1)
    a_cs = jnp.cumsum(a_dt, axis=-1)
    states = jnp.einsum('bclhn,bhcl,bclhp->bchpn', bh, jnp.exp(a_cs[..., -1:] - a_cs), xdt)
    states = jnp.concatenate([s0[:, None], states], axis=1)
    chunk_decay = jnp.exp(segsum(jnp.pad(a_cs[..., -1], ((0, 0), (0, 0), (1, 0)))))
    new_states = jnp.einsum('bhzc,bchpn->bzhpn', chunk_decay, states)
    if not want_y:
        return None, new_states[:, -1]
    ch = blk(jnp.repeat(cm, rep, axis=2))
    scores = jnp.einsum('bclhn,bcshn->bhcls', ch, bh) * jnp.exp(segsum(a_dt))
    y = jnp.einsum('bhcls,bcshp->bclhp', scores, xdt) + jnp.einsum('bclhn,bchpn,bhcl->bclhp', ch, new_states[:, :-1], jnp.exp(a_cs))
    return y.reshape(b, t, h, p), new_states[:, -1]


def odd_mixer(u_c, u_l, q_norm, w_q_up, kv_norm, w_kv_up, conv_w, conv_b, dt_bias, a_log, d_skip, onorm, cos, sin, need_ctx):
    assert not need_ctx
    att_l = _mla(u_c, u_l, q_norm, w_q_up, kv_norm, w_kv_up, cos, sin)
    ssd = _ssd(u_c, u_l, conv_w, conv_b, dt_bias, a_log, d_skip, onorm)
    return None, jnp.concatenate([att_l, ssd[:, u_c.shape[1]:]], axis=-1)


def ec_moe(x, gain, sc, sh, w_router, w_gate, w_up, w_down, merge_sets):
    b, t, d = x.shape
    cap = CAPACITY_FACTOR * t // N_EXPERTS
    h, aff = _router(x, gain, sc, sh, w_router)
    gate, idx = lax.top_k(aff, cap)
    bidx = jnp.arange(b)[:, None, None]
    xe = h[bidx, idx]
    if merge_sets:
        xe = jnp.swapaxes(xe, 0, 1).reshape(1, N_EXPERTS, b * cap, d)
        ye = _expert_ffn(xe, jnp.swapaxes(gate, 0, 1).reshape(1, N_EXPERTS, b * cap), w_gate, w_up, w_down)
        ye = jnp.swapaxes(ye.reshape(N_EXPERTS, b, cap, d), 0, 1)
    else:
        ye = _expert_ffn(xe, gate, w_gate, w_up, w_down)
    return jnp.zeros((b, t, d), F32).at[bidx, idx].add(ye)


def kernel(x, c, ctx, c_ctx, w_mod, b_mod, norm_mix, norm_ffn, w_out, ev_w_in, hg_lb, hg_onorm, lru_conv_w, lru_conv_b, lru_wa, lru_ba, lru_wx, lru_bx, lru_lambda, od_w_in, mla_q_norm, mla_w_q_up, mla_kv_norm, mla_w_kv_up, ssd_conv_w, ssd_conv_b, ssd_dt_bias, ssd_a_log, ssd_d, ssd_onorm, moe_router, moe_w_gate, moe_w_up, moe_w_down, final_norm):
    rows = x.shape[1] // GRID_W
    cos, sin = axial_rope(rows)
    lb_all = jnp.cumsum(jax.nn.softmax(hg_lb.astype(F32), axis=1), axis=1)
    for l in range(DEPTH):
        need_ctx = l < DEPTH - 1
        j = l // 2
        bsz = x.shape[0]
        c_rows = jnp.concatenate([c, c_ctx[None], jnp.zeros((MOD_ROWS - bsz - 1, D_MODEL), F32)], axis=0)
        mod_all = _modulation(c_rows, w_mod[l], b_mod[l])
        mod, mod_c = mod_all[:bsz], mod_all[bsz]
        sh1, sc1, g1, sh2, sc2, g2 = jnp.split(mod, 6, axis=-1)
        csh1, csc1, cg1, csh2, csc2, cg2 = (jnp.broadcast_to(v, (x.shape[0], D_MODEL)) for v in jnp.split(mod_c, 6, axis=-1))
        wg, wu, wd = moe_w_gate[l].astype(BF16), moe_w_up[l].astype(BF16), moe_w_down[l].astype(BF16)
        wo = w_out[l].astype(BF16)
        if l % 2 == 0:
            w_in = ev_w_in[j].astype(BF16)
            u_c = _proj_in(ctx, norm_mix[l], csc1, csh1, w_in)
            u_l = _proj_in(x, norm_mix[l], sc1, sh1, w_in)
            o_c, o_l = even_mixer(u_c, u_l, lb_all[0, j], lb_all[1, j], hg_onorm[j], lru_conv_w[j], lru_conv_b[j], lru_wa[j], lru_ba[j], lru_wx[j], lru_bx[j], lru_lambda[j], need_ctx)
        else:
            w_in = od_w_in[j]
            u_c = _proj_in(ctx, norm_mix[l], csc1, csh1, _od_weight_layout(w_in, True).astype(BF16))
            u_l = _proj_in(x, norm_mix[l], sc1, sh1, _od_weight_layout(w_in, False).astype(BF16))
            o_c, o_l = odd_mixer(u_c, u_l, mla_q_norm[j], mla_w_q_up[j], mla_kv_norm[j], mla_w_kv_up[j], ssd_conv_w[j], ssd_conv_b[j], ssd_dt_bias[j], ssd_a_log[j], ssd_d[j], ssd_onorm[j], cos, sin, need_ctx)
        x = _proj_out(o_l, wo, x, g1)
        x = x + g2[:, None] * ec_moe(x, norm_ffn[l], sc2, sh2, moe_router[l], wg, wu, wd, False)
        if need_ctx:
            ctx = _proj_out(o_c, wo, ctx, cg1)
            ctx = ctx + cg2[:, None] * ec_moe(ctx, norm_ffn[l], csc2, csh2, moe_router[l], wg, wu, wd, True)
    return _final_norm(x, final_norm)
```

```python
import functools
import math

import jax
import jax.numpy as jnp
from jax import lax
from jax.experimental import pallas as pl
from jax.experimental.pallas import tpu as pltpu

D_MODEL = 2048
DEPTH = 2
CTX_LEN = 256
GRID_W = 64
EPS = 1e-6
F32 = jnp.float32
BF16 = jnp.bfloat16

HG_HEADS = 8
HG_DK = 128
HG_DV = 128
HG_KW = HG_HEADS * HG_DK
HG_VW = HG_HEADS * HG_DV
HG_CHUNK = 64

LRU_WIDTH = 1024
LRU_HEADS = 8
LRU_C = 8.0

MLA_HEADS = 8
MLA_Q_LORA = 512
MLA_KV_LORA = 512
MLA_NOPE = 128
MLA_ROPE = 64
MLA_V = 128
ROPE_THETA = 10000.0
ATTN_BLOCK = 128

SSD_WIDTH = 1024
SSD_HEADDIM = 64
SSD_HEADS = SSD_WIDTH // SSD_HEADDIM
SSD_GROUPS = 2
SSD_STATE = 128
SSD_BC = SSD_GROUPS * SSD_STATE
SSD_XBC = SSD_WIDTH + 2 * SSD_BC
SSD_CHUNK = 128

N_EXPERTS = 16
CAPACITY_FACTOR = 2

EV_OUT = HG_KW + HG_VW + LRU_WIDTH
OD_OUT = MLA_Q_LORA + SSD_WIDTH

V7X_VMEM_LIMIT_BYTES = 56 * 1024 * 1024


def _mm_kernel(a_ref, b_ref, o_ref):
    o_ref[0] = jnp.dot(a_ref[0].astype(BF16), b_ref[0].astype(BF16), preferred_element_type=F32)


def _pick_tile(n, target):
    if n <= target:
        return n
    t = target
    while n % t:
        t //= 2
    return t


def _lane_tile(n, target=1536):
    k = n // 128
    return 128 * max(d for d in range(1, k + 1) if k % d == 0 and 128 * d <= target)


def _bmm(a, b, tm=512, keep_pad=False):
    g, m, k = a.shape
    gb, _, n = b.shape
    tm = _pick_tile(m, tm)
    n_real = n
    cands = [(-(-n // 128) + i) * 128 for i in range(4)]
    n = next((c for c in cands if _lane_tile(c) >= 1024), max(cands, key=_lane_tile))
    if n != n_real:
        b = jnp.pad(b, ((0, 0), (0, 0), (0, n - n_real)))
    tn = _lane_tile(n)
    out = _bmm_call(a, b, g, gb, m, k, n, tm, tn)
    return out if keep_pad or n == n_real else out[..., :n_real]


def _bmm_call(a, b, g, gb, m, k, n, tm, tn):
    return pl.pallas_call(
        _mm_kernel,
        grid=(g, m // tm, n // tn),
        in_specs=[
            pl.BlockSpec((1, tm, k), lambda gi, i, j: (gi, i, 0)),
            pl.BlockSpec((1, k, tn), lambda gi, i, j: (gi % gb, 0, j)),
        ],
        out_specs=pl.BlockSpec((1, tm, tn), lambda gi, i, j: (gi, i, j)),
        out_shape=jax.ShapeDtypeStruct((g, m, n), F32),
        compiler_params=pltpu.CompilerParams(
            dimension_semantics=("parallel", "parallel", "arbitrary"),
            vmem_limit_bytes=V7X_VMEM_LIMIT_BYTES),
    )(a, b)


def _mm(a, b, keep_pad=False):
    lead = a.shape[:-1]
    m = math.prod(lead)
    out = _bmm(a.reshape(1, m, a.shape[-1]), b[None].astype(BF16), keep_pad=keep_pad)
    return out.reshape(*lead, out.shape[-1])


PROJ_TM = 512


def _proj_in_kernel(x_ref, gain_ref, sc_ref, sh_ref, w_ref, o_ref, h_ref):
    @pl.when(pl.program_id(2) == 0)
    def _():
        x = x_ref[0]
        y = x * lax.rsqrt(jnp.mean(x * x, axis=-1, keepdims=True) + EPS) * gain_ref[...]
        h_ref[...] = (y * (1.0 + sc_ref[0]) + sh_ref[0]).astype(BF16)

    o_ref[0] = jnp.dot(h_ref[...], w_ref[...], preferred_element_type=F32)


def _proj_in(x, gain, sc, sh, w):
    bsz, t, dm = x.shape
    n = w.shape[1]
    tm, tn = min(PROJ_TM, t), _lane_tile(n)
    return pl.pallas_call(
        _proj_in_kernel,
        grid=(bsz, t // tm, n // tn),
        in_specs=[
            pl.BlockSpec((1, tm, dm), lambda b, i, j: (b, i, 0)),
            pl.BlockSpec((1, dm), lambda b, i, j: (0, 0)),
            pl.BlockSpec((1, 1, dm), lambda b, i, j: (b, 0, 0)),
            pl.BlockSpec((1, 1, dm), lambda b, i, j: (b, 0, 0)),
            pl.BlockSpec((dm, tn), lambda b, i, j: (0, j)),
        ],
        out_specs=pl.BlockSpec((1, tm, tn), lambda b, i, j: (b, i, j)),
        out_shape=jax.ShapeDtypeStruct((bsz, t, n), F32),
        scratch_shapes=[pltpu.VMEM((tm, dm), BF16)],
        compiler_params=pltpu.CompilerParams(
            dimension_semantics=("parallel", "parallel", "arbitrary"), vmem_limit_bytes=V7X_VMEM_LIMIT_BYTES),
        name="proj_in",
    )(x, gain[None], sc[:, None], sh[:, None], w)


def _proj_out_kernel(ya_ref, yb_ref, wa_ref, wb_ref, x_ref, g_ref, o_ref):
    acc = jnp.dot(ya_ref[0].astype(BF16), wa_ref[...], preferred_element_type=F32)
    acc = acc + jnp.dot(yb_ref[0].astype(BF16), wb_ref[...], preferred_element_type=F32)
    o_ref[0] = x_ref[0] + g_ref[0] * acc


def _proj_out(ya, yb, w, x, g):
    bsz, t, kh = ya.shape
    dm = w.shape[1]
    tm, tn = min(PROJ_TM, t), _lane_tile(dm)
    half = lambda: pl.BlockSpec((1, tm, kh), lambda b, i, j: (b, i, 0))
    return pl.pallas_call(
        _proj_out_kernel,
        grid=(bsz, t // tm, dm // tn),
        in_specs=[
            half(), half(),
            pl.BlockSpec((kh, tn), lambda b, i, j: (0, j)),
            pl.BlockSpec((kh, tn), lambda b, i, j: (1, j)),
            pl.BlockSpec((1, tm, tn), lambda b, i, j: (b, i, j)),
            pl.BlockSpec((1, 1, tn), lambda b, i, j: (b, 0, j)),
        ],
        out_specs=pl.BlockSpec((1, tm, tn), lambda b, i, j: (b, i, j)),
        out_shape=jax.ShapeDtypeStruct((bsz, t, dm), F32),
        compiler_params=pltpu.CompilerParams(
            dimension_semantics=("parallel", "parallel", "arbitrary"), vmem_limit_bytes=V7X_VMEM_LIMIT_BYTES),
        name="proj_out",
    )(ya, yb, w, w, x, g[:, None])


MOD_ROWS = 8
MOD_TN = 1024


def _mod_kernel(c_ref, w_ref, b_ref, o_ref):
    o_ref[...] = jnp.dot(jax.nn.silu(c_ref[...]).astype(BF16), w_ref[0].astype(BF16), preferred_element_type=F32) + b_ref[0]


def _modulation(c_rows, w, b, layer):
    _, dm, n = w.shape
    return pl.pallas_call(
        _mod_kernel,
        grid=(n // MOD_TN,),
        in_specs=[
            pl.BlockSpec((MOD_ROWS, dm), lambda j: (0, 0)),
            pl.BlockSpec((1, dm, MOD_TN), lambda j: (layer, 0, j)),
            pl.BlockSpec((1, 1, MOD_TN), lambda j: (layer, 0, j)),
        ],
        out_specs=pl.BlockSpec((MOD_ROWS, MOD_TN), lambda j: (0, j)),
        out_shape=jax.ShapeDtypeStruct((MOD_ROWS, n), F32),
        compiler_params=pltpu.CompilerParams(dimension_semantics=("parallel",), vmem_limit_bytes=V7X_VMEM_LIMIT_BYTES),
        name="modulation",
    )(c_rows, w, b[:, None])


def _final_norm_kernel(x_ref, g_ref, o_ref):
    x = x_ref[0]
    o_ref[0] = x * lax.rsqrt(jnp.mean(x * x, axis=-1, keepdims=True) + EPS) * g_ref[...]


def _final_norm(x, gain):
    bsz, t, dm = x.shape
    return pl.pallas_call(
        _final_norm_kernel,
        grid=(bsz, t // PROJ_TM),
        in_specs=[pl.BlockSpec((1, PROJ_TM, dm), lambda b, i: (b, i, 0)), pl.BlockSpec((1, dm), lambda b, i: (0, 0))],
        out_specs=pl.BlockSpec((1, PROJ_TM, dm), lambda b, i: (b, i, 0)),
        out_shape=jax.ShapeDtypeStruct(x.shape, F32),
        compiler_params=pltpu.CompilerParams(dimension_semantics=("parallel", "parallel"), vmem_limit_bytes=V7X_VMEM_LIMIT_BYTES),
        name="final_norm",
    )(x, gain[None])


def _router_kernel(x_ref, gain_ref, sc_ref, sh_ref, wr_ref, h_ref, aff_ref):
    x = x_ref[0]
    y = x * lax.rsqrt(jnp.mean(x * x, axis=-1, keepdims=True) + EPS) * gain_ref[...]
    h = (y * (1.0 + sc_ref[0]) + sh_ref[0]).astype(BF16)
    h_ref[0] = h
    logits = lax.dot_general(wr_ref[...], h, (((1,), (1,)), ((), ())), preferred_element_type=F32)
    e = jnp.exp(logits - jnp.max(logits, axis=0, keepdims=True))
    aff_ref[0] = e / jnp.sum(e, axis=0, keepdims=True)


def _router(x, gain, sc, sh, w_router):
    bsz, t, dm = x.shape
    tm = min(PROJ_TM, t)
    return pl.pallas_call(
        _router_kernel,
        grid=(bsz, t // tm),
        in_specs=[
            pl.BlockSpec((1, tm, dm), lambda b, i: (b, i, 0)),
            pl.BlockSpec((1, dm), lambda b, i: (0, 0)),
            pl.BlockSpec((1, 1, dm), lambda b, i: (b, 0, 0)),
            pl.BlockSpec((1, 1, dm), lambda b, i: (b, 0, 0)),
            pl.BlockSpec((N_EXPERTS, dm), lambda b, i: (0, 0)),
        ],
        out_specs=[
            pl.BlockSpec((1, tm, dm), lambda b, i: (b, i, 0)),
            pl.BlockSpec((1, N_EXPERTS, tm), lambda b, i: (b, 0, i)),
        ],
        out_shape=[jax.ShapeDtypeStruct((bsz, t, dm), BF16), jax.ShapeDtypeStruct((bsz, N_EXPERTS, t), F32)],
        compiler_params=pltpu.CompilerParams(dimension_semantics=("parallel", "parallel"), vmem_limit_bytes=V7X_VMEM_LIMIT_BYTES),
        name="moe_router",
    )(x, gain[None], sc[:, None], sh[:, None], w_router.T.astype(BF16))


def _ffn_kernel(x_ref, gate_ref, g2_ref, wg_ref, wu_ref, wd_ref, o_ref):
    x = x_ref[0, 0]
    a = jnp.dot(x, wg_ref[0, 0], preferred_element_type=F32)
    u = jnp.dot(x, wu_ref[0, 0], preferred_element_type=F32)
    hid = (jax.nn.silu(a) * u).astype(BF16)
    o_ref[0, 0] = jnp.dot(hid, wd_ref[0, 0], preferred_element_type=F32) * gate_ref[0, 0] * g2_ref[0, 0]


def _expert_ffn(xe, gate, g2, w_gate, w_up, w_down, layer):
    s, e, c, dm = xe.shape
    ff = w_gate.shape[-1]
    g2_rows = g2.shape[1]
    return pl.pallas_call(
        _ffn_kernel,
        grid=(e, s),
        in_specs=[
            pl.BlockSpec((1, 1, c, dm), lambda ei, si: (si, ei, 0, 0)),
            pl.BlockSpec((1, 1, c, 1), lambda ei, si: (si, ei, 0, 0)),
            pl.BlockSpec((1, 1, g2_rows, dm), lambda ei, si: (si, 0, 0, 0)),
            pl.BlockSpec((1, 1, dm, ff), lambda ei, si: (layer, ei, 0, 0)),
            pl.BlockSpec((1, 1, dm, ff), lambda ei, si: (layer, ei, 0, 0)),
            pl.BlockSpec((1, 1, ff, dm), lambda ei, si: (layer, ei, 0, 0)),
        ],
        out_specs=pl.BlockSpec((1, 1, c, dm), lambda ei, si: (si, ei, 0, 0)),
        out_shape=jax.ShapeDtypeStruct((s, e, c, dm), F32),
        compiler_params=pltpu.CompilerParams(dimension_semantics=("parallel", "arbitrary"), vmem_limit_bytes=V7X_VMEM_LIMIT_BYTES),
        name="moe_ffn",
    )(xe, gate[..., None], g2[:, None], w_gate, w_up, w_down)


LRU_BLOCK = LRU_WIDTH // LRU_HEADS
LRU_L = 128
SUBLANES = 8
LRU_PAD = SUBLANES


def _lru_kernel(seq, xc_ref, gc_ref, xl_ref, gl_ref, cw_ref, cb_ref, wa_ref, wx_ref, ba_ref, bx_ref, lam_ref,
                oc_ref, ol_ref, xs_ref, hf_ref):
    t_c, t_l = seq
    n_c, n_l = t_c // LRU_L, t_l // LRU_L
    off_c, off_l = LRU_PAD, 2 * LRU_PAD + t_c
    zeros = jnp.zeros((LRU_PAD, LRU_BLOCK), F32)
    xs_ref[0:LRU_PAD] = zeros
    xs_ref[off_c + t_c:off_l] = zeros
    xs_ref[off_l + t_l:off_l + t_l + LRU_PAD] = zeros
    xs_ref[off_c:off_c + t_c] = xc_ref[0]
    xs_ref[off_l:off_l + t_l] = xl_ref[0]

    cw = cw_ref[...]
    cb = cb_ref[...]
    row = lax.broadcasted_iota(jnp.int32, (LRU_L // SUBLANES, SUBLANES, LRU_BLOCK), 1)

    def gates(r0, d):
        win = xs_ref[pl.ds(r0 - LRU_PAD, LRU_L + 2 * LRU_PAD), :]
        x = cb
        for k in range(4):
            x = x + cw[k:k + 1] * win[LRU_PAD - 2 + k:LRU_PAD - 2 + k + LRU_L]
        xb = x.astype(BF16)
        r = jax.nn.sigmoid(jnp.dot(xb, wa_ref[d, 0].astype(BF16), preferred_element_type=F32) + ba_ref[d:d + 1])
        ig = jax.nn.sigmoid(jnp.dot(xb, wx_ref[d, 0].astype(BF16), preferred_element_type=F32) + bx_ref[d:d + 1])
        log_a = -LRU_C * r * jax.nn.softplus(-lam_ref[d:d + 1])
        a = jnp.exp(log_a)
        return a, jnp.sqrt(1.0 - a * a) * ig * x

    def chunk_scan(a, b, carry, reverse):
        n = LRU_L // SUBLANES
        a = a.reshape(n, SUBLANES, LRU_BLOCK)
        b = b.reshape(n, SUBLANES, LRU_BLOCK)
        for s in (1, 2, 4):
            if reverse:
                keep = row < SUBLANES - s
                shift = SUBLANES - s
            else:
                keep = row >= s
                shift = s
            a_n = jnp.where(keep, pltpu.roll(a, shift, 1), 1.0)
            b_n = jnp.where(keep, pltpu.roll(b, shift, 1), 0.0)
            b = a * b_n + b
            a = a * a_n
        outs = [None] * n
        last = 0 if reverse else SUBLANES - 1
        for j in (range(n - 1, -1, -1) if reverse else range(n)):
            h = b[j] + a[j] * carry
            carry = h[last:last + 1]
            outs[j] = h
        return jnp.concatenate(outs, axis=0), carry

    def fwd_chunk(t0, r0, carry):
        a, b = gates(r0, 0)
        h, carry = chunk_scan(a, b, carry, False)
        hf_ref[pl.ds(t0, LRU_L), :] = h
        return carry

    def bwd_chunk(t0, r0, carry):
        a, b = gates(r0, 1)
        h, carry = chunk_scan(a, b, carry, True)
        return hf_ref[pl.ds(t0, LRU_L), :] + h, carry

    carry = jnp.zeros((1, LRU_BLOCK), F32)
    for c in range(n_c):
        carry = fwd_chunk(c * LRU_L, off_c + c * LRU_L, carry)

    def fwd_body(c, carry):
        t0 = pl.multiple_of(c * LRU_L, LRU_L)
        return fwd_chunk(t_c + t0, off_l + t0, carry)

    lax.fori_loop(0, n_l, fwd_body, carry)

    carry = jnp.zeros((1, LRU_BLOCK), F32)
    for c in range(n_c - 1, -1, -1):
        h, carry = bwd_chunk(c * LRU_L, off_c + c * LRU_L, carry)
        oc_ref[0, c * LRU_L:(c + 1) * LRU_L, :] = h * jax.nn.gelu(gc_ref[0, c * LRU_L:(c + 1) * LRU_L, :])

    def bwd_body(i, carry):
        t0 = pl.multiple_of((n_l - 1 - i) * LRU_L, LRU_L)
        h, carry = bwd_chunk(t_c + t0, off_l + t0, carry)
        ol_ref[0, pl.ds(t0, LRU_L), :] = h * jax.nn.gelu(gl_ref[0, pl.ds(t0, LRU_L), :])
        return carry

    lax.fori_loop(0, n_l, bwd_body, carry)


def _rglru(u_c, u_l, conv_w, conv_b, wa, ba, wx, bx, lam):
    bsz, t_c, _ = u_c.shape
    t_l = u_l.shape[1]
    x_blk = (EV_OUT + 2 * HG_KW + HG_VW) // LRU_BLOCK
    g_blk = (HG_KW + HG_VW) // LRU_BLOCK
    t_pad = t_c + t_l + 3 * LRU_PAD
    vec = lambda: pl.BlockSpec((2, LRU_BLOCK), lambda b, h: (0, h))
    return pl.pallas_call(
        functools.partial(_lru_kernel, (t_c, t_l)),
        grid=(bsz, LRU_HEADS),
        in_specs=[
            pl.BlockSpec((1, t_c, LRU_BLOCK), lambda b, h: (b, 0, x_blk + h)),
            pl.BlockSpec((1, t_c, LRU_BLOCK), lambda b, h: (b, 0, g_blk + h)),
            pl.BlockSpec((1, t_l, LRU_BLOCK), lambda b, h: (b, 0, x_blk + h)),
            pl.BlockSpec((1, t_l, LRU_BLOCK), lambda b, h: (b, 0, g_blk + h)),
            pl.BlockSpec((4, LRU_BLOCK), lambda b, h: (0, h)),
            pl.BlockSpec((1, LRU_BLOCK), lambda b, h: (0, h)),
            pl.BlockSpec((2, 1, LRU_BLOCK, LRU_BLOCK), lambda b, h: (0, h, 0, 0)),
            pl.BlockSpec((2, 1, LRU_BLOCK, LRU_BLOCK), lambda b, h: (0, h, 0, 0)),
            vec(), vec(), vec(),
        ],
        out_specs=[
            pl.BlockSpec((1, t_c, LRU_BLOCK), lambda b, h: (b, 0, h)),
            pl.BlockSpec((1, t_l, LRU_BLOCK), lambda b, h: (b, 0, h)),
        ],
        out_shape=[jax.ShapeDtypeStruct((bsz, t_c, LRU_WIDTH), F32), jax.ShapeDtypeStruct((bsz, t_l, LRU_WIDTH), F32)],
        scratch_shapes=[pltpu.VMEM((t_pad, LRU_BLOCK), F32), pltpu.VMEM((t_c + t_l, LRU_BLOCK), F32)],
        compiler_params=pltpu.CompilerParams(
            dimension_semantics=("parallel", "parallel"), vmem_limit_bytes=V7X_VMEM_LIMIT_BYTES),
        name="rglru",
    )(u_c, u_c, u_l, u_l, conv_w, conv_b[None], wa, wx, ba, bx, lam)


HG_GROUP = 8


def _hgrn2_kernel(seq, qc_ref, gc_ref, ffc_ref, fbc_ref, vc_ref, ql_ref, gl_ref, ffl_ref, fbl_ref, vl_ref,
                  lbf_ref, lbb_ref, gain_ref, oc_ref, ol_ref, of_ref):
    t_c, t_l = seq
    L = HG_CHUNK
    R = HG_GROUP * L
    n_gc = t_c // L
    r_i = lax.broadcasted_iota(jnp.int32, (R, R), 0)
    c_i = lax.broadcasted_iota(jnp.int32, (R, R), 1)
    same = (r_i // L) == (c_i // L)
    causal = (same & (c_i <= r_i), same & (c_i >= r_i))
    tri = tuple(jnp.where(m, 1.0, 0.0).astype(BF16) for m in causal)
    lb = (lbf_ref[...], lbb_ref[...])
    gain = gain_ref[...]

    def group(d, n, q_raw, f_raw, v, st):
        rows = n * L
        f = lb[d] + (1.0 - lb[d]) * jax.nn.sigmoid(f_raw)
        logf = jnp.log(f)
        k = 1.0 - f
        hi = logf.astype(BF16)
        r1 = logf - hi.astype(F32)
        mid = r1.astype(BF16)
        lo = (r1 - mid.astype(F32)).astype(BF16)
        g3 = jnp.dot(tri[d][:rows, :rows], jnp.concatenate([hi, mid, lo], axis=1), preferred_element_type=F32)
        g = g3[:, :HG_DK] + g3[:, HG_DK:2 * HG_DK] + g3[:, 2 * HG_DK:]
        g4 = g.reshape(n, L, HG_DK)
        g_last = g4[:, 0:1] if d else g4[:, L - 1:L]
        g_rest = (jnp.broadcast_to(g_last, (n, L, HG_DK)) - g4).reshape(rows, HG_DK)
        qg = (jax.nn.silu(q_raw) * jnp.exp(g)).astype(BF16)
        kg = (k * jnp.exp(-g)).astype(BF16)
        kd = (k * jnp.exp(g_rest)).astype(BF16)
        vb = v.astype(BF16)
        att = lax.dot_general(qg, kg, (((1,), (1,)), ((), ())), preferred_element_type=F32)
        att = jnp.where(causal[d][:rows, :rows], att, 0.0).astype(BF16)
        o = jnp.dot(att, vb, preferred_element_type=F32)
        dec = jnp.exp(g_last)
        sl = [slice(j * L, (j + 1) * L) for j in range(n)]
        ds_t = [lax.dot_general(vb[s], kd[s], (((0,), (0,)), ((), ())), preferred_element_type=F32) for s in sl]
        entering = [None] * n
        for j in (range(n - 1, -1, -1) if d else range(n)):
            entering[j] = st.astype(BF16)
            st = dec[j] * st + ds_t[j]
        outs = [o[s] + lax.dot_general(qg[s], e, (((1,), (1,)), ((), ())), preferred_element_type=F32)
                for s, e in zip(sl, entering)]
        return jnp.concatenate(outs, axis=0), st

    def finish(o, gate):
        return _rms(o, gain) * jax.nn.silu(gate)

    def lat_rows(i):
        return pl.ds(pl.multiple_of(i * R, R), R), pl.ds(pl.multiple_of(t_c + i * R, math.gcd(t_c, R)), R)

    st = jnp.zeros((HG_DV, HG_DK), F32)
    o, st = group(0, n_gc, qc_ref[0], ffc_ref[0], vc_ref[0], st)
    of_ref[0:t_c, :] = o

    def fwd_body(i, st):
        rows, srows = lat_rows(i)
        o, st = group(0, HG_GROUP, ql_ref[0, rows, :], ffl_ref[0, rows, :], vl_ref[0, rows, :], st)
        of_ref[srows, :] = o
        return st

    lax.fori_loop(0, t_l // R, fwd_body, st)

    st = jnp.zeros((HG_DV, HG_DK), F32)
    o, st = group(1, n_gc, qc_ref[0], fbc_ref[0], vc_ref[0], st)
    oc_ref[0] = finish(of_ref[0:t_c, :] + o, gc_ref[0])

    def bwd_body(i, st):
        rows, srows = lat_rows(t_l // R - 1 - i)
        o, st = group(1, HG_GROUP, ql_ref[0, rows, :], fbl_ref[0, rows, :], vl_ref[0, rows, :], st)
        ol_ref[0, rows, :] = finish(of_ref[srows, :] + o, gl_ref[0, rows, :])
        return st

    lax.fori_loop(0, t_l // R, bwd_body, st)


def _hgrn2(u_c, u_l, lb_f, lb_b, onorm):
    bsz, t_c, _ = u_c.shape
    t_l = u_l.shape[1]
    nb = HG_KW // HG_DK
    col = lambda t, blk: pl.BlockSpec((1, t, HG_DK), lambda b, h: (b, 0, blk + h))
    seq_cols = lambda t: [col(t, 0), col(t, nb), col(t, EV_OUT // HG_DK), col(t, EV_OUT // HG_DK + nb), col(t, EV_OUT // HG_DK + 2 * nb)]
    vec = lambda: pl.BlockSpec((1, HG_DK), lambda b, h: (0, h))
    return pl.pallas_call(
        functools.partial(_hgrn2_kernel, (t_c, t_l)),
        grid=(bsz, HG_HEADS),
        in_specs=seq_cols(t_c) + seq_cols(t_l) + [vec(), vec(), vec()],
        out_specs=[
            pl.BlockSpec((1, t_c, HG_DV), lambda b, h: (b, 0, h)),
            pl.BlockSpec((1, t_l, HG_DV), lambda b, h: (b, 0, h)),
        ],
        out_shape=[jax.ShapeDtypeStruct((bsz, t_c, HG_VW), F32), jax.ShapeDtypeStruct((bsz, t_l, HG_VW), F32)],
        scratch_shapes=[pltpu.VMEM((t_c + t_l, HG_DV), F32)],
        compiler_params=pltpu.CompilerParams(
            dimension_semantics=("parallel", "parallel"), vmem_limit_bytes=V7X_VMEM_LIMIT_BYTES),
        name="hgrn2",
    )(*([u_c] * 5), *([u_l] * 5), lb_f[None], lb_b[None], onorm[None])


LANES = 128
MLA_HW = 2 * LANES
MLA_TM = 512
MLA_TQ = 512
MLA_TK = 512
MLA_UNROLL = 2
MLA_VW = MLA_V + LANES
MLA_SCALE_LOG2E = (MLA_NOPE + MLA_ROPE) ** -0.5 * math.log2(math.e)


def _rms(x, gain):
    return x * lax.rsqrt(jnp.mean(x * x, axis=-1, keepdims=True) + EPS) * gain


def _rope_tile(r, cos_t, sin_t):
    return r * cos_t + (pltpu.roll(r, MLA_ROPE // 2, 1) + pltpu.roll(r, LANES - MLA_ROPE // 2, 1)) * sin_t


def _mla_q_kernel(u_ref, gain_ref, w_ref, cos_ref, sin_ref, q_ref):
    q = jnp.dot(_rms(u_ref[0], gain_ref[...]).astype(BF16), w_ref[...], preferred_element_type=F32)
    for h in range(MLA_HEADS):
        lo = h * MLA_HW
        q_ref[0, :, lo:lo + LANES] = q[:, lo:lo + LANES].astype(BF16)
        q_ref[0, :, lo + LANES:lo + MLA_HW] = _rope_tile(q[:, lo + LANES:lo + MLA_HW], cos_ref[...], sin_ref[...]).astype(BF16)


def _mla_kv_kernel(u_ref, kr_ref, gain_ref, w_ref, cos_ref, sin_ref, k_ref, v_ref):
    kv = jnp.dot(_rms(u_ref[0], gain_ref[...]).astype(BF16), w_ref[...], preferred_element_type=F32)
    kr = _rope_tile(kr_ref[0], cos_ref[...], sin_ref[...]).astype(BF16)
    for h in range(MLA_HEADS):
        k_ref[0, :, h * MLA_HW:h * MLA_HW + LANES] = kv[:, h * LANES:(h + 1) * LANES].astype(BF16)
        k_ref[0, :, h * MLA_HW + LANES:(h + 1) * MLA_HW] = kr
    lane = lax.broadcasted_iota(jnp.int32, (kv.shape[0], LANES), 1)
    one_hot = jnp.where(lane == 0, 1.0, 0.0).astype(BF16)
    for h in range(MLA_HEADS):
        v_ref[0, :, h * MLA_VW:h * MLA_VW + MLA_V] = kv[:, (MLA_HEADS + h) * LANES:(MLA_HEADS + h + 1) * LANES].astype(BF16)
        v_ref[0, :, h * MLA_VW + MLA_V:(h + 1) * MLA_VW] = one_hot


def _mla_q(u, gain, w, cos_t, sin_t):
    bsz, t, _ = u.shape
    tm = min(MLA_TM, t)
    return pl.pallas_call(
        _mla_q_kernel,
        grid=(bsz, t // tm),
        in_specs=[
            pl.BlockSpec((1, tm, MLA_Q_LORA), lambda b, i: (b, i, 0)),
            pl.BlockSpec((1, MLA_Q_LORA), lambda b, i: (0, 0)),
            pl.BlockSpec((MLA_Q_LORA, MLA_HEADS * MLA_HW), lambda b, i: (0, 0)),
            pl.BlockSpec((tm, LANES), lambda b, i: (i, 0)),
            pl.BlockSpec((tm, LANES), lambda b, i: (i, 0)),
        ],
        out_specs=pl.BlockSpec((1, tm, MLA_HEADS * MLA_HW), lambda b, i: (b, i, 0)),
        out_shape=jax.ShapeDtypeStruct((bsz, t, MLA_HEADS * MLA_HW), BF16),
        compiler_params=pltpu.CompilerParams(
            dimension_semantics=("parallel", "parallel"), vmem_limit_bytes=V7X_VMEM_LIMIT_BYTES),
        name="mla_q_up",
    )(u, gain[None], w, cos_t, sin_t)


def _mla_kv(u, c_blk, r_blk, gain, w, cos_t, sin_t):
    bsz, t, _ = u.shape
    tm = min(MLA_TM, t)
    return pl.pallas_call(
        _mla_kv_kernel,
        grid=(bsz, t // tm),
        in_specs=[
            pl.BlockSpec((1, tm, MLA_KV_LORA), lambda b, i: (b, i, c_blk)),
            pl.BlockSpec((1, tm, LANES), lambda b, i: (b, i, r_blk)),
            pl.BlockSpec((1, MLA_KV_LORA), lambda b, i: (0, 0)),
            pl.BlockSpec((MLA_KV_LORA, MLA_HEADS * (LANES + MLA_V)), lambda b, i: (0, 0)),
            pl.BlockSpec((tm, LANES), lambda b, i: (i, 0)),
            pl.BlockSpec((tm, LANES), lambda b, i: (i, 0)),
        ],
        out_specs=[
            pl.BlockSpec((1, tm, MLA_HEADS * MLA_HW), lambda b, i: (b, i, 0)),
            pl.BlockSpec((1, tm, MLA_HEADS * MLA_VW), lambda b, i: (b, i, 0)),
        ],
        out_shape=[jax.ShapeDtypeStruct((bsz, t, MLA_HEADS * MLA_HW), BF16),
                   jax.ShapeDtypeStruct((bsz, t, MLA_HEADS * MLA_VW), BF16)],
        compiler_params=pltpu.CompilerParams(
            dimension_semantics=("parallel", "parallel"), vmem_limit_bytes=V7X_VMEM_LIMIT_BYTES),
        name="mla_kv_up",
    )(u, u, gain[None], w, cos_t, sin_t)


def _flash_kernel(t_l, q_ref, kc_ref, vc_ref, kl_ref, vl_ref, o_ref):
    q = q_ref[0]

    tiles = [(kc_ref, vc_ref, slice(None))] + [(kl_ref, vl_ref, slice(i * MLA_TK, (i + 1) * MLA_TK)) for i in range(t_l // MLA_TK)]

    def scores(tile):
        k_ref, _, rows = tile
        return lax.dot_general(q, k_ref[0, rows, :], (((1,), (1,)), ((), ())), preferred_element_type=F32)

    tq = q.shape[0]
    m = jnp.full((tq, 1), -jnp.inf, F32)
    acc = jnp.zeros((tq, MLA_VW), F32)
    s_next = scores(tiles[0])
    for i, (_, v_ref, rows) in enumerate(tiles):
        s = s_next * MLA_SCALE_LOG2E
        if i + 1 < len(tiles):
            s_next = scores(tiles[i + 1])
        m_new = jnp.maximum(m, jnp.max(s, axis=-1, keepdims=True))
        p = jnp.exp2(s - m_new)
        acc = jnp.exp2(m - m_new) * acc + jnp.dot(p.astype(BF16), v_ref[0, rows, :], preferred_element_type=F32)
        m = m_new
    o_ref[0] = acc[:, :MLA_V] / acc[:, MLA_V:MLA_V + 1]


def _flash(q, k_c, v_c, k_l, v_l):
    bsz, t, _ = q.shape
    t_c, t_l = k_c.shape[1], k_l.shape[1]
    return pl.pallas_call(
        functools.partial(_flash_kernel, t_l),
        grid=(bsz, MLA_HEADS, t // MLA_TQ),
        in_specs=[
            pl.BlockSpec((1, MLA_TQ, MLA_HW), lambda b, h, i: (b, i, h)),
            pl.BlockSpec((1, t_c, MLA_HW), lambda b, h, i: (b, 0, h)),
            pl.BlockSpec((1, t_c, MLA_VW), lambda b, h, i: (b, 0, h)),
            pl.BlockSpec((1, t_l, MLA_HW), lambda b, h, i: (b, 0, h)),
            pl.BlockSpec((1, t_l, MLA_VW), lambda b, h, i: (b, 0, h)),
        ],
        out_specs=pl.BlockSpec((1, MLA_TQ, MLA_V), lambda b, h, i: (b, i, h)),
        out_shape=jax.ShapeDtypeStruct((bsz, t, MLA_HEADS * MLA_V), F32),
        compiler_params=pltpu.CompilerParams(
            dimension_semantics=("parallel", "parallel", "arbitrary"), vmem_limit_bytes=V7X_VMEM_LIMIT_BYTES),
        name="mla_flash",
    )(q, k_c, v_c, k_l, v_l)


def _mla(u_c, u_l, q_norm, w_q_up, kv_norm, w_kv_up, cos, sin):
    t_c, t_l = u_c.shape[1], u_l.shape[1]
    zeros = jnp.zeros((t_l, LANES - MLA_ROPE), F32)
    cos_l = jnp.concatenate([cos, cos, zeros], axis=-1)
    sin_l = jnp.concatenate([-sin, sin, zeros], axis=-1)
    cos_c = jnp.concatenate([jnp.ones((t_c, MLA_ROPE), F32), zeros[:t_c]], axis=-1)
    sin_c = jnp.zeros((t_c, LANES), F32)
    wq = w_q_up.reshape(MLA_Q_LORA, MLA_HEADS, MLA_NOPE + MLA_ROPE)
    wq = jnp.pad(wq, ((0, 0), (0, 0), (0, MLA_HW - MLA_NOPE - MLA_ROPE))).reshape(MLA_Q_LORA, -1).astype(BF16)
    wkv = w_kv_up.reshape(MLA_KV_LORA, MLA_HEADS, MLA_NOPE + MLA_V)
    wkv = jnp.concatenate([wkv[..., :MLA_NOPE].reshape(MLA_KV_LORA, -1), wkv[..., MLA_NOPE:].reshape(MLA_KV_LORA, -1)], axis=-1).astype(BF16)
    q = _mla_q(u_l, q_norm, wq, cos_l, sin_l)
    k_l, v_l = _mla_kv(u_l, OD_OUT // MLA_KV_LORA, (OD_OUT + MLA_KV_LORA) // LANES, kv_norm, wkv, cos_l, sin_l)
    k_c, v_c = _mla_kv(u_c, 0, MLA_KV_LORA // LANES, kv_norm, wkv, cos_c, sin_c)
    return _flash(q, k_c, v_c, k_l, v_l)


OD_BLK_Z = MLA_Q_LORA // LANES
OD_BLK_CKV = OD_OUT // LANES
OD_BLK_KR = OD_BLK_CKV + MLA_KV_LORA // LANES
OD_BLK_XBC = OD_BLK_KR + 1
SSD_XBC_BLKS = SSD_XBC // LANES
SSD_ACT_BLKS = SSD_XBC_BLKS + SSD_GROUPS
SSD_GH = SSD_HEADS // SSD_GROUPS
SSD_GW = SSD_WIDTH // SSD_GROUPS
SSD_TILE = 2 * SSD_CHUNK
SSD_CONV_ROWS = 512


def _od_weight_layout(w_in, ctx_only):
    k = w_in.shape[0]
    z = lambda n: jnp.zeros((k, n), w_in.dtype)
    o = OD_OUT + MLA_KV_LORA
    xbc0 = o + MLA_ROPE
    dt0 = xbc0 + SSD_XBC
    cols = [] if ctx_only else [w_in[:, :OD_OUT]]
    cols += [w_in[:, OD_OUT:o], w_in[:, o:xbc0], z(LANES - MLA_ROPE), w_in[:, xbc0:dt0]]
    for g in range(SSD_GROUPS):
        cols += [w_in[:, dt0 + g * SSD_GH:dt0 + (g + 1) * SSD_GH],
                 w_in[:, dt0 + SSD_HEADS + g * SSD_GH:dt0 + SSD_HEADS + (g + 1) * SSD_GH], z(LANES - 2 * SSD_GH)]
    w = jnp.concatenate(cols, axis=1)
    return jnp.pad(w, ((0, 0), (0, -w.shape[1] % (10 * LANES) if ctx_only else -w.shape[1] % (8 * LANES))))


def _ssd_group_lanes(v):
    rows = [jnp.concatenate([v[0, g * SSD_GH:(g + 1) * SSD_GH], v[1, g * SSD_GH:(g + 1) * SSD_GH],
                             jnp.zeros((LANES - 2 * SSD_GH,), v.dtype)]) for g in range(SSD_GROUPS)]
    return jnp.stack(rows)


def _ssd_act_kernel(seq, xc_ref, xl_ref, cw_ref, cb_ref, o_ref, xs_ref):
    t_c, t_l = seq
    j = pl.program_id(1)

    @pl.when(j < SSD_XBC_BLKS)
    def _():
        off_c, off_l = LRU_PAD, 2 * LRU_PAD + t_c
        zeros = jnp.zeros((LRU_PAD, LANES), F32)
        xs_ref[0:LRU_PAD] = zeros
        xs_ref[off_c + t_c:off_l] = zeros
        xs_ref[off_l + t_l:off_l + t_l + LRU_PAD] = zeros
        xs_ref[off_c:off_c + t_c] = xc_ref[0]
        xs_ref[off_l:off_l + t_l] = xl_ref[0]
        cw = cw_ref[...]
        cb = cb_ref[...]

        def conv(r0, rows):
            win = xs_ref[pl.ds(r0 - LRU_PAD, rows + 2 * LRU_PAD), :]
            y = cb
            for k in range(4):
                y = y + cw[k:k + 1] * win[LRU_PAD - 2 + k:LRU_PAD - 2 + k + rows]
            return jax.nn.silu(y)

        o_ref[0, 0:t_c, :] = conv(off_c, t_c)

        def body(i, _):
            r0 = pl.multiple_of(i * SSD_CONV_ROWS, SSD_CONV_ROWS)
            o_ref[0, pl.ds(t_c + r0, SSD_CONV_ROWS), :] = conv(off_l + r0, SSD_CONV_ROWS)
            return 0

        lax.fori_loop(0, t_l // SSD_CONV_ROWS, body, 0)

    @pl.when(j >= SSD_XBC_BLKS)
    def _():
        o_ref[0, 0:t_c, :] = jax.nn.softplus(xc_ref[0] + cb_ref[...])
        o_ref[0, t_c:t_c + t_l, :] = jax.nn.softplus(xl_ref[0] + cb_ref[...])


def _ssd_act(u_c, u_l, conv_w, conv_b, dt_bias):
    bsz, t_c, _ = u_c.shape
    t_l = u_l.shape[1]
    cw = jnp.pad(conv_w, ((0, 0), (0, SSD_GROUPS * LANES)))
    cb = jnp.concatenate([conv_b, _ssd_group_lanes(dt_bias).reshape(-1)])[None]
    ctx_blk = OD_BLK_XBC - OD_BLK_CKV
    return pl.pallas_call(
        functools.partial(_ssd_act_kernel, (t_c, t_l)),
        grid=(bsz, SSD_ACT_BLKS),
        in_specs=[
            pl.BlockSpec((1, t_c, LANES), lambda b, j: (b, 0, ctx_blk + j)),
            pl.BlockSpec((1, t_l, LANES), lambda b, j: (b, 0, OD_BLK_XBC + j)),
            pl.BlockSpec((4, LANES), lambda b, j: (0, j)),
            pl.BlockSpec((1, LANES), lambda b, j: (0, j)),
        ],
        out_specs=pl.BlockSpec((1, t_c + t_l, LANES), lambda b, j: (b, 0, j)),
        out_shape=jax.ShapeDtypeStruct((bsz, t_c + t_l, SSD_ACT_BLKS * LANES), F32),
        scratch_shapes=[pltpu.VMEM((t_c + t_l + 3 * LRU_PAD, LANES), F32)],
        compiler_params=pltpu.CompilerParams(
            dimension_semantics=("parallel", "parallel"), vmem_limit_bytes=V7X_VMEM_LIMIT_BYTES),
        name="ssd_act",
    )(u_c, u_l, cw, cb)


def _split3(x):
    hi = x.astype(BF16)
    r1 = x - hi.astype(F32)
    mid = r1.astype(BF16)
    return hi, mid, (r1 - mid.astype(F32)).astype(BF16)


def _ssd_scan_kernel(d, n_tiles, *refs):
    if d == 0:
        x_ref, b_ref, c_ref, dt_ref, alog_ref, y_ref, s_ref = refs
    else:
        x_ref, b_ref, c_ref, dt_ref, alog_ref, yf_ref, z_ref, dskip_ref, gain_ref, y_ref, s_ref = refs
    L = SSD_CHUNK
    P = SSD_HEADDIM

    @pl.when(pl.program_id(2) == 0)
    def _():
        s_ref[...] = jnp.zeros_like(s_ref)

    r_i = lax.broadcasted_iota(jnp.int32, (L, L), 0)
    c_i = lax.broadcasted_iota(jnp.int32, (L, L), 1)
    causal = (c_i >= r_i) if d else (c_i <= r_i)
    tri = jnp.where(causal, 1.0, 0.0).astype(BF16)
    low_half = lax.broadcasted_iota(jnp.int32, (L, LANES), 1) < P
    a_lane = -jnp.exp(alog_ref[...])

    def pair_cols(cols, p):
        return jnp.where(low_half, cols[2 * p], cols[2 * p + 1])

    for ch in ((1, 0) if d else (0, 1)):
        rows = slice(ch * L, (ch + 1) * L)
        xs = x_ref[0, rows, :]
        bm = b_ref[0, rows, :].astype(BF16)
        cm = c_ref[0, rows, :].astype(BF16)
        dt = dt_ref[0, rows, :]
        a3 = _split3(dt * a_lane)
        cs3 = jnp.dot(tri, jnp.concatenate(a3, axis=1), preferred_element_type=F32)
        cs_col = cs3[:, :LANES] + cs3[:, LANES:2 * LANES] + cs3[:, 2 * LANES:]
        cs_row = sum(lax.dot_general(t, tri, (((0,), (1,)), ((), ())), preferred_element_type=F32) for t in a3)
        cb = lax.dot_general(cm, bm, (((1,), (1,)), ((), ())), preferred_element_type=F32)
        s_in = s_ref[...].astype(BF16)
        y_off = lax.dot_general(cm, s_in, (((1,), (1,)), ((), ())), preferred_element_type=F32)
        last = 0 if d else L - 1
        cols, dts, lasts = [], [], []
        for i in range(SSD_GH):
            lane = SSD_GH * d + i
            cols.append(cs_col[:, lane:lane + 1])
            dts.append(dt[:, lane:lane + 1])
            lasts.append(cs_col[last:last + 1, lane:lane + 1])
        ys, xdecs = [], []
        for p in range(SSD_GH // 2):
            xp = xs[:, p * LANES:(p + 1) * LANES]
            col2 = pair_cols(cols, p)
            xdt = xp * pair_cols(dts, p)
            xdt_b = xdt.astype(BF16)
            y2 = []
            for e in range(2):
                i = 2 * p + e
                lane = SSD_GH * d + i
                decay = jnp.where(causal, jnp.exp(cols[i] - cs_row[lane:lane + 1, :]), 0.0)
                y2.append(jnp.dot((cb * decay).astype(BF16), xdt_b, preferred_element_type=F32))
            ys.append(jnp.where(low_half, y2[0], y2[1]) + y_off[:, p * LANES:(p + 1) * LANES] * jnp.exp(col2))
            last2 = jnp.where(low_half, lasts[2 * p], lasts[2 * p + 1])
            xdecs.append((xdt * jnp.exp(last2 - col2)).astype(BF16))
        y = jnp.concatenate(ys, axis=1)
        ds = lax.dot_general(jnp.concatenate(xdecs, axis=1), bm, (((0,), (0,)), ((), ())), preferred_element_type=F32)
        dec = jnp.concatenate([jnp.broadcast_to(jnp.exp(lasts[i]), (P, SSD_STATE)) for i in range(SSD_GH)], axis=0)
        s_ref[...] = dec * s_ref[...] + ds
        if d == 0:
            y_ref[0, rows, :] = y
        else:
            out = (yf_ref[0, rows, :] + y + dskip_ref[...] * xs) * jax.nn.silu(z_ref[0, rows, :])
            y_ref[0, rows, :] = _rms(out, gain_ref[...])


def _ssd_scan(d, act, t_c, a_lanes, extra=()):
    bsz, t, _ = act.shape
    n_tiles = t // SSD_TILE
    n_ctx = t_c // SSD_TILE
    assert n_ctx == 1
    if d == 0:
        tile = lambda i: i
    else:
        tile = lambda i: jnp.where(i < n_ctx, n_ctx - 1 - i, n_tiles + n_ctx - 1 - i)
    in_specs = [
        pl.BlockSpec((1, SSD_TILE, SSD_GW), lambda b, g, i: (b, tile(i), g)),
        pl.BlockSpec((1, SSD_TILE, LANES), lambda b, g, i: (b, tile(i), SSD_WIDTH // LANES + g)),
        pl.BlockSpec((1, SSD_TILE, LANES), lambda b, g, i: (b, tile(i), (SSD_WIDTH + SSD_BC) // LANES + g)),
        pl.BlockSpec((1, SSD_TILE, LANES), lambda b, g, i: (b, tile(i), SSD_XBC_BLKS + g)),
        pl.BlockSpec((1, LANES), lambda b, g, i: (0, g)),
    ]
    args = [act, act, act, act, a_lanes.reshape(1, -1)]
    if d:
        y_f, u_l, d_skip, onorm = extra
        in_specs += [
            pl.BlockSpec((1, SSD_TILE, SSD_GW), lambda b, g, i: (b, tile(i), g)),
            pl.BlockSpec((1, SSD_TILE, SSD_GW), lambda b, g, i: (b, jnp.maximum(tile(i) - n_ctx, 0), OD_BLK_Z * LANES // SSD_GW + g)),
            pl.BlockSpec((1, SSD_GW), lambda b, g, i: (0, g)),
            pl.BlockSpec((1, SSD_GW), lambda b, g, i: (0, g)),
        ]
        args += [y_f, u_l, jnp.repeat(d_skip, SSD_HEADDIM)[None], onorm[None]]
        out_tile = lambda i: jnp.where(i < n_ctx, n_tiles - n_ctx - 1, tile(i) - n_ctx)
        out_rows = t - t_c
    else:
        out_tile, out_rows = tile, t
    return pl.pallas_call(
        functools.partial(_ssd_scan_kernel, d, n_tiles),
        grid=(bsz, SSD_GROUPS, n_tiles),
        in_specs=in_specs,
        out_specs=pl.BlockSpec((1, SSD_TILE, SSD_GW), lambda b, g, i: (b, out_tile(i), g)),
        out_shape=jax.ShapeDtypeStruct((bsz, out_rows, SSD_WIDTH), F32),
        scratch_shapes=[pltpu.VMEM((SSD_GW, SSD_STATE), F32)],
        compiler_params=pltpu.CompilerParams(
            dimension_semantics=("parallel", "parallel", "arbitrary"), vmem_limit_bytes=V7X_VMEM_LIMIT_BYTES),
        name="ssd_scan_bwd" if d else "ssd_scan_fwd",
    )(*args)


def _ssd(u_c, u_l, conv_w, conv_b, dt_bias, a_log, d_skip, onorm):
    act = _ssd_act(u_c, u_l, conv_w, conv_b, dt_bias)
    a_lanes = _ssd_group_lanes(a_log)
    y_f = _ssd_scan(0, act, u_c.shape[1], a_lanes)
    return _ssd_scan(1, act, u_c.shape[1], a_lanes, (y_f, u_l, d_skip, onorm))


def rmsnorm(x, g):
    xf = x.astype(F32)
    y = xf * lax.rsqrt(jnp.mean(xf * xf, axis=-1, keepdims=True) + EPS)
    return (y * g.astype(F32)).astype(x.dtype)


def heads(t, h):
    return t.reshape(t.shape[0], t.shape[1], h, t.shape[-1] // h)


def flip(t):
    return jnp.flip(t, axis=1)


def dwconv(x, w, b):
    k = w.shape[0]
    y = lax.conv_general_dilated(x, w[:, None, :].astype(x.dtype), window_strides=(1,), padding=[(k // 2, k - 1 - k // 2)], dimension_numbers=('NWC', 'WIO', 'NWC'), feature_group_count=x.shape[-1])
    return y + b.astype(x.dtype)


def axial_rope(rows):
    row = jnp.repeat(jnp.arange(rows, dtype=F32), GRID_W)
    col = jnp.arange(rows * GRID_W) % GRID_W
    n_freq = MLA_ROPE // 4
    inv = ROPE_THETA ** (-jnp.arange(n_freq, dtype=F32) / n_freq)
    ang = jnp.concatenate([row[:, None] * inv, col.astype(F32)[:, None] * inv], axis=-1)
    return jnp.cos(ang), jnp.sin(ang)


def apply_rope(t, cos, sin):
    t1, t2 = jnp.split(t.astype(F32), 2, axis=-1)
    cs, sn = cos[None, :, None, :], sin[None, :, None, :]
    return jnp.concatenate([t1 * cs - t2 * sn, t1 * sn + t2 * cs], axis=-1).astype(t.dtype)


def segsum(x):
    t = x.shape[-1]
    xx = jnp.broadcast_to(x[..., :, None], x.shape + (t,))
    xx = jnp.where(jnp.tril(jnp.ones((t, t), bool), -1), xx, 0.0)
    cs = jnp.cumsum(xx, axis=-2)
    return jnp.where(jnp.tril(jnp.ones((t, t), bool)), cs, -jnp.inf)


def gla_chunked(q, k, v, logf, s0):
    b, t, h, _ = k.shape
    dv = v.shape[-1]
    n = t // HG_CHUNK
    blk = lambda a: a.reshape(b, n, HG_CHUNK, h, a.shape[-1])
    k, v, g = blk(k), blk(v), jnp.cumsum(blk(logf), axis=2)
    g_last = g[:, :, -1:]
    ds = jnp.einsum('bnmhd,bnmhe->bnhde', k * jnp.exp(g_last - g), v)
    decay = jnp.exp(g_last[:, :, 0])

    def step(s, inp):
        dec, d = inp
        return dec[..., None] * s + d, s

    s_fin, s_prev = lax.scan(step, s0, (jnp.moveaxis(decay, 1, 0), jnp.moveaxis(ds, 1, 0)))
    if q is None:
        return None, s_fin
    qg = blk(q) * jnp.exp(g)
    att = jnp.einsum('bnlhd,bnmhd->bnhlm', qg, k * jnp.exp(-g))
    att = jnp.where(jnp.tril(jnp.ones((HG_CHUNK, HG_CHUNK), bool)), att, 0.0)
    o = jnp.einsum('bnhlm,bnmhe->bnlhe', att, v) + jnp.einsum('bnlhd,nbhde->bnlhe', qg, s_prev)
    return o.reshape(b, t, h, dv), s_fin


def hgrn2_scan(q_c, f_c, i_c, q_l, f_l, i_l, lb):
    def gates(fr):
        f = lb + (1.0 - lb) * jax.nn.sigmoid(fr.astype(F32))
        return heads(jnp.log(f), HG_HEADS), heads(1.0 - f, HG_HEADS)

    s0 = jnp.zeros((f_l.shape[0], HG_HEADS, HG_DK, HG_DV), F32)
    lf, k = gates(f_c)
    o_c, s_c = gla_chunked(q_c, k, i_c, lf, s0)
    lf, k = gates(f_l)
    o_l, _ = gla_chunked(q_l, k, i_l, lf, s_c)
    return o_c, o_l


def even_mixer(u_c, u_l, lb_f, lb_b, onorm, conv_w, conv_b, wa, ba, wx, bx, lam, need_ctx):
    assert need_ctx
    lru_c, lru_l = _rglru(u_c, u_l, conv_w, conv_b, wa, ba, wx, bx, lam)
    hg_c, hg_l = _hgrn2(u_c, u_l, lb_f, lb_b, onorm)
    return (hg_c, lru_c), (hg_l, lru_l)


def block_attention(q, k, v):
    b, t, h, dq = q.shape
    nb = t // ATTN_BLOCK
    scale = dq ** -0.5
    qb = jnp.moveaxis(q.reshape(b, nb, ATTN_BLOCK, h, dq), 1, 0)

    def one(qblk):
        s = jnp.einsum('bqhd,bkhd->bhqk', qblk, k).astype(F32) * scale
        p = jax.nn.softmax(s, axis=-1).astype(v.dtype)
        return jnp.einsum('bhqk,bkhe->bqhe', p, v)

    o = lax.map(one, qb)
    return jnp.moveaxis(o, 0, 1).reshape(b, t, h, v.shape[-1])


def ssd_chunked(x, dt, a, bm, cm, s0, want_y):
    b, t, h, p = x.shape
    n_c = t // SSD_CHUNK
    rep = h // bm.shape[2]
    blk = lambda z: z.reshape(b, n_c, SSD_CHUNK, *z.shape[2:])
    xdt = blk(x * dt[..., None])
    bh = blk(jnp.repeat(bm, rep, axis=2))
    a_dt = jnp.moveaxis(blk(dt * a), 3, 1)
    a_cs = jnp.cumsum(a_dt, axis=-1)
    states = jnp.einsum('bclhn,bhcl,bclhp->bchpn', bh, jnp.exp(a_cs[..., -1:] - a_cs), xdt)
    states = jnp.concatenate([s0[:, None], states], axis=1)
    chunk_decay = jnp.exp(segsum(jnp.pad(a_cs[..., -1], ((0, 0), (0, 0), (1, 0)))))
    new_states = jnp.einsum('bhzc,bchpn->bzhpn', chunk_decay, states)
    if not want_y:
        return None, new_states[:, -1]
    ch = blk(jnp.repeat(cm, rep, axis=2))
    scores = jnp.einsum('bclhn,bcshn->bhcls', ch, bh) * jnp.exp(segsum(a_dt))
    y = jnp.einsum('bhcls,bcshp->bclhp', scores, xdt) + jnp.einsum('bclhn,bchpn,bhcl->bclhp', ch, new_states[:, :-1], jnp.exp(a_cs))
    return y.reshape(b, t, h, p), new_states[:, -1]


def odd_mixer(u_c, u_l, q_norm, w_q_up, kv_norm, w_kv_up, conv_w, conv_b, dt_bias, a_log, d_skip, onorm, cos, sin, need_ctx):
    assert not need_ctx
    att_l = _mla(u_c, u_l, q_norm, w_q_up, kv_norm, w_kv_up, cos, sin)
    ssd_l = _ssd(u_c, u_l, conv_w, conv_b, dt_bias, a_log, d_skip, onorm)
    return None, (att_l, ssd_l)


def ec_moe(x, gain, sc, sh, g2, w_router, w_gate, w_up, w_down, layer, merge_sets):
    b, t, d = x.shape
    cap = CAPACITY_FACTOR * t // N_EXPERTS
    h, aff = _router(x, gain, sc, sh, w_router)
    gate, idx = lax.top_k(aff, cap)
    bidx = jnp.arange(b)[:, None, None]
    xe = h[bidx, idx]
    if merge_sets:
        xe = jnp.swapaxes(xe, 0, 1).reshape(1, N_EXPERTS, b * cap, d)
        gate_m = jnp.swapaxes(gate, 0, 1).reshape(1, N_EXPERTS, b * cap)
        ye = _expert_ffn(xe, gate_m, jnp.repeat(g2, cap, axis=0)[None], w_gate, w_up, w_down, layer)
        ye = jnp.swapaxes(ye.reshape(N_EXPERTS, b, cap, d), 0, 1)
    else:
        ye = _expert_ffn(xe, gate, g2[:, None], w_gate, w_up, w_down, layer)
    return x.at[bidx, idx].add(ye)


def kernel(x, c, ctx, c_ctx, w_mod, b_mod, norm_mix, norm_ffn, w_out, ev_w_in, hg_lb, hg_onorm, lru_conv_w, lru_conv_b, lru_wa, lru_ba, lru_wx, lru_bx, lru_lambda, od_w_in, mla_q_norm, mla_w_q_up, mla_kv_norm, mla_w_kv_up, ssd_conv_w, ssd_conv_b, ssd_dt_bias, ssd_a_log, ssd_d, ssd_onorm, moe_router, moe_w_gate, moe_w_up, moe_w_down, final_norm):
    rows = x.shape[1] // GRID_W
    cos, sin = axial_rope(rows)
    lb_all = jnp.cumsum(jax.nn.softmax(hg_lb.astype(F32), axis=1), axis=1)
    bsz = x.shape[0]
    c_rows = jnp.concatenate([c, c_ctx[None], jnp.zeros((MOD_ROWS - bsz - 1, D_MODEL), F32)], axis=0)
    wg, wu, wd = moe_w_gate.astype(BF16), moe_w_up.astype(BF16), moe_w_down.astype(BF16)
    for l in range(DEPTH):
        need_ctx = l < DEPTH - 1
        j = l // 2
        mod_all = _modulation(c_rows, w_mod, b_mod, l)
        mod, mod_c = mod_all[:bsz], mod_all[bsz]
        sh1, sc1, g1, sh2, sc2, g2 = jnp.split(mod, 6, axis=-1)
        csh1, csc1, cg1, csh2, csc2, cg2 = (jnp.broadcast_to(v, (x.shape[0], D_MODEL)) for v in jnp.split(mod_c, 6, axis=-1))
        wo = w_out[l].astype(BF16)
        if l % 2 == 0:
            w_in = ev_w_in[j].astype(BF16)
            u_c = _proj_in(ctx, norm_mix[l], csc1, csh1, w_in)
            u_l = _proj_in(x, norm_mix[l], sc1, sh1, w_in)
            o_c, o_l = even_mixer(u_c, u_l, lb_all[0, j], lb_all[1, j], hg_onorm[j], lru_conv_w[j], lru_conv_b[j], lru_wa[j], lru_ba[j], lru_wx[j], lru_bx[j], lru_lambda[j], need_ctx)
        else:
            w_in = od_w_in[j]
            u_c = _proj_in(ctx, norm_mix[l], csc1, csh1, _od_weight_layout(w_in, True).astype(BF16))
            u_l = _proj_in(x, norm_mix[l], sc1, sh1, _od_weight_layout(w_in, False).astype(BF16))
            o_c, o_l = odd_mixer(u_c, u_l, mla_q_norm[j], mla_w_q_up[j], mla_kv_norm[j], mla_w_kv_up[j], ssd_conv_w[j], ssd_conv_b[j], ssd_dt_bias[j], ssd_a_log[j], ssd_d[j], ssd_onorm[j], cos, sin, need_ctx)
        x = _proj_out(*o_l, wo, x, g1)
        x = ec_moe(x, norm_ffn[l], sc2, sh2, g2, moe_router[l], wg, wu, wd, l, False)
        if need_ctx:
            ctx = _proj_out(*o_c, wo, ctx, cg1)
            ctx = ec_moe(ctx, norm_ffn[l], csc2, csh2, cg2, moe_router[l], wg, wu, wd, l, True)
    return _final_norm(x, final_norm)
```

```python
import functools
import math

import jax
import jax.numpy as jnp
from jax import lax
from jax.experimental import pallas as pl
from jax.experimental.pallas import tpu as pltpu

D_MODEL = 2048
DEPTH = 2
CTX_LEN = 256
GRID_W = 64
EPS = 1e-6
F32 = jnp.float32
BF16 = jnp.bfloat16

HG_HEADS = 8
HG_DK = 128
HG_DV = 128
HG_KW = HG_HEADS * HG_DK
HG_VW = HG_HEADS * HG_DV
HG_CHUNK = 64

LRU_WIDTH = 1024
LRU_HEADS = 8
LRU_C = 8.0

MLA_HEADS = 8
MLA_Q_LORA = 512
MLA_KV_LORA = 512
MLA_NOPE = 128
MLA_ROPE = 64
MLA_V = 128
ROPE_THETA = 10000.0
ATTN_BLOCK = 128

SSD_WIDTH = 1024
SSD_HEADDIM = 64
SSD_HEADS = SSD_WIDTH // SSD_HEADDIM
SSD_GROUPS = 2
SSD_STATE = 128
SSD_BC = SSD_GROUPS * SSD_STATE
SSD_XBC = SSD_WIDTH + 2 * SSD_BC
SSD_CHUNK = 128

N_EXPERTS = 16
CAPACITY_FACTOR = 2

EV_OUT = HG_KW + HG_VW + LRU_WIDTH
OD_OUT = MLA_Q_LORA + SSD_WIDTH

V7X_VMEM_LIMIT_BYTES = 56 * 1024 * 1024


def _mm_kernel(a_ref, b_ref, o_ref):
    o_ref[0] = jnp.dot(a_ref[0].astype(BF16), b_ref[0].astype(BF16), preferred_element_type=F32)


def _pick_tile(n, target):
    if n <= target:
        return n
    t = target
    while n % t:
        t //= 2
    return t


def _lane_tile(n, target=1536):
    k = n // 128
    return 128 * max(d for d in range(1, k + 1) if k % d == 0 and 128 * d <= target)


def _bmm(a, b, tm=512, keep_pad=False):
    g, m, k = a.shape
    gb, _, n = b.shape
    tm = _pick_tile(m, tm)
    n_real = n
    cands = [(-(-n // 128) + i) * 128 for i in range(4)]
    n = next((c for c in cands if _lane_tile(c) >= 1024), max(cands, key=_lane_tile))
    if n != n_real:
        b = jnp.pad(b, ((0, 0), (0, 0), (0, n - n_real)))
    tn = _lane_tile(n)
    out = _bmm_call(a, b, g, gb, m, k, n, tm, tn)
    return out if keep_pad or n == n_real else out[..., :n_real]


def _bmm_call(a, b, g, gb, m, k, n, tm, tn):
    return pl.pallas_call(
        _mm_kernel,
        grid=(g, m // tm, n // tn),
        in_specs=[
            pl.BlockSpec((1, tm, k), lambda gi, i, j: (gi, i, 0)),
            pl.BlockSpec((1, k, tn), lambda gi, i, j: (gi % gb, 0, j)),
        ],
        out_specs=pl.BlockSpec((1, tm, tn), lambda gi, i, j: (gi, i, j)),
        out_shape=jax.ShapeDtypeStruct((g, m, n), F32),
        compiler_params=pltpu.CompilerParams(
            dimension_semantics=("parallel", "parallel", "arbitrary"),
            vmem_limit_bytes=V7X_VMEM_LIMIT_BYTES),
    )(a, b)


def _mm(a, b, keep_pad=False):
    lead = a.shape[:-1]
    m = math.prod(lead)
    out = _bmm(a.reshape(1, m, a.shape[-1]), b[None].astype(BF16), keep_pad=keep_pad)
    return out.reshape(*lead, out.shape[-1])


PROJ_TM = 1024


def _proj_in_kernel(x_ref, gain_ref, sc_ref, sh_ref, w_ref, o_ref, h_ref):
    @pl.when(pl.program_id(2) == 0)
    def _():
        x = x_ref[0]
        y = x * lax.rsqrt(jnp.mean(x * x, axis=-1, keepdims=True) + EPS) * gain_ref[...]
        h_ref[...] = (y * (1.0 + sc_ref[0]) + sh_ref[0]).astype(BF16)

    o_ref[0] = jnp.dot(h_ref[...], w_ref[...], preferred_element_type=F32)


def _proj_in(x, gain, sc, sh, w):
    bsz, t, dm = x.shape
    n = w.shape[1]
    tm, tn = min(PROJ_TM, t), _lane_tile(n)
    return pl.pallas_call(
        _proj_in_kernel,
        grid=(bsz, t // tm, n // tn),
        in_specs=[
            pl.BlockSpec((1, tm, dm), lambda b, i, j: (b, i, 0)),
            pl.BlockSpec((1, dm), lambda b, i, j: (0, 0)),
            pl.BlockSpec((1, 1, dm), lambda b, i, j: (b, 0, 0)),
            pl.BlockSpec((1, 1, dm), lambda b, i, j: (b, 0, 0)),
            pl.BlockSpec((dm, tn), lambda b, i, j: (0, j)),
        ],
        out_specs=pl.BlockSpec((1, tm, tn), lambda b, i, j: (b, i, j)),
        out_shape=jax.ShapeDtypeStruct((bsz, t, n), F32),
        scratch_shapes=[pltpu.VMEM((tm, dm), BF16)],
        compiler_params=pltpu.CompilerParams(
            dimension_semantics=("parallel", "parallel", "arbitrary"), vmem_limit_bytes=V7X_VMEM_LIMIT_BYTES),
        name="proj_in",
    )(x, gain[None], sc[:, None], sh[:, None], w)


def _proj_out_kernel(ya_ref, yb_ref, wa_ref, wb_ref, x_ref, g_ref, o_ref):
    acc = jnp.dot(ya_ref[0].astype(BF16), wa_ref[...], preferred_element_type=F32)
    acc = acc + jnp.dot(yb_ref[0].astype(BF16), wb_ref[...], preferred_element_type=F32)
    o_ref[0] = x_ref[0] + g_ref[0] * acc


def _proj_out(ya, yb, w, x, g):
    bsz, t, kh = ya.shape
    dm = w.shape[1]
    tm, tn = min(PROJ_TM, t), _lane_tile(dm)
    half = lambda: pl.BlockSpec((1, tm, kh), lambda b, i, j: (b, i, 0))
    return pl.pallas_call(
        _proj_out_kernel,
        grid=(bsz, t // tm, dm // tn),
        in_specs=[
            half(), half(),
            pl.BlockSpec((kh, tn), lambda b, i, j: (0, j)),
            pl.BlockSpec((kh, tn), lambda b, i, j: (1, j)),
            pl.BlockSpec((1, tm, tn), lambda b, i, j: (b, i, j)),
            pl.BlockSpec((1, 1, tn), lambda b, i, j: (b, 0, j)),
        ],
        out_specs=pl.BlockSpec((1, tm, tn), lambda b, i, j: (b, i, j)),
        out_shape=jax.ShapeDtypeStruct((bsz, t, dm), F32),
        compiler_params=pltpu.CompilerParams(
            dimension_semantics=("parallel", "parallel", "arbitrary"), vmem_limit_bytes=V7X_VMEM_LIMIT_BYTES),
        name="proj_out",
    )(ya, yb, w, w, x, g[:, None])


MOD_ROWS = 8
MOD_TN = 1024


def _mod_kernel(c_ref, w_ref, b_ref, o_ref):
    o_ref[...] = jnp.dot(jax.nn.silu(c_ref[...]).astype(BF16), w_ref[0].astype(BF16), preferred_element_type=F32) + b_ref[0]


def _modulation(c_rows, w, b, layer):
    _, dm, n = w.shape
    return pl.pallas_call(
        _mod_kernel,
        grid=(n // MOD_TN,),
        in_specs=[
            pl.BlockSpec((MOD_ROWS, dm), lambda j: (0, 0)),
            pl.BlockSpec((1, dm, MOD_TN), lambda j: (layer, 0, j)),
            pl.BlockSpec((1, 1, MOD_TN), lambda j: (layer, 0, j)),
        ],
        out_specs=pl.BlockSpec((MOD_ROWS, MOD_TN), lambda j: (0, j)),
        out_shape=jax.ShapeDtypeStruct((MOD_ROWS, n), F32),
        compiler_params=pltpu.CompilerParams(dimension_semantics=("parallel",), vmem_limit_bytes=V7X_VMEM_LIMIT_BYTES),
        name="modulation",
    )(c_rows, w, b[:, None])


def _final_norm_kernel(x_ref, g_ref, o_ref):
    x = x_ref[0]
    o_ref[0] = x * lax.rsqrt(jnp.mean(x * x, axis=-1, keepdims=True) + EPS) * g_ref[...]


def _final_norm(x, gain):
    bsz, t, dm = x.shape
    return pl.pallas_call(
        _final_norm_kernel,
        grid=(bsz, t // PROJ_TM),
        in_specs=[pl.BlockSpec((1, PROJ_TM, dm), lambda b, i: (b, i, 0)), pl.BlockSpec((1, dm), lambda b, i: (0, 0))],
        out_specs=pl.BlockSpec((1, PROJ_TM, dm), lambda b, i: (b, i, 0)),
        out_shape=jax.ShapeDtypeStruct(x.shape, F32),
        compiler_params=pltpu.CompilerParams(dimension_semantics=("parallel", "parallel"), vmem_limit_bytes=V7X_VMEM_LIMIT_BYTES),
        name="final_norm",
    )(x, gain[None])


def _router_kernel(x_ref, gain_ref, sc_ref, sh_ref, wr_ref, h_ref, aff_ref):
    x = x_ref[0]
    y = x * lax.rsqrt(jnp.mean(x * x, axis=-1, keepdims=True) + EPS) * gain_ref[...]
    h = (y * (1.0 + sc_ref[0]) + sh_ref[0]).astype(BF16)
    h_ref[0] = h
    logits = lax.dot_general(wr_ref[...], h, (((1,), (1,)), ((), ())), preferred_element_type=F32)
    e = jnp.exp(logits - jnp.max(logits, axis=0, keepdims=True))
    aff_ref[0] = e / jnp.sum(e, axis=0, keepdims=True)


def _router(x, gain, sc, sh, w_router):
    bsz, t, dm = x.shape
    tm = min(PROJ_TM, t)
    return pl.pallas_call(
        _router_kernel,
        grid=(bsz, t // tm),
        in_specs=[
            pl.BlockSpec((1, tm, dm), lambda b, i: (b, i, 0)),
            pl.BlockSpec((1, dm), lambda b, i: (0, 0)),
            pl.BlockSpec((1, 1, dm), lambda b, i: (b, 0, 0)),
            pl.BlockSpec((1, 1, dm), lambda b, i: (b, 0, 0)),
            pl.BlockSpec((N_EXPERTS, dm), lambda b, i: (0, 0)),
        ],
        out_specs=[
            pl.BlockSpec((1, tm, dm), lambda b, i: (b, i, 0)),
            pl.BlockSpec((1, N_EXPERTS, tm), lambda b, i: (b, 0, i)),
        ],
        out_shape=[jax.ShapeDtypeStruct((bsz, t, dm), BF16), jax.ShapeDtypeStruct((bsz, N_EXPERTS, t), F32)],
        compiler_params=pltpu.CompilerParams(dimension_semantics=("parallel", "parallel"), vmem_limit_bytes=V7X_VMEM_LIMIT_BYTES),
        name="moe_router",
    )(x, gain[None], sc[:, None], sh[:, None], w_router.T.astype(BF16))


def _ffn_kernel(x_ref, gate_ref, g2_ref, wg_ref, wu_ref, wd_ref, o_ref):
    x = x_ref[0, 0]
    a = jnp.dot(x, wg_ref[0, 0], preferred_element_type=F32)
    u = jnp.dot(x, wu_ref[0, 0], preferred_element_type=F32)
    hid = (jax.nn.silu(a) * u).astype(BF16)
    o_ref[0, 0] = jnp.dot(hid, wd_ref[0, 0], preferred_element_type=F32) * gate_ref[0, 0] * g2_ref[0, 0]


def _expert_ffn(xe, gate, g2, w_gate, w_up, w_down, layer):
    s, e, c, dm = xe.shape
    ff = w_gate.shape[-1]
    g2_rows = g2.shape[1]
    return pl.pallas_call(
        _ffn_kernel,
        grid=(e, s),
        in_specs=[
            pl.BlockSpec((1, 1, c, dm), lambda ei, si: (si, ei, 0, 0)),
            pl.BlockSpec((1, 1, c, 1), lambda ei, si: (si, ei, 0, 0)),
            pl.BlockSpec((1, 1, g2_rows, dm), lambda ei, si: (si, 0, 0, 0)),
            pl.BlockSpec((1, 1, dm, ff), lambda ei, si: (layer, ei, 0, 0)),
            pl.BlockSpec((1, 1, dm, ff), lambda ei, si: (layer, ei, 0, 0)),
            pl.BlockSpec((1, 1, ff, dm), lambda ei, si: (layer, ei, 0, 0)),
        ],
        out_specs=pl.BlockSpec((1, 1, c, dm), lambda ei, si: (si, ei, 0, 0)),
        out_shape=jax.ShapeDtypeStruct((s, e, c, dm), F32),
        compiler_params=pltpu.CompilerParams(dimension_semantics=("parallel", "arbitrary"), vmem_limit_bytes=V7X_VMEM_LIMIT_BYTES),
        name="moe_ffn",
    )(xe, gate[..., None], g2[:, None], w_gate, w_up, w_down)


LRU_BLOCK = LRU_WIDTH // LRU_HEADS
LRU_L = 128
SUBLANES = 8
LRU_PAD = SUBLANES
LRU_UNROLL = 4


def _lru_kernel(seq, xc_ref, gc_ref, xl_ref, gl_ref, cw_ref, cb_ref, wa_ref, wx_ref, ba_ref, bx_ref, lam_ref,
                oc_ref, ol_ref, xs_ref, hf_ref):
    t_c, t_l = seq
    n_c, n_l = t_c // LRU_L, t_l // LRU_L
    off_c, off_l = LRU_PAD, 2 * LRU_PAD + t_c
    zeros = jnp.zeros((LRU_PAD, LRU_BLOCK), F32)
    xs_ref[0:LRU_PAD] = zeros
    xs_ref[off_c + t_c:off_l] = zeros
    xs_ref[off_l + t_l:off_l + t_l + LRU_PAD] = zeros
    xs_ref[off_c:off_c + t_c] = xc_ref[0]
    xs_ref[off_l:off_l + t_l] = xl_ref[0]

    cw = cw_ref[...]
    cb = cb_ref[...]
    row = lax.broadcasted_iota(jnp.int32, (LRU_L // SUBLANES, SUBLANES, LRU_BLOCK), 1)

    def gates(r0, d):
        win = xs_ref[pl.ds(r0 - LRU_PAD, LRU_L + 2 * LRU_PAD), :]
        x = cb
        for k in range(4):
            x = x + cw[k:k + 1] * win[LRU_PAD - 2 + k:LRU_PAD - 2 + k + LRU_L]
        xb = x.astype(BF16)
        r = jax.nn.sigmoid(jnp.dot(xb, wa_ref[d, 0].astype(BF16), preferred_element_type=F32) + ba_ref[d:d + 1])
        ig = jax.nn.sigmoid(jnp.dot(xb, wx_ref[d, 0].astype(BF16), preferred_element_type=F32) + bx_ref[d:d + 1])
        log_a = -LRU_C * r * jax.nn.softplus(-lam_ref[d:d + 1])
        a = jnp.exp(log_a)
        return a, jnp.sqrt(1.0 - a * a) * ig * x

    def chunk_scan(a, b, carry, reverse):
        n = LRU_L // SUBLANES
        a = a.reshape(n, SUBLANES, LRU_BLOCK)
        b = b.reshape(n, SUBLANES, LRU_BLOCK)
        for s in (1, 2, 4):
            if reverse:
                keep = row < SUBLANES - s
                shift = SUBLANES - s
            else:
                keep = row >= s
                shift = s
            a_n = jnp.where(keep, pltpu.roll(a, shift, 1), 1.0)
            b_n = jnp.where(keep, pltpu.roll(b, shift, 1), 0.0)
            b = a * b_n + b
            a = a * a_n
        outs = [None] * n
        last = 0 if reverse else SUBLANES - 1
        for j in (range(n - 1, -1, -1) if reverse else range(n)):
            h = b[j] + a[j] * carry
            carry = h[last:last + 1]
            outs[j] = h
        return jnp.concatenate(outs, axis=0), carry

    def fwd_chunk(t0, r0, carry):
        a, b = gates(r0, 0)
        h, carry = chunk_scan(a, b, carry, False)
        hf_ref[pl.ds(t0, LRU_L), :] = h
        return carry

    def bwd_chunk(t0, r0, carry):
        a, b = gates(r0, 1)
        h, carry = chunk_scan(a, b, carry, True)
        return hf_ref[pl.ds(t0, LRU_L), :] + h, carry

    carry = jnp.zeros((1, LRU_BLOCK), F32)
    for c in range(n_c):
        carry = fwd_chunk(c * LRU_L, off_c + c * LRU_L, carry)

    def fwd_body(c, carry):
        t0 = pl.multiple_of(c * LRU_L, LRU_L)
        return fwd_chunk(t_c + t0, off_l + t0, carry)

    lax.fori_loop(0, n_l, fwd_body, carry, unroll=LRU_UNROLL)

    carry = jnp.zeros((1, LRU_BLOCK), F32)
    for c in range(n_c - 1, -1, -1):
        h, carry = bwd_chunk(c * LRU_L, off_c + c * LRU_L, carry)
        oc_ref[0, c * LRU_L:(c + 1) * LRU_L, :] = h * jax.nn.gelu(gc_ref[0, c * LRU_L:(c + 1) * LRU_L, :])

    def bwd_body(i, carry):
        t0 = pl.multiple_of((n_l - 1 - i) * LRU_L, LRU_L)
        h, carry = bwd_chunk(t_c + t0, off_l + t0, carry)
        ol_ref[0, pl.ds(t0, LRU_L), :] = h * jax.nn.gelu(gl_ref[0, pl.ds(t0, LRU_L), :])
        return carry

    lax.fori_loop(0, n_l, bwd_body, carry, unroll=LRU_UNROLL)


def _rglru(u_c, u_l, conv_w, conv_b, wa, ba, wx, bx, lam):
    bsz, t_c, _ = u_c.shape
    t_l = u_l.shape[1]
    x_blk = (EV_OUT + 2 * HG_KW + HG_VW) // LRU_BLOCK
    g_blk = (HG_KW + HG_VW) // LRU_BLOCK
    t_pad = t_c + t_l + 3 * LRU_PAD
    vec = lambda: pl.BlockSpec((2, LRU_BLOCK), lambda b, h: (0, h))
    return pl.pallas_call(
        functools.partial(_lru_kernel, (t_c, t_l)),
        grid=(bsz, LRU_HEADS),
        in_specs=[
            pl.BlockSpec((1, t_c, LRU_BLOCK), lambda b, h: (b, 0, x_blk + h)),
            pl.BlockSpec((1, t_c, LRU_BLOCK), lambda b, h: (b, 0, g_blk + h)),
            pl.BlockSpec((1, t_l, LRU_BLOCK), lambda b, h: (b, 0, x_blk + h)),
            pl.BlockSpec((1, t_l, LRU_BLOCK), lambda b, h: (b, 0, g_blk + h)),
            pl.BlockSpec((4, LRU_BLOCK), lambda b, h: (0, h)),
            pl.BlockSpec((1, LRU_BLOCK), lambda b, h: (0, h)),
            pl.BlockSpec((2, 1, LRU_BLOCK, LRU_BLOCK), lambda b, h: (0, h, 0, 0)),
            pl.BlockSpec((2, 1, LRU_BLOCK, LRU_BLOCK), lambda b, h: (0, h, 0, 0)),
            vec(), vec(), vec(),
        ],
        out_specs=[
            pl.BlockSpec((1, t_c, LRU_BLOCK), lambda b, h: (b, 0, h)),
            pl.BlockSpec((1, t_l, LRU_BLOCK), lambda b, h: (b, 0, h)),
        ],
        out_shape=[jax.ShapeDtypeStruct((bsz, t_c, LRU_WIDTH), F32), jax.ShapeDtypeStruct((bsz, t_l, LRU_WIDTH), F32)],
        scratch_shapes=[pltpu.VMEM((t_pad, LRU_BLOCK), F32), pltpu.VMEM((t_c + t_l, LRU_BLOCK), F32)],
        compiler_params=pltpu.CompilerParams(
            dimension_semantics=("parallel", "parallel"), vmem_limit_bytes=V7X_VMEM_LIMIT_BYTES),
        name="rglru",
    )(u_c, u_c, u_l, u_l, conv_w, conv_b[None], wa, wx, ba, bx, lam)


HG_GROUP = 8


def _hgrn2_kernel(seq, qc_ref, gc_ref, ffc_ref, fbc_ref, vc_ref, ql_ref, gl_ref, ffl_ref, fbl_ref, vl_ref,
                  lbf_ref, lbb_ref, gain_ref, oc_ref, ol_ref, of_ref):
    t_c, t_l = seq
    L = HG_CHUNK
    R = HG_GROUP * L
    n_gc = t_c // L
    r_i = lax.broadcasted_iota(jnp.int32, (R, R), 0)
    c_i = lax.broadcasted_iota(jnp.int32, (R, R), 1)
    same = (r_i // L) == (c_i // L)
    causal = (same & (c_i <= r_i), same & (c_i >= r_i))
    tri = tuple(jnp.where(m, 1.0, 0.0).astype(BF16) for m in causal)
    lb = (lbf_ref[...], lbb_ref[...])
    gain = gain_ref[...]

    def group(d, n, q_raw, f_raw, v, st):
        rows = n * L
        f = lb[d] + (1.0 - lb[d]) * jax.nn.sigmoid(f_raw)
        logf = jnp.log(f)
        k = 1.0 - f
        hi = logf.astype(BF16)
        r1 = logf - hi.astype(F32)
        mid = r1.astype(BF16)
        lo = (r1 - mid.astype(F32)).astype(BF16)
        g3 = jnp.dot(tri[d][:rows, :rows], jnp.concatenate([hi, mid, lo], axis=1), preferred_element_type=F32)
        g = g3[:, :HG_DK] + g3[:, HG_DK:2 * HG_DK] + g3[:, 2 * HG_DK:]
        g4 = g.reshape(n, L, HG_DK)
        g_last = g4[:, 0:1] if d else g4[:, L - 1:L]
        g_rest = (jnp.broadcast_to(g_last, (n, L, HG_DK)) - g4).reshape(rows, HG_DK)
        qg = (jax.nn.silu(q_raw) * jnp.exp(g)).astype(BF16)
        kg = (k * jnp.exp(-g)).astype(BF16)
        kd = (k * jnp.exp(g_rest)).astype(BF16)
        vb = v.astype(BF16)
        att = lax.dot_general(qg, kg, (((1,), (1,)), ((), ())), preferred_element_type=F32)
        att = jnp.where(causal[d][:rows, :rows], att, 0.0).astype(BF16)
        o = jnp.dot(att, vb, preferred_element_type=F32)
        dec = jnp.exp(g_last)
        sl = [slice(j * L, (j + 1) * L) for j in range(n)]
        ds_t = [lax.dot_general(vb[s], kd[s], (((0,), (0,)), ((), ())), preferred_element_type=F32) for s in sl]
        entering = [None] * n
        for j in (range(n - 1, -1, -1) if d else range(n)):
            entering[j] = st.astype(BF16)
            st = dec[j] * st + ds_t[j]
        outs = [o[s] + lax.dot_general(qg[s], e, (((1,), (1,)), ((), ())), preferred_element_type=F32)
                for s, e in zip(sl, entering)]
        return jnp.concatenate(outs, axis=0), st

    def finish(o, gate):
        return _rms(o, gain) * jax.nn.silu(gate)

    def lat_rows(i):
        return pl.ds(pl.multiple_of(i * R, R), R), pl.ds(pl.multiple_of(t_c + i * R, math.gcd(t_c, R)), R)

    st = jnp.zeros((HG_DV, HG_DK), F32)
    o, st = group(0, n_gc, qc_ref[0], ffc_ref[0], vc_ref[0], st)
    of_ref[0:t_c, :] = o

    def fwd_body(i, st):
        rows, srows = lat_rows(i)
        o, st = group(0, HG_GROUP, ql_ref[0, rows, :], ffl_ref[0, rows, :], vl_ref[0, rows, :], st)
        of_ref[srows, :] = o
        return st

    lax.fori_loop(0, t_l // R, fwd_body, st, unroll=2)

    st = jnp.zeros((HG_DV, HG_DK), F32)
    o, st = group(1, n_gc, qc_ref[0], fbc_ref[0], vc_ref[0], st)
    oc_ref[0] = finish(of_ref[0:t_c, :] + o, gc_ref[0])

    def bwd_body(i, st):
        rows, srows = lat_rows(t_l // R - 1 - i)
        o, st = group(1, HG_GROUP, ql_ref[0, rows, :], fbl_ref[0, rows, :], vl_ref[0, rows, :], st)
        ol_ref[0, rows, :] = finish(of_ref[srows, :] + o, gl_ref[0, rows, :])
        return st

    lax.fori_loop(0, t_l // R, bwd_body, st, unroll=2)


def _hgrn2(u_c, u_l, lb_f, lb_b, onorm):
    bsz, t_c, _ = u_c.shape
    t_l = u_l.shape[1]
    nb = HG_KW // HG_DK
    col = lambda t, blk: pl.BlockSpec((1, t, HG_DK), lambda b, h: (b, 0, blk + h))
    seq_cols = lambda t: [col(t, 0), col(t, nb), col(t, EV_OUT // HG_DK), col(t, EV_OUT // HG_DK + nb), col(t, EV_OUT // HG_DK + 2 * nb)]
    vec = lambda: pl.BlockSpec((1, HG_DK), lambda b, h: (0, h))
    return pl.pallas_call(
        functools.partial(_hgrn2_kernel, (t_c, t_l)),
        grid=(bsz, HG_HEADS),
        in_specs=seq_cols(t_c) + seq_cols(t_l) + [vec(), vec(), vec()],
        out_specs=[
            pl.BlockSpec((1, t_c, HG_DV), lambda b, h: (b, 0, h)),
            pl.BlockSpec((1, t_l, HG_DV), lambda b, h: (b, 0, h)),
        ],
        out_shape=[jax.ShapeDtypeStruct((bsz, t_c, HG_VW), F32), jax.ShapeDtypeStruct((bsz, t_l, HG_VW), F32)],
        scratch_shapes=[pltpu.VMEM((t_c + t_l, HG_DV), F32)],
        compiler_params=pltpu.CompilerParams(
            dimension_semantics=("parallel", "parallel"), vmem_limit_bytes=V7X_VMEM_LIMIT_BYTES),
        name="hgrn2",
    )(*([u_c] * 5), *([u_l] * 5), lb_f[None], lb_b[None], onorm[None])


LANES = 128
MLA_HW = 2 * LANES
MLA_TM = 512
MLA_TQ = 1024
MLA_TK = 512
MLA_UNROLL = 2
MLA_VW = MLA_V + LANES
MLA_SCALE_LOG2E = (MLA_NOPE + MLA_ROPE) ** -0.5 * math.log2(math.e)


def _rms(x, gain):
    return x * lax.rsqrt(jnp.mean(x * x, axis=-1, keepdims=True) + EPS) * gain


def _rope_tile(r, cos_t, sin_t):
    return r * cos_t + (pltpu.roll(r, MLA_ROPE // 2, 1) + pltpu.roll(r, LANES - MLA_ROPE // 2, 1)) * sin_t


def _mla_q_kernel(u_ref, gain_ref, w_ref, cos_ref, sin_ref, q_ref):
    q = jnp.dot(_rms(u_ref[0], gain_ref[...]).astype(BF16), w_ref[...], preferred_element_type=F32)
    for h in range(MLA_HEADS):
        lo = h * MLA_HW
        q_ref[0, :, lo:lo + LANES] = q[:, lo:lo + LANES].astype(BF16)
        q_ref[0, :, lo + LANES:lo + MLA_HW] = _rope_tile(q[:, lo + LANES:lo + MLA_HW], cos_ref[...], sin_ref[...]).astype(BF16)


def _mla_kv_kernel(u_ref, kr_ref, gain_ref, w_ref, cos_ref, sin_ref, k_ref, v_ref):
    kv = jnp.dot(_rms(u_ref[0], gain_ref[...]).astype(BF16), w_ref[...], preferred_element_type=F32)
    kr = _rope_tile(kr_ref[0], cos_ref[...], sin_ref[...]).astype(BF16)
    for h in range(MLA_HEADS):
        k_ref[0, :, h * MLA_HW:h * MLA_HW + LANES] = kv[:, h * LANES:(h + 1) * LANES].astype(BF16)
        k_ref[0, :, h * MLA_HW + LANES:(h + 1) * MLA_HW] = kr
    lane = lax.broadcasted_iota(jnp.int32, (kv.shape[0], LANES), 1)
    one_hot = jnp.where(lane == 0, 1.0, 0.0).astype(BF16)
    for h in range(MLA_HEADS):
        v_ref[0, :, h * MLA_VW:h * MLA_VW + MLA_V] = kv[:, (MLA_HEADS + h) * LANES:(MLA_HEADS + h + 1) * LANES].astype(BF16)
        v_ref[0, :, h * MLA_VW + MLA_V:(h + 1) * MLA_VW] = one_hot


def _mla_q(u, gain, w, cos_t, sin_t):
    bsz, t, _ = u.shape
    tm = min(MLA_TM, t)
    return pl.pallas_call(
        _mla_q_kernel,
        grid=(bsz, t // tm),
        in_specs=[
            pl.BlockSpec((1, tm, MLA_Q_LORA), lambda b, i: (b, i, 0)),
            pl.BlockSpec((1, MLA_Q_LORA), lambda b, i: (0, 0)),
            pl.BlockSpec((MLA_Q_LORA, MLA_HEADS * MLA_HW), lambda b, i: (0, 0)),
            pl.BlockSpec((tm, LANES), lambda b, i: (i, 0)),
            pl.BlockSpec((tm, LANES), lambda b, i: (i, 0)),
        ],
        out_specs=pl.BlockSpec((1, tm, MLA_HEADS * MLA_HW), lambda b, i: (b, i, 0)),
        out_shape=jax.ShapeDtypeStruct((bsz, t, MLA_HEADS * MLA_HW), BF16),
        compiler_params=pltpu.CompilerParams(
            dimension_semantics=("parallel", "parallel"), vmem_limit_bytes=V7X_VMEM_LIMIT_BYTES),
        name="mla_q_up",
    )(u, gain[None], w, cos_t, sin_t)


def _mla_kv(u, c_blk, r_blk, gain, w, cos_t, sin_t):
    bsz, t, _ = u.shape
    tm = min(MLA_TM, t)
    return pl.pallas_call(
        _mla_kv_kernel,
        grid=(bsz, t // tm),
        in_specs=[
            pl.BlockSpec((1, tm, MLA_KV_LORA), lambda b, i: (b, i, c_blk)),
            pl.BlockSpec((1, tm, LANES), lambda b, i: (b, i, r_blk)),
            pl.BlockSpec((1, MLA_KV_LORA), lambda b, i: (0, 0)),
            pl.BlockSpec((MLA_KV_LORA, MLA_HEADS * (LANES + MLA_V)), lambda b, i: (0, 0)),
            pl.BlockSpec((tm, LANES), lambda b, i: (i, 0)),
            pl.BlockSpec((tm, LANES), lambda b, i: (i, 0)),
        ],
        out_specs=[
            pl.BlockSpec((1, tm, MLA_HEADS * MLA_HW), lambda b, i: (b, i, 0)),
            pl.BlockSpec((1, tm, MLA_HEADS * MLA_VW), lambda b, i: (b, i, 0)),
        ],
        out_shape=[jax.ShapeDtypeStruct((bsz, t, MLA_HEADS * MLA_HW), BF16),
                   jax.ShapeDtypeStruct((bsz, t, MLA_HEADS * MLA_VW), BF16)],
        compiler_params=pltpu.CompilerParams(
            dimension_semantics=("parallel", "parallel"), vmem_limit_bytes=V7X_VMEM_LIMIT_BYTES),
        name="mla_kv_up",
    )(u, u, gain[None], w, cos_t, sin_t)


def _flash_kernel(t_l, q_ref, kc_ref, vc_ref, kl_ref, vl_ref, o_ref):
    q = q_ref[0]

    tiles = [(kc_ref, vc_ref, slice(None))] + [(kl_ref, vl_ref, slice(i * MLA_TK, (i + 1) * MLA_TK)) for i in range(t_l // MLA_TK)]

    def scores(tile):
        k_ref, _, rows = tile
        return lax.dot_general(q, k_ref[0, rows, :], (((1,), (1,)), ((), ())), preferred_element_type=F32)

    tq = q.shape[0]
    m = jnp.full((tq, 1), -jnp.inf, F32)
    acc = jnp.zeros((tq, MLA_VW), F32)
    s_next = scores(tiles[0])
    for i, (_, v_ref, rows) in enumerate(tiles):
        s = s_next * MLA_SCALE_LOG2E
        if i + 1 < len(tiles):
            s_next = scores(tiles[i + 1])
        m_new = jnp.maximum(m, jnp.max(s, axis=-1, keepdims=True))
        p = jnp.exp2(s - m_new)
        acc = jnp.exp2(m - m_new) * acc + jnp.dot(p.astype(BF16), v_ref[0, rows, :], preferred_element_type=F32)
        m = m_new
    o_ref[0] = acc[:, :MLA_V] / acc[:, MLA_V:MLA_V + 1]


def _flash(q, k_c, v_c, k_l, v_l):
    bsz, t, _ = q.shape
    t_c, t_l = k_c.shape[1], k_l.shape[1]
    return pl.pallas_call(
        functools.partial(_flash_kernel, t_l),
        grid=(bsz, MLA_HEADS, t // MLA_TQ),
        in_specs=[
            pl.BlockSpec((1, MLA_TQ, MLA_HW), lambda b, h, i: (b, i, h)),
            pl.BlockSpec((1, t_c, MLA_HW), lambda b, h, i: (b, 0, h)),
            pl.BlockSpec((1, t_c, MLA_VW), lambda b, h, i: (b, 0, h)),
            pl.BlockSpec((1, t_l, MLA_HW), lambda b, h, i: (b, 0, h)),
            pl.BlockSpec((1, t_l, MLA_VW), lambda b, h, i: (b, 0, h)),
        ],
        out_specs=pl.BlockSpec((1, MLA_TQ, MLA_V), lambda b, h, i: (b, i, h)),
        out_shape=jax.ShapeDtypeStruct((bsz, t, MLA_HEADS * MLA_V), F32),
        compiler_params=pltpu.CompilerParams(
            dimension_semantics=("parallel", "parallel", "arbitrary"), vmem_limit_bytes=V7X_VMEM_LIMIT_BYTES),
        name="mla_flash",
    )(q, k_c, v_c, k_l, v_l)


def _mla(u_c, u_l, q_norm, w_q_up, kv_norm, w_kv_up, cos, sin):
    t_c, t_l = u_c.shape[1], u_l.shape[1]
    zeros = jnp.zeros((t_l, LANES - MLA_ROPE), F32)
    cos_l = jnp.concatenate([cos, cos, zeros], axis=-1)
    sin_l = jnp.concatenate([-sin, sin, zeros], axis=-1)
    cos_c = jnp.concatenate([jnp.ones((t_c, MLA_ROPE), F32), zeros[:t_c]], axis=-1)
    sin_c = jnp.zeros((t_c, LANES), F32)
    wq = w_q_up.reshape(MLA_Q_LORA, MLA_HEADS, MLA_NOPE + MLA_ROPE)
    wq = jnp.pad(wq, ((0, 0), (0, 0), (0, MLA_HW - MLA_NOPE - MLA_ROPE))).reshape(MLA_Q_LORA, -1).astype(BF16)
    wkv = w_kv_up.reshape(MLA_KV_LORA, MLA_HEADS, MLA_NOPE + MLA_V)
    wkv = jnp.concatenate([wkv[..., :MLA_NOPE].reshape(MLA_KV_LORA, -1), wkv[..., MLA_NOPE:].reshape(MLA_KV_LORA, -1)], axis=-1).astype(BF16)
    q = _mla_q(u_l, q_norm, wq, cos_l, sin_l)
    k_l, v_l = _mla_kv(u_l, OD_OUT // MLA_KV_LORA, (OD_OUT + MLA_KV_LORA) // LANES, kv_norm, wkv, cos_l, sin_l)
    k_c, v_c = _mla_kv(u_c, 0, MLA_KV_LORA // LANES, kv_norm, wkv, cos_c, sin_c)
    return _flash(q, k_c, v_c, k_l, v_l)


OD_BLK_Z = MLA_Q_LORA // LANES
OD_BLK_CKV = OD_OUT // LANES
OD_BLK_KR = OD_BLK_CKV + MLA_KV_LORA // LANES
OD_BLK_XBC = OD_BLK_KR + 1
SSD_XBC_BLKS = SSD_XBC // LANES
SSD_ACT_BLKS = SSD_XBC_BLKS + SSD_GROUPS
SSD_GH = SSD_HEADS // SSD_GROUPS
SSD_GW = SSD_WIDTH // SSD_GROUPS
SSD_TILE = 2 * SSD_CHUNK
SSD_CONV_ROWS = 512


def _od_weight_layout(w_in, ctx_only):
    k = w_in.shape[0]
    z = lambda n: jnp.zeros((k, n), w_in.dtype)
    o = OD_OUT + MLA_KV_LORA
    xbc0 = o + MLA_ROPE
    dt0 = xbc0 + SSD_XBC
    cols = [] if ctx_only else [w_in[:, :OD_OUT]]
    cols += [w_in[:, OD_OUT:o], w_in[:, o:xbc0], z(LANES - MLA_ROPE), w_in[:, xbc0:dt0]]
    for g in range(SSD_GROUPS):
        cols += [w_in[:, dt0 + g * SSD_GH:dt0 + (g + 1) * SSD_GH],
                 w_in[:, dt0 + SSD_HEADS + g * SSD_GH:dt0 + SSD_HEADS + (g + 1) * SSD_GH], z(LANES - 2 * SSD_GH)]
    w = jnp.concatenate(cols, axis=1)
    return jnp.pad(w, ((0, 0), (0, -w.shape[1] % (10 * LANES) if ctx_only else -w.shape[1] % (8 * LANES))))


def _ssd_group_lanes(v):
    rows = [jnp.concatenate([v[0, g * SSD_GH:(g + 1) * SSD_GH], v[1, g * SSD_GH:(g + 1) * SSD_GH],
                             jnp.zeros((LANES - 2 * SSD_GH,), v.dtype)]) for g in range(SSD_GROUPS)]
    return jnp.stack(rows)


def _ssd_act_kernel(seq, xc_ref, xl_ref, cw_ref, cb_ref, o_ref, xs_ref):
    t_c, t_l = seq
    j = pl.program_id(1)

    @pl.when(j < SSD_XBC_BLKS)
    def _():
        off_c, off_l = LRU_PAD, 2 * LRU_PAD + t_c
        zeros = jnp.zeros((LRU_PAD, LANES), F32)
        xs_ref[0:LRU_PAD] = zeros
        xs_ref[off_c + t_c:off_l] = zeros
        xs_ref[off_l + t_l:off_l + t_l + LRU_PAD] = zeros
        xs_ref[off_c:off_c + t_c] = xc_ref[0]
        xs_ref[off_l:off_l + t_l] = xl_ref[0]
        cw = cw_ref[...]
        cb = cb_ref[...]

        def conv(r0, rows):
            win = xs_ref[pl.ds(r0 - LRU_PAD, rows + 2 * LRU_PAD), :]
            y = cb
            for k in range(4):
                y = y + cw[k:k + 1] * win[LRU_PAD - 2 + k:LRU_PAD - 2 + k + rows]
            return jax.nn.silu(y)

        o_ref[0, 0:t_c, :] = conv(off_c, t_c)

        def body(i, _):
            r0 = pl.multiple_of(i * SSD_CONV_ROWS, SSD_CONV_ROWS)
            o_ref[0, pl.ds(t_c + r0, SSD_CONV_ROWS), :] = conv(off_l + r0, SSD_CONV_ROWS)
            return 0

        lax.fori_loop(0, t_l // SSD_CONV_ROWS, body, 0)

    @pl.when(j >= SSD_XBC_BLKS)
    def _():
        o_ref[0, 0:t_c, :] = jax.nn.softplus(xc_ref[0] + cb_ref[...])
        o_ref[0, t_c:t_c + t_l, :] = jax.nn.softplus(xl_ref[0] + cb_ref[...])


def _ssd_act(u_c, u_l, conv_w, conv_b, dt_bias):
    bsz, t_c, _ = u_c.shape
    t_l = u_l.shape[1]
    cw = jnp.pad(conv_w, ((0, 0), (0, SSD_GROUPS * LANES)))
    cb = jnp.concatenate([conv_b, _ssd_group_lanes(dt_bias).reshape(-1)])[None]
    ctx_blk = OD_BLK_XBC - OD_BLK_CKV
    return pl.pallas_call(
        functools.partial(_ssd_act_kernel, (t_c, t_l)),
        grid=(bsz, SSD_ACT_BLKS),
        in_specs=[
            pl.BlockSpec((1, t_c, LANES), lambda b, j: (b, 0, ctx_blk + j)),
            pl.BlockSpec((1, t_l, LANES), lambda b, j: (b, 0, OD_BLK_XBC + j)),
            pl.BlockSpec((4, LANES), lambda b, j: (0, j)),
            pl.BlockSpec((1, LANES), lambda b, j: (0, j)),
        ],
        out_specs=pl.BlockSpec((1, t_c + t_l, LANES), lambda b, j: (b, 0, j)),
        out_shape=jax.ShapeDtypeStruct((bsz, t_c + t_l, SSD_ACT_BLKS * LANES), F32),
        scratch_shapes=[pltpu.VMEM((t_c + t_l + 3 * LRU_PAD, LANES), F32)],
        compiler_params=pltpu.CompilerParams(
            dimension_semantics=("parallel", "parallel"), vmem_limit_bytes=V7X_VMEM_LIMIT_BYTES),
        name="ssd_act",
    )(u_c, u_l, cw, cb)


def _split3(x):
    hi = x.astype(BF16)
    r1 = x - hi.astype(F32)
    mid = r1.astype(BF16)
    return hi, mid, (r1 - mid.astype(F32)).astype(BF16)


def _ssd_scan_kernel(d, n_tiles, *refs):
    if d == 0:
        x_ref, b_ref, c_ref, dt_ref, alog_ref, y_ref, s_ref = refs
    else:
        x_ref, b_ref, c_ref, dt_ref, alog_ref, yf_ref, z_ref, dskip_ref, gain_ref, y_ref, s_ref = refs
    L = SSD_CHUNK
    P = SSD_HEADDIM

    @pl.when(pl.program_id(2) == 0)
    def _():
        s_ref[...] = jnp.zeros_like(s_ref)

    r_i = lax.broadcasted_iota(jnp.int32, (L, L), 0)
    c_i = lax.broadcasted_iota(jnp.int32, (L, L), 1)
    causal = (c_i >= r_i) if d else (c_i <= r_i)
    tri = jnp.where(causal, 1.0, 0.0).astype(BF16)
    low_half = lax.broadcasted_iota(jnp.int32, (L, LANES), 1) < P
    a_lane = -jnp.exp(alog_ref[...])

    def pair_cols(cols, p):
        return jnp.where(low_half, cols[2 * p], cols[2 * p + 1])

    for ch in ((1, 0) if d else (0, 1)):
        rows = slice(ch * L, (ch + 1) * L)
        xs = x_ref[0, rows, :]
        bm = b_ref[0, rows, :].astype(BF16)
        cm = c_ref[0, rows, :].astype(BF16)
        dt = dt_ref[0, rows, :]
        a3 = _split3(dt * a_lane)
        cs3 = jnp.dot(tri, jnp.concatenate(a3, axis=1), preferred_element_type=F32)
        cs_col = cs3[:, :LANES] + cs3[:, LANES:2 * LANES] + cs3[:, 2 * LANES:]
        cs_row = sum(lax.dot_general(t, tri, (((0,), (1,)), ((), ())), preferred_element_type=F32) for t in a3)
        cb = lax.dot_general(cm, bm, (((1,), (1,)), ((), ())), preferred_element_type=F32)
        s_in = s_ref[...].astype(BF16)
        y_off = lax.dot_general(cm, s_in, (((1,), (1,)), ((), ())), preferred_element_type=F32)
        last = 0 if d else L - 1
        cols, dts, lasts = [], [], []
        for i in range(SSD_GH):
            lane = SSD_GH * d + i
            cols.append(cs_col[:, lane:lane + 1])
            dts.append(dt[:, lane:lane + 1])
            lasts.append(cs_col[last:last + 1, lane:lane + 1])
        ys, xdecs = [], []
        for p in range(SSD_GH // 2):
            xp = xs[:, p * LANES:(p + 1) * LANES]
            col2 = pair_cols(cols, p)
            xdt = xp * pair_cols(dts, p)
            xdt_b = xdt.astype(BF16)
            y2 = []
            for e in range(2):
                i = 2 * p + e
                lane = SSD_GH * d + i
                decay = jnp.where(causal, jnp.exp(cols[i] - cs_row[lane:lane + 1, :]), 0.0)
                y2.append(jnp.dot((cb * decay).astype(BF16), xdt_b, preferred_element_type=F32))
            ys.append(jnp.where(low_half, y2[0], y2[1]) + y_off[:, p * LANES:(p + 1) * LANES] * jnp.exp(col2))
            last2 = jnp.where(low_half, lasts[2 * p], lasts[2 * p + 1])
            xdecs.append((xdt * jnp.exp(last2 - col2)).astype(BF16))
        y = jnp.concatenate(ys, axis=1)
        ds = lax.dot_general(jnp.concatenate(xdecs, axis=1), bm, (((0,), (0,)), ((), ())), preferred_element_type=F32)
        dec = jnp.concatenate([jnp.broadcast_to(jnp.exp(lasts[i]), (P, SSD_STATE)) for i in range(SSD_GH)], axis=0)
        s_ref[...] = dec * s_ref[...] + ds
        if d == 0:
            y_ref[0, rows, :] = y
        else:
            out = (yf_ref[0, rows, :] + y + dskip_ref[...] * xs) * jax.nn.silu(z_ref[0, rows, :])
            y_ref[0, rows, :] = _rms(out, gain_ref[...])


def _ssd_scan(d, act, t_c, a_lanes, extra=()):
    bsz, t, _ = act.shape
    n_tiles = t // SSD_TILE
    n_ctx = t_c // SSD_TILE
    assert n_ctx == 1
    if d == 0:
        tile = lambda i: i
    else:
        tile = lambda i: jnp.where(i < n_ctx, n_ctx - 1 - i, n_tiles + n_ctx - 1 - i)
    in_specs = [
        pl.BlockSpec((1, SSD_TILE, SSD_GW), lambda b, g, i: (b, tile(i), g)),
        pl.BlockSpec((1, SSD_TILE, LANES), lambda b, g, i: (b, tile(i), SSD_WIDTH // LANES + g)),
        pl.BlockSpec((1, SSD_TILE, LANES), lambda b, g, i: (b, tile(i), (SSD_WIDTH + SSD_BC) // LANES + g)),
        pl.BlockSpec((1, SSD_TILE, LANES), lambda b, g, i: (b, tile(i), SSD_XBC_BLKS + g)),
        pl.BlockSpec((1, LANES), lambda b, g, i: (0, g)),
    ]
    args = [act, act, act, act, a_lanes.reshape(1, -1)]
    if d:
        y_f, u_l, d_skip, onorm = extra
        in_specs += [
            pl.BlockSpec((1, SSD_TILE, SSD_GW), lambda b, g, i: (b, tile(i), g)),
            pl.BlockSpec((1, SSD_TILE, SSD_GW), lambda b, g, i: (b, jnp.maximum(tile(i) - n_ctx, 0), OD_BLK_Z * LANES // SSD_GW + g)),
            pl.BlockSpec((1, SSD_GW), lambda b, g, i: (0, g)),
            pl.BlockSpec((1, SSD_GW), lambda b, g, i: (0, g)),
        ]
        args += [y_f, u_l, jnp.repeat(d_skip, SSD_HEADDIM)[None], onorm[None]]
        out_tile = lambda i: jnp.where(i < n_ctx, n_tiles - n_ctx - 1, tile(i) - n_ctx)
        out_rows = t - t_c
    else:
        out_tile, out_rows = tile, t
    return pl.pallas_call(
        functools.partial(_ssd_scan_kernel, d, n_tiles),
        grid=(bsz, SSD_GROUPS, n_tiles),
        in_specs=in_specs,
        out_specs=pl.BlockSpec((1, SSD_TILE, SSD_GW), lambda b, g, i: (b, out_tile(i), g)),
        out_shape=jax.ShapeDtypeStruct((bsz, out_rows, SSD_WIDTH), F32),
        scratch_shapes=[pltpu.VMEM((SSD_GW, SSD_STATE), F32)],
        compiler_params=pltpu.CompilerParams(
            dimension_semantics=("parallel", "parallel", "arbitrary"), vmem_limit_bytes=V7X_VMEM_LIMIT_BYTES),
        name="ssd_scan_bwd" if d else "ssd_scan_fwd",
    )(*args)


def _ssd(u_c, u_l, conv_w, conv_b, dt_bias, a_log, d_skip, onorm):
    act = _ssd_act(u_c, u_l, conv_w, conv_b, dt_bias)
    a_lanes = _ssd_group_lanes(a_log)
    y_f = _ssd_scan(0, act, u_c.shape[1], a_lanes)
    return _ssd_scan(1, act, u_c.shape[1], a_lanes, (y_f, u_l, d_skip, onorm))


def rmsnorm(x, g):
    xf = x.astype(F32)
    y = xf * lax.rsqrt(jnp.mean(xf * xf, axis=-1, keepdims=True) + EPS)
    return (y * g.astype(F32)).astype(x.dtype)


def heads(t, h):
    return t.reshape(t.shape[0], t.shape[1], h, t.shape[-1] // h)


def flip(t):
    return jnp.flip(t, axis=1)


def dwconv(x, w, b):
    k = w.shape[0]
    y = lax.conv_general_dilated(x, w[:, None, :].astype(x.dtype), window_strides=(1,), padding=[(k // 2, k - 1 - k // 2)], dimension_numbers=('NWC', 'WIO', 'NWC'), feature_group_count=x.shape[-1])
    return y + b.astype(x.dtype)


def axial_rope(rows):
    row = jnp.repeat(jnp.arange(rows, dtype=F32), GRID_W)
    col = jnp.arange(rows * GRID_W) % GRID_W
    n_freq = MLA_ROPE // 4
    inv = ROPE_THETA ** (-jnp.arange(n_freq, dtype=F32) / n_freq)
    ang = jnp.concatenate([row[:, None] * inv, col.astype(F32)[:, None] * inv], axis=-1)
    return jnp.cos(ang), jnp.sin(ang)


def apply_rope(t, cos, sin):
    t1, t2 = jnp.split(t.astype(F32), 2, axis=-1)
    cs, sn = cos[None, :, None, :], sin[None, :, None, :]
    return jnp.concatenate([t1 * cs - t2 * sn, t1 * sn + t2 * cs], axis=-1).astype(t.dtype)


def segsum(x):
    t = x.shape[-1]
    xx = jnp.broadcast_to(x[..., :, None], x.shape + (t,))
    xx = jnp.where(jnp.tril(jnp.ones((t, t), bool), -1), xx, 0.0)
    cs = jnp.cumsum(xx, axis=-2)
    return jnp.where(jnp.tril(jnp.ones((t, t), bool)), cs, -jnp.inf)


def gla_chunked(q, k, v, logf, s0):
    b, t, h, _ = k.shape
    dv = v.shape[-1]
    n = t // HG_CHUNK
    blk = lambda a: a.reshape(b, n, HG_CHUNK, h, a.shape[-1])
    k, v, g = blk(k), blk(v), jnp.cumsum(blk(logf), axis=2)
    g_last = g[:, :, -1:]
    ds = jnp.einsum('bnmhd,bnmhe->bnhde', k * jnp.exp(g_last - g), v)
    decay = jnp.exp(g_last[:, :, 0])

    def step(s, inp):
        dec, d = inp
        return dec[..., None] * s + d, s

    s_fin, s_prev = lax.scan(step, s0, (jnp.moveaxis(decay, 1, 0), jnp.moveaxis(ds, 1, 0)))
    if q is None:
        return None, s_fin
    qg = blk(q) * jnp.exp(g)
    att = jnp.einsum('bnlhd,bnmhd->bnhlm', qg, k * jnp.exp(-g))
    att = jnp.where(jnp.tril(jnp.ones((HG_CHUNK, HG_CHUNK), bool)), att, 0.0)
    o = jnp.einsum('bnhlm,bnmhe->bnlhe', att, v) + jnp.einsum('bnlhd,nbhde->bnlhe', qg, s_prev)
    return o.reshape(b, t, h, dv), s_fin


def hgrn2_scan(q_c, f_c, i_c, q_l, f_l, i_l, lb):
    def gates(fr):
        f = lb + (1.0 - lb) * jax.nn.sigmoid(fr.astype(F32))
        return heads(jnp.log(f), HG_HEADS), heads(1.0 - f, HG_HEADS)

    s0 = jnp.zeros((f_l.shape[0], HG_HEADS, HG_DK, HG_DV), F32)
    lf, k = gates(f_c)
    o_c, s_c = gla_chunked(q_c, k, i_c, lf, s0)
    lf, k = gates(f_l)
    o_l, _ = gla_chunked(q_l, k, i_l, lf, s_c)
    return o_c, o_l


def even_mixer(u_c, u_l, lb_f, lb_b, onorm, conv_w, conv_b, wa, ba, wx, bx, lam, need_ctx):
    assert need_ctx
    lru_c, lru_l = _rglru(u_c, u_l, conv_w, conv_b, wa, ba, wx, bx, lam)
    hg_c, hg_l = _hgrn2(u_c, u_l, lb_f, lb_b, onorm)
    return (hg_c, lru_c), (hg_l, lru_l)


def block_attention(q, k, v):
    b, t, h, dq = q.shape
    nb = t // ATTN_BLOCK
    scale = dq ** -0.5
    qb = jnp.moveaxis(q.reshape(b, nb, ATTN_BLOCK, h, dq), 1, 0)

    def one(qblk):
        s = jnp.einsum('bqhd,bkhd->bhqk', qblk, k).astype(F32) * scale
        p = jax.nn.softmax(s, axis=-1).astype(v.dtype)
        return jnp.einsum('bhqk,bkhe->bqhe', p, v)

    o = lax.map(one, qb)
    return jnp.moveaxis(o, 0, 1).reshape(b, t, h, v.shape[-1])


def ssd_chunked(x, dt, a, bm, cm, s0, want_y):
    b, t, h, p = x.shape
    n_c = t // SSD_CHUNK
    rep = h // bm.shape[2]
    blk = lambda z: z.reshape(b, n_c, SSD_CHUNK, *z.shape[2:])
    xdt = blk(x * dt[..., None])
    bh = blk(jnp.repeat(bm, rep, axis=2))
    a_dt = jnp.moveaxis(blk(dt * a), 3, 1)
    a_cs = jnp.cumsum(a_dt, axis=-1)
    states = jnp.einsum('bclhn,bhcl,bclhp->bchpn', bh, jnp.exp(a_cs[..., -1:] - a_cs), xdt)
    states = jnp.concatenate([s0[:, None], states], axis=1)
    chunk_decay = jnp.exp(segsum(jnp.pad(a_cs[..., -1], ((0, 0), (0, 0), (1, 0)))))
    new_states = jnp.einsum('bhzc,bchpn->bzhpn', chunk_decay, states)
    if not want_y:
        return None, new_states[:, -1]
    ch = blk(jnp.repeat(cm, rep, axis=2))
    scores = jnp.einsum('bclhn,bcshn->bhcls', ch, bh) * jnp.exp(segsum(a_dt))
    y = jnp.einsum('bhcls,bcshp->bclhp', scores, xdt) + jnp.einsum('bclhn,bchpn,bhcl->bclhp', ch, new_states[:, :-1], jnp.exp(a_cs))
    return y.reshape(b, t, h, p), new_states[:, -1]


def odd_mixer(u_c, u_l, q_norm, w_q_up, kv_norm, w_kv_up, conv_w, conv_b, dt_bias, a_log, d_skip, onorm, cos, sin, need_ctx):
    assert not need_ctx
    att_l = _mla(u_c, u_l, q_norm, w_q_up, kv_norm, w_kv_up, cos, sin)
    ssd_l = _ssd(u_c, u_l, conv_w, conv_b, dt_bias, a_log, d_skip, onorm)
    return None, (att_l, ssd_l)


def ec_moe(x, gain, sc, sh, g2, w_router, w_gate, w_up, w_down, layer, merge_sets):
    b, t, d = x.shape
    cap = CAPACITY_FACTOR * t // N_EXPERTS
    h, aff = _router(x, gain, sc, sh, w_router)
    gate, idx = lax.top_k(aff, cap)
    bidx = jnp.arange(b)[:, None, None]
    xe = h[bidx, idx]
    if merge_sets:
        xe = jnp.swapaxes(xe, 0, 1).reshape(1, N_EXPERTS, b * cap, d)
        gate_m = jnp.swapaxes(gate, 0, 1).reshape(1, N_EXPERTS, b * cap)
        ye = _expert_ffn(xe, gate_m, jnp.repeat(g2, cap, axis=0)[None], w_gate, w_up, w_down, layer)
        ye = jnp.swapaxes(ye.reshape(N_EXPERTS, b, cap, d), 0, 1)
    else:
        ye = _expert_ffn(xe, gate, g2[:, None], w_gate, w_up, w_down, layer)
    return x.at[bidx, idx].add(ye)


def kernel(x, c, ctx, c_ctx, w_mod, b_mod, norm_mix, norm_ffn, w_out, ev_w_in, hg_lb, hg_onorm, lru_conv_w, lru_conv_b, lru_wa, lru_ba, lru_wx, lru_bx, lru_lambda, od_w_in, mla_q_norm, mla_w_q_up, mla_kv_norm, mla_w_kv_up, ssd_conv_w, ssd_conv_b, ssd_dt_bias, ssd_a_log, ssd_d, ssd_onorm, moe_router, moe_w_gate, moe_w_up, moe_w_down, final_norm):
    rows = x.shape[1] // GRID_W
    cos, sin = axial_rope(rows)
    lb_all = jnp.cumsum(jax.nn.softmax(hg_lb.astype(F32), axis=1), axis=1)
    bsz = x.shape[0]
    c_rows = jnp.concatenate([c, c_ctx[None], jnp.zeros((MOD_ROWS - bsz - 1, D_MODEL), F32)], axis=0)
    wg, wu, wd = moe_w_gate.astype(BF16), moe_w_up.astype(BF16), moe_w_down.astype(BF16)
    for l in range(DEPTH):
        need_ctx = l < DEPTH - 1
        j = l // 2
        mod_all = _modulation(c_rows, w_mod, b_mod, l)
        mod, mod_c = mod_all[:bsz], mod_all[bsz]
        sh1, sc1, g1, sh2, sc2, g2 = jnp.split(mod, 6, axis=-1)
        csh1, csc1, cg1, csh2, csc2, cg2 = (jnp.broadcast_to(v, (x.shape[0], D_MODEL)) for v in jnp.split(mod_c, 6, axis=-1))
        wo = w_out[l].astype(BF16)
        if l % 2 == 0:
            w_in = ev_w_in[j].astype(BF16)
            u_c = _proj_in(ctx, norm_mix[l], csc1, csh1, w_in)
            u_l = _proj_in(x, norm_mix[l], sc1, sh1, w_in)
            o_c, o_l = even_mixer(u_c, u_l, lb_all[0, j], lb_all[1, j], hg_onorm[j], lru_conv_w[j], lru_conv_b[j], lru_wa[j], lru_ba[j], lru_wx[j], lru_bx[j], lru_lambda[j], need_ctx)
        else:
            w_in = od_w_in[j]
            u_c = _proj_in(ctx, norm_mix[l], csc1, csh1, _od_weight_layout(w_in, True).astype(BF16))
            u_l = _proj_in(x, norm_mix[l], sc1, sh1, _od_weight_layout(w_in, False).astype(BF16))
            o_c, o_l = odd_mixer(u_c, u_l, mla_q_norm[j], mla_w_q_up[j], mla_kv_norm[j], mla_w_kv_up[j], ssd_conv_w[j], ssd_conv_b[j], ssd_dt_bias[j], ssd_a_log[j], ssd_d[j], ssd_onorm[j], cos, sin, need_ctx)
        x = _proj_out(*o_l, wo, x, g1)
        x = ec_moe(x, norm_ffn[l], sc2, sh2, g2, moe_router[l], wg, wu, wd, l, False)
        if need_ctx:
            ctx = _proj_out(*o_c, wo, ctx, cg1)
            ctx = ec_moe(ctx, norm_ffn[l], csc2, csh2, cg2, moe_router[l], wg, wu, wd, l, True)
    return _final_norm(x, final_norm)
```

```python
import functools
import math

import jax
import jax.numpy as jnp
from jax import lax
from jax.experimental import pallas as pl
from jax.experimental.pallas import tpu as pltpu

D_MODEL = 2048
DEPTH = 2
CTX_LEN = 256
GRID_W = 64
EPS = 1e-6
F32 = jnp.float32
BF16 = jnp.bfloat16

HG_HEADS = 8
HG_DK = 128
HG_DV = 128
HG_KW = HG_HEADS * HG_DK
HG_VW = HG_HEADS * HG_DV
HG_CHUNK = 64

LRU_WIDTH = 1024
LRU_HEADS = 8
LRU_C = 8.0

MLA_HEADS = 8
MLA_Q_LORA = 512
MLA_KV_LORA = 512
MLA_NOPE = 128
MLA_ROPE = 64
MLA_V = 128
ROPE_THETA = 10000.0

SSD_WIDTH = 1024
SSD_HEADDIM = 64
SSD_HEADS = SSD_WIDTH // SSD_HEADDIM
SSD_GROUPS = 2
SSD_STATE = 128
SSD_BC = SSD_GROUPS * SSD_STATE
SSD_XBC = SSD_WIDTH + 2 * SSD_BC
SSD_CHUNK = 128

N_EXPERTS = 16
CAPACITY_FACTOR = 2

EV_OUT = HG_KW + HG_VW + LRU_WIDTH
OD_OUT = MLA_Q_LORA + SSD_WIDTH

V7X_VMEM_LIMIT_BYTES = 56 * 1024 * 1024


def _lane_tile(n, target=1536):
    k = n // 128
    return 128 * max(d for d in range(1, k + 1) if k % d == 0 and 128 * d <= target)


PROJ_TM = 1024


def _proj_in_kernel(x_ref, gain_ref, sc_ref, sh_ref, w_ref, o_ref, h_ref):
    @pl.when(pl.program_id(2) == 0)
    def _():
        x = x_ref[0]
        y = x * lax.rsqrt(jnp.mean(x * x, axis=-1, keepdims=True) + EPS) * gain_ref[...]
        h_ref[...] = (y * (1.0 + sc_ref[0]) + sh_ref[0]).astype(BF16)

    o_ref[0] = jnp.dot(h_ref[...], w_ref[...], preferred_element_type=F32)


def _proj_in(x, gain, sc, sh, w):
    bsz, t, dm = x.shape
    n = w.shape[1]
    tm, tn = min(PROJ_TM, t), _lane_tile(n)
    return pl.pallas_call(
        _proj_in_kernel,
        grid=(bsz, t // tm, n // tn),
        in_specs=[
            pl.BlockSpec((1, tm, dm), lambda b, i, j: (b, i, 0)),
            pl.BlockSpec((1, dm), lambda b, i, j: (0, 0)),
            pl.BlockSpec((1, 1, dm), lambda b, i, j: (b, 0, 0)),
            pl.BlockSpec((1, 1, dm), lambda b, i, j: (b, 0, 0)),
            pl.BlockSpec((dm, tn), lambda b, i, j: (0, j)),
        ],
        out_specs=pl.BlockSpec((1, tm, tn), lambda b, i, j: (b, i, j)),
        out_shape=jax.ShapeDtypeStruct((bsz, t, n), F32),
        scratch_shapes=[pltpu.VMEM((tm, dm), BF16)],
        compiler_params=pltpu.CompilerParams(
            dimension_semantics=("parallel", "parallel", "arbitrary"), vmem_limit_bytes=V7X_VMEM_LIMIT_BYTES),
        name="proj_in",
    )(x, gain[None], sc[:, None], sh[:, None], w)


def _proj_out_kernel(ya_ref, yb_ref, wa_ref, wb_ref, x_ref, g_ref, o_ref):
    acc = jnp.dot(ya_ref[0].astype(BF16), wa_ref[...], preferred_element_type=F32)
    acc = acc + jnp.dot(yb_ref[0].astype(BF16), wb_ref[...], preferred_element_type=F32)
    o_ref[0] = x_ref[0] + g_ref[0] * acc


def _proj_out(ya, yb, w, x, g):
    bsz, t, kh = ya.shape
    dm = w.shape[1]
    tm, tn = min(PROJ_TM, t), _lane_tile(dm)
    half = lambda: pl.BlockSpec((1, tm, kh), lambda b, i, j: (b, i, 0))
    return pl.pallas_call(
        _proj_out_kernel,
        grid=(bsz, t // tm, dm // tn),
        in_specs=[
            half(), half(),
            pl.BlockSpec((kh, tn), lambda b, i, j: (0, j)),
            pl.BlockSpec((kh, tn), lambda b, i, j: (1, j)),
            pl.BlockSpec((1, tm, tn), lambda b, i, j: (b, i, j)),
            pl.BlockSpec((1, 1, tn), lambda b, i, j: (b, 0, j)),
        ],
        out_specs=pl.BlockSpec((1, tm, tn), lambda b, i, j: (b, i, j)),
        out_shape=jax.ShapeDtypeStruct((bsz, t, dm), F32),
        compiler_params=pltpu.CompilerParams(
            dimension_semantics=("parallel", "parallel", "arbitrary"), vmem_limit_bytes=V7X_VMEM_LIMIT_BYTES),
        name="proj_out",
    )(ya, yb, w, w, x, g[:, None])


MOD_ROWS = 8
MOD_TN = 1024


def _mod_kernel(c_ref, w_ref, b_ref, o_ref):
    o_ref[...] = jnp.dot(jax.nn.silu(c_ref[...]).astype(BF16), w_ref[0].astype(BF16), preferred_element_type=F32) + b_ref[0]


def _modulation(c_rows, w, b, layer):
    _, dm, n = w.shape
    return pl.pallas_call(
        _mod_kernel,
        grid=(n // MOD_TN,),
        in_specs=[
            pl.BlockSpec((MOD_ROWS, dm), lambda j: (0, 0)),
            pl.BlockSpec((1, dm, MOD_TN), lambda j: (layer, 0, j)),
            pl.BlockSpec((1, 1, MOD_TN), lambda j: (layer, 0, j)),
        ],
        out_specs=pl.BlockSpec((MOD_ROWS, MOD_TN), lambda j: (0, j)),
        out_shape=jax.ShapeDtypeStruct((MOD_ROWS, n), F32),
        compiler_params=pltpu.CompilerParams(dimension_semantics=("parallel",), vmem_limit_bytes=V7X_VMEM_LIMIT_BYTES),
        name="modulation",
    )(c_rows, w, b[:, None])


def _final_norm_kernel(x_ref, g_ref, o_ref):
    x = x_ref[0]
    o_ref[0] = x * lax.rsqrt(jnp.mean(x * x, axis=-1, keepdims=True) + EPS) * g_ref[...]


def _final_norm(x, gain):
    bsz, t, dm = x.shape
    return pl.pallas_call(
        _final_norm_kernel,
        grid=(bsz, t // PROJ_TM),
        in_specs=[pl.BlockSpec((1, PROJ_TM, dm), lambda b, i: (b, i, 0)), pl.BlockSpec((1, dm), lambda b, i: (0, 0))],
        out_specs=pl.BlockSpec((1, PROJ_TM, dm), lambda b, i: (b, i, 0)),
        out_shape=jax.ShapeDtypeStruct(x.shape, F32),
        compiler_params=pltpu.CompilerParams(dimension_semantics=("parallel", "parallel"), vmem_limit_bytes=V7X_VMEM_LIMIT_BYTES),
        name="final_norm",
    )(x, gain[None])


def _router_kernel(x_ref, gain_ref, sc_ref, sh_ref, wr_ref, h_ref, aff_ref):
    x = x_ref[0]
    y = x * lax.rsqrt(jnp.mean(x * x, axis=-1, keepdims=True) + EPS) * gain_ref[...]
    h = (y * (1.0 + sc_ref[0]) + sh_ref[0]).astype(BF16)
    h_ref[0] = h
    logits = lax.dot_general(wr_ref[...], h, (((1,), (1,)), ((), ())), preferred_element_type=F32)
    e = jnp.exp(logits - jnp.max(logits, axis=0, keepdims=True))
    aff_ref[0] = e / jnp.sum(e, axis=0, keepdims=True)


def _router(x, gain, sc, sh, w_router):
    bsz, t, dm = x.shape
    tm = min(PROJ_TM, t)
    return pl.pallas_call(
        _router_kernel,
        grid=(bsz, t // tm),
        in_specs=[
            pl.BlockSpec((1, tm, dm), lambda b, i: (b, i, 0)),
            pl.BlockSpec((1, dm), lambda b, i: (0, 0)),
            pl.BlockSpec((1, 1, dm), lambda b, i: (b, 0, 0)),
            pl.BlockSpec((1, 1, dm), lambda b, i: (b, 0, 0)),
            pl.BlockSpec((N_EXPERTS, dm), lambda b, i: (0, 0)),
        ],
        out_specs=[
            pl.BlockSpec((1, tm, dm), lambda b, i: (b, i, 0)),
            pl.BlockSpec((1, N_EXPERTS, tm), lambda b, i: (b, 0, i)),
        ],
        out_shape=[jax.ShapeDtypeStruct((bsz, t, dm), BF16), jax.ShapeDtypeStruct((bsz, N_EXPERTS, t), F32)],
        compiler_params=pltpu.CompilerParams(dimension_semantics=("parallel", "parallel"), vmem_limit_bytes=V7X_VMEM_LIMIT_BYTES),
        name="moe_router",
    )(x, gain[None], sc[:, None], sh[:, None], w_router.T.astype(BF16))


def _ffn_kernel(x_ref, gate_ref, g2_ref, wg_ref, wu_ref, wd_ref, o_ref):
    x = x_ref[0, 0]
    a = jnp.dot(x, wg_ref[0, 0], preferred_element_type=F32)
    u = jnp.dot(x, wu_ref[0, 0], preferred_element_type=F32)
    hid = (jax.nn.silu(a) * u).astype(BF16)
    o_ref[0, 0] = jnp.dot(hid, wd_ref[0, 0], preferred_element_type=F32) * gate_ref[0, 0] * g2_ref[0, 0]


def _expert_ffn(xe, gate, g2, w_gate, w_up, w_down, layer):
    s, e, c, dm = xe.shape
    ff = w_gate.shape[-1]
    g2_rows = g2.shape[1]
    return pl.pallas_call(
        _ffn_kernel,
        grid=(e, s),
        in_specs=[
            pl.BlockSpec((1, 1, c, dm), lambda ei, si: (si, ei, 0, 0)),
            pl.BlockSpec((1, 1, c, 1), lambda ei, si: (si, ei, 0, 0)),
            pl.BlockSpec((1, 1, g2_rows, dm), lambda ei, si: (si, 0, 0, 0)),
            pl.BlockSpec((1, 1, dm, ff), lambda ei, si: (layer, ei, 0, 0)),
            pl.BlockSpec((1, 1, dm, ff), lambda ei, si: (layer, ei, 0, 0)),
            pl.BlockSpec((1, 1, ff, dm), lambda ei, si: (layer, ei, 0, 0)),
        ],
        out_specs=pl.BlockSpec((1, 1, c, dm), lambda ei, si: (si, ei, 0, 0)),
        out_shape=jax.ShapeDtypeStruct((s, e, c, dm), F32),
        compiler_params=pltpu.CompilerParams(dimension_semantics=("parallel", "arbitrary"), vmem_limit_bytes=V7X_VMEM_LIMIT_BYTES),
        name="moe_ffn",
    )(xe, gate[..., None], g2[:, None], w_gate, w_up, w_down)


LRU_BLOCK = LRU_WIDTH // LRU_HEADS
LRU_L = 128
SUBLANES = 8
LRU_PAD = SUBLANES
LRU_UNROLL = 4


def _lru_kernel(seq, xc_ref, gc_ref, xl_ref, gl_ref, cw_ref, cb_ref, wa_ref, wx_ref, ba_ref, bx_ref, lam_ref,
                oc_ref, ol_ref, xs_ref, hf_ref):
    t_c, t_l = seq
    n_c, n_l = t_c // LRU_L, t_l // LRU_L
    off_c, off_l = LRU_PAD, 2 * LRU_PAD + t_c
    zeros = jnp.zeros((LRU_PAD, LRU_BLOCK), F32)
    xs_ref[0:LRU_PAD] = zeros
    xs_ref[off_c + t_c:off_l] = zeros
    xs_ref[off_l + t_l:off_l + t_l + LRU_PAD] = zeros
    xs_ref[off_c:off_c + t_c] = xc_ref[0]
    xs_ref[off_l:off_l + t_l] = xl_ref[0]

    cw = cw_ref[...]
    cb = cb_ref[...]
    row = lax.broadcasted_iota(jnp.int32, (LRU_L // SUBLANES, SUBLANES, LRU_BLOCK), 1)

    def gates(r0, d):
        win = xs_ref[pl.ds(r0 - LRU_PAD, LRU_L + 2 * LRU_PAD), :]
        x = cb
        for k in range(4):
            x = x + cw[k:k + 1] * win[LRU_PAD - 2 + k:LRU_PAD - 2 + k + LRU_L]
        xb = x.astype(BF16)
        r = jax.nn.sigmoid(jnp.dot(xb, wa_ref[d, 0].astype(BF16), preferred_element_type=F32) + ba_ref[d:d + 1])
        ig = jax.nn.sigmoid(jnp.dot(xb, wx_ref[d, 0].astype(BF16), preferred_element_type=F32) + bx_ref[d:d + 1])
        log_a = -LRU_C * r * jax.nn.softplus(-lam_ref[d:d + 1])
        a = jnp.exp(log_a)
        return a, jnp.sqrt(1.0 - a * a) * ig * x

    def chunk_scan(a, b, carry, reverse):
        n = LRU_L // SUBLANES
        a = a.reshape(n, SUBLANES, LRU_BLOCK)
        b = b.reshape(n, SUBLANES, LRU_BLOCK)
        for s in (1, 2, 4):
            if reverse:
                keep = row < SUBLANES - s
                shift = SUBLANES - s
            else:
                keep = row >= s
                shift = s
            a_n = jnp.where(keep, pltpu.roll(a, shift, 1), 1.0)
            b_n = jnp.where(keep, pltpu.roll(b, shift, 1), 0.0)
            b = a * b_n + b
            a = a * a_n
        outs = [None] * n
        last = 0 if reverse else SUBLANES - 1
        for j in (range(n - 1, -1, -1) if reverse else range(n)):
            h = b[j] + a[j] * carry
            carry = h[last:last + 1]
            outs[j] = h
        return jnp.concatenate(outs, axis=0), carry

    def fwd_chunk(t0, r0, carry):
        a, b = gates(r0, 0)
        h, carry = chunk_scan(a, b, carry, False)
        hf_ref[pl.ds(t0, LRU_L), :] = h
        return carry

    def bwd_chunk(t0, r0, carry):
        a, b = gates(r0, 1)
        h, carry = chunk_scan(a, b, carry, True)
        return hf_ref[pl.ds(t0, LRU_L), :] + h, carry

    carry = jnp.zeros((1, LRU_BLOCK), F32)
    for c in range(n_c):
        carry = fwd_chunk(c * LRU_L, off_c + c * LRU_L, carry)

    def fwd_body(c, carry):
        t0 = pl.multiple_of(c * LRU_L, LRU_L)
        return fwd_chunk(t_c + t0, off_l + t0, carry)

    lax.fori_loop(0, n_l, fwd_body, carry, unroll=LRU_UNROLL)

    carry = jnp.zeros((1, LRU_BLOCK), F32)
    for c in range(n_c - 1, -1, -1):
        h, carry = bwd_chunk(c * LRU_L, off_c + c * LRU_L, carry)
        oc_ref[0, c * LRU_L:(c + 1) * LRU_L, :] = h * jax.nn.gelu(gc_ref[0, c * LRU_L:(c + 1) * LRU_L, :])

    def bwd_body(i, carry):
        t0 = pl.multiple_of((n_l - 1 - i) * LRU_L, LRU_L)
        h, carry = bwd_chunk(t_c + t0, off_l + t0, carry)
        ol_ref[0, pl.ds(t0, LRU_L), :] = h * jax.nn.gelu(gl_ref[0, pl.ds(t0, LRU_L), :])
        return carry

    lax.fori_loop(0, n_l, bwd_body, carry, unroll=LRU_UNROLL)


def _rglru(u_c, u_l, conv_w, conv_b, wa, ba, wx, bx, lam):
    bsz, t_c, _ = u_c.shape
    t_l = u_l.shape[1]
    x_blk = (EV_OUT + 2 * HG_KW + HG_VW) // LRU_BLOCK
    g_blk = (HG_KW + HG_VW) // LRU_BLOCK
    t_pad = t_c + t_l + 3 * LRU_PAD
    vec = lambda: pl.BlockSpec((2, LRU_BLOCK), lambda b, h: (0, h))
    return pl.pallas_call(
        functools.partial(_lru_kernel, (t_c, t_l)),
        grid=(bsz, LRU_HEADS),
        in_specs=[
            pl.BlockSpec((1, t_c, LRU_BLOCK), lambda b, h: (b, 0, x_blk + h)),
            pl.BlockSpec((1, t_c, LRU_BLOCK), lambda b, h: (b, 0, g_blk + h)),
            pl.BlockSpec((1, t_l, LRU_BLOCK), lambda b, h: (b, 0, x_blk + h)),
            pl.BlockSpec((1, t_l, LRU_BLOCK), lambda b, h: (b, 0, g_blk + h)),
            pl.BlockSpec((4, LRU_BLOCK), lambda b, h: (0, h)),
            pl.BlockSpec((1, LRU_BLOCK), lambda b, h: (0, h)),
            pl.BlockSpec((2, 1, LRU_BLOCK, LRU_BLOCK), lambda b, h: (0, h, 0, 0)),
            pl.BlockSpec((2, 1, LRU_BLOCK, LRU_BLOCK), lambda b, h: (0, h, 0, 0)),
            vec(), vec(), vec(),
        ],
        out_specs=[
            pl.BlockSpec((1, t_c, LRU_BLOCK), lambda b, h: (b, 0, h)),
            pl.BlockSpec((1, t_l, LRU_BLOCK), lambda b, h: (b, 0, h)),
        ],
        out_shape=[jax.ShapeDtypeStruct((bsz, t_c, LRU_WIDTH), F32), jax.ShapeDtypeStruct((bsz, t_l, LRU_WIDTH), F32)],
        scratch_shapes=[pltpu.VMEM((t_pad, LRU_BLOCK), F32), pltpu.VMEM((t_c + t_l, LRU_BLOCK), F32)],
        compiler_params=pltpu.CompilerParams(
            dimension_semantics=("parallel", "parallel"), vmem_limit_bytes=V7X_VMEM_LIMIT_BYTES),
        name="rglru",
    )(u_c, u_c, u_l, u_l, conv_w, conv_b[None], wa, wx, ba, bx, lam)


HG_GROUP = 8


def _hgrn2_kernel(seq, qc_ref, gc_ref, ffc_ref, fbc_ref, vc_ref, ql_ref, gl_ref, ffl_ref, fbl_ref, vl_ref,
                  lbf_ref, lbb_ref, gain_ref, oc_ref, ol_ref, of_ref):
    t_c, t_l = seq
    L = HG_CHUNK
    R = HG_GROUP * L
    n_gc = t_c // L
    r_i = lax.broadcasted_iota(jnp.int32, (R, R), 0)
    c_i = lax.broadcasted_iota(jnp.int32, (R, R), 1)
    same = (r_i // L) == (c_i // L)
    causal = (same & (c_i <= r_i), same & (c_i >= r_i))
    lb = (lbf_ref[...], lbb_ref[...])
    gain = gain_ref[...]
    V = L // SUBLANES

    def chunk_cumsum(d, n, x):
        x = x.reshape(n * V, SUBLANES, HG_DK)
        row = lax.broadcasted_iota(jnp.int32, x.shape, 1)
        for s in (1, 2, 4):
            keep, shift = (row < SUBLANES - s, SUBLANES - s) if d else (row >= s, s)
            x = x + jnp.where(keep, pltpu.roll(x, shift, 1), 0.0)
        x = x.reshape(n, V, SUBLANES, HG_DK)
        edge = 0 if d else SUBLANES - 1
        tot = jnp.broadcast_to(x[:, :, edge:edge + 1, :], x.shape)
        carry = jnp.zeros((n, SUBLANES, HG_DK), F32)
        outs = [None] * V
        for j in (range(V - 1, -1, -1) if d else range(V)):
            outs[j] = x[:, j] + carry
            carry = carry + tot[:, j]
        return jnp.stack(outs, axis=1).reshape(n * L, HG_DK)

    def group(d, n, q_raw, f_raw, v, st):
        rows = n * L
        f = lb[d] + (1.0 - lb[d]) * jax.nn.sigmoid(f_raw)
        k = 1.0 - f
        g = chunk_cumsum(d, n, jnp.log(f))
        g4 = g.reshape(n, L, HG_DK)
        g_last = g4[:, 0:1] if d else g4[:, L - 1:L]
        g_rest = (jnp.broadcast_to(g_last, (n, L, HG_DK)) - g4).reshape(rows, HG_DK)
        qg = (jax.nn.silu(q_raw) * jnp.exp(g)).astype(BF16)
        kg = (k * jnp.exp(-g)).astype(BF16)
        kd = (k * jnp.exp(g_rest)).astype(BF16)
        vb = v.astype(BF16)
        att = lax.dot_general(qg, kg, (((1,), (1,)), ((), ())), preferred_element_type=F32)
        att = jnp.where(causal[d][:rows, :rows], att, 0.0).astype(BF16)
        o = jnp.dot(att, vb, preferred_element_type=F32)
        dec = jnp.exp(g_last)
        sl = [slice(j * L, (j + 1) * L) for j in range(n)]
        ds_t = [lax.dot_general(vb[s], kd[s], (((0,), (0,)), ((), ())), preferred_element_type=F32) for s in sl]
        entering = [None] * n
        for j in (range(n - 1, -1, -1) if d else range(n)):
            entering[j] = st.astype(BF16)
            st = dec[j] * st + ds_t[j]
        outs = [o[s] + lax.dot_general(qg[s], e, (((1,), (1,)), ((), ())), preferred_element_type=F32)
                for s, e in zip(sl, entering)]
        return jnp.concatenate(outs, axis=0), st

    def finish(o, gate):
        return _rms(o, gain) * jax.nn.silu(gate)

    def lat_rows(i):
        return pl.ds(pl.multiple_of(i * R, R), R), pl.ds(pl.multiple_of(t_c + i * R, math.gcd(t_c, R)), R)

    st = jnp.zeros((HG_DV, HG_DK), F32)
    o, st = group(0, n_gc, qc_ref[0], ffc_ref[0], vc_ref[0], st)
    of_ref[0:t_c, :] = o

    def fwd_body(i, st):
        rows, srows = lat_rows(i)
        o, st = group(0, HG_GROUP, ql_ref[0, rows, :], ffl_ref[0, rows, :], vl_ref[0, rows, :], st)
        of_ref[srows, :] = o
        return st

    lax.fori_loop(0, t_l // R, fwd_body, st, unroll=2)

    st = jnp.zeros((HG_DV, HG_DK), F32)
    o, st = group(1, n_gc, qc_ref[0], fbc_ref[0], vc_ref[0], st)
    oc_ref[0] = finish(of_ref[0:t_c, :] + o, gc_ref[0])

    def bwd_body(i, st):
        rows, srows = lat_rows(t_l // R - 1 - i)
        o, st = group(1, HG_GROUP, ql_ref[0, rows, :], fbl_ref[0, rows, :], vl_ref[0, rows, :], st)
        ol_ref[0, rows, :] = finish(of_ref[srows, :] + o, gl_ref[0, rows, :])
        return st

    lax.fori_loop(0, t_l // R, bwd_body, st, unroll=2)


def _hgrn2(u_c, u_l, lb_f, lb_b, onorm):
    bsz, t_c, _ = u_c.shape
    t_l = u_l.shape[1]
    nb = HG_KW // HG_DK
    col = lambda t, blk: pl.BlockSpec((1, t, HG_DK), lambda b, h: (b, 0, blk + h))
    seq_cols = lambda t: [col(t, 0), col(t, nb), col(t, EV_OUT // HG_DK), col(t, EV_OUT // HG_DK + nb), col(t, EV_OUT // HG_DK + 2 * nb)]
    vec = lambda: pl.BlockSpec((1, HG_DK), lambda b, h: (0, h))
    return pl.pallas_call(
        functools.partial(_hgrn2_kernel, (t_c, t_l)),
        grid=(bsz, HG_HEADS),
        in_specs=seq_cols(t_c) + seq_cols(t_l) + [vec(), vec(), vec()],
        out_specs=[
            pl.BlockSpec((1, t_c, HG_DV), lambda b, h: (b, 0, h)),
            pl.BlockSpec((1, t_l, HG_DV), lambda b, h: (b, 0, h)),
        ],
        out_shape=[jax.ShapeDtypeStruct((bsz, t_c, HG_VW), F32), jax.ShapeDtypeStruct((bsz, t_l, HG_VW), F32)],
        scratch_shapes=[pltpu.VMEM((t_c + t_l, HG_DV), F32)],
        compiler_params=pltpu.CompilerParams(
            dimension_semantics=("parallel", "parallel"), vmem_limit_bytes=V7X_VMEM_LIMIT_BYTES),
        name="hgrn2",
    )(*([u_c] * 5), *([u_l] * 5), lb_f[None], lb_b[None], onorm[None])


LANES = 128
MLA_HW = 2 * LANES
MLA_TM = 512
MLA_TQ = 1024
MLA_TK = 512
MLA_VW = MLA_V + LANES
MLA_SCALE_LOG2E = (MLA_NOPE + MLA_ROPE) ** -0.5 * math.log2(math.e)


def _rms(x, gain):
    return x * lax.rsqrt(jnp.mean(x * x, axis=-1, keepdims=True) + EPS) * gain


def _rope_tile(r, cos_t, sin_t):
    return r * cos_t + (pltpu.roll(r, MLA_ROPE // 2, 1) + pltpu.roll(r, LANES - MLA_ROPE // 2, 1)) * sin_t


def _mla_q_kernel(u_ref, gain_ref, w_ref, cos_ref, sin_ref, q_ref):
    q = jnp.dot(_rms(u_ref[0], gain_ref[...]).astype(BF16), w_ref[...], preferred_element_type=F32)
    for h in range(MLA_HEADS):
        lo = h * MLA_HW
        q_ref[0, :, lo:lo + LANES] = q[:, lo:lo + LANES].astype(BF16)
        q_ref[0, :, lo + LANES:lo + MLA_HW] = _rope_tile(q[:, lo + LANES:lo + MLA_HW], cos_ref[...], sin_ref[...]).astype(BF16)


def _mla_kv_kernel(u_ref, kr_ref, gain_ref, w_ref, cos_ref, sin_ref, k_ref, v_ref):
    kv = jnp.dot(_rms(u_ref[0], gain_ref[...]).astype(BF16), w_ref[...], preferred_element_type=F32)
    kr = _rope_tile(kr_ref[0], cos_ref[...], sin_ref[...]).astype(BF16)
    for h in range(MLA_HEADS):
        k_ref[0, :, h * MLA_HW:h * MLA_HW + LANES] = kv[:, h * LANES:(h + 1) * LANES].astype(BF16)
        k_ref[0, :, h * MLA_HW + LANES:(h + 1) * MLA_HW] = kr
    lane = lax.broadcasted_iota(jnp.int32, (kv.shape[0], LANES), 1)
    one_hot = jnp.where(lane == 0, 1.0, 0.0).astype(BF16)
    for h in range(MLA_HEADS):
        v_ref[0, :, h * MLA_VW:h * MLA_VW + MLA_V] = kv[:, (MLA_HEADS + h) * LANES:(MLA_HEADS + h + 1) * LANES].astype(BF16)
        v_ref[0, :, h * MLA_VW + MLA_V:(h + 1) * MLA_VW] = one_hot


def _mla_q(u, gain, w, cos_t, sin_t):
    bsz, t, _ = u.shape
    tm = min(MLA_TM, t)
    return pl.pallas_call(
        _mla_q_kernel,
        grid=(bsz, t // tm),
        in_specs=[
            pl.BlockSpec((1, tm, MLA_Q_LORA), lambda b, i: (b, i, 0)),
            pl.BlockSpec((1, MLA_Q_LORA), lambda b, i: (0, 0)),
            pl.BlockSpec((MLA_Q_LORA, MLA_HEADS * MLA_HW), lambda b, i: (0, 0)),
            pl.BlockSpec((tm, LANES), lambda b, i: (i, 0)),
            pl.BlockSpec((tm, LANES), lambda b, i: (i, 0)),
        ],
        out_specs=pl.BlockSpec((1, tm, MLA_HEADS * MLA_HW), lambda b, i: (b, i, 0)),
        out_shape=jax.ShapeDtypeStruct((bsz, t, MLA_HEADS * MLA_HW), BF16),
        compiler_params=pltpu.CompilerParams(
            dimension_semantics=("parallel", "parallel"), vmem_limit_bytes=V7X_VMEM_LIMIT_BYTES),
        name="mla_q_up",
    )(u, gain[None], w, cos_t, sin_t)


def _mla_kv(u, c_blk, r_blk, gain, w, cos_t, sin_t):
    bsz, t, _ = u.shape
    tm = min(MLA_TM, t)
    return pl.pallas_call(
        _mla_kv_kernel,
        grid=(bsz, t // tm),
        in_specs=[
            pl.BlockSpec((1, tm, MLA_KV_LORA), lambda b, i: (b, i, c_blk)),
            pl.BlockSpec((1, tm, LANES), lambda b, i: (b, i, r_blk)),
            pl.BlockSpec((1, MLA_KV_LORA), lambda b, i: (0, 0)),
            pl.BlockSpec((MLA_KV_LORA, MLA_HEADS * (LANES + MLA_V)), lambda b, i: (0, 0)),
            pl.BlockSpec((tm, LANES), lambda b, i: (i, 0)),
            pl.BlockSpec((tm, LANES), lambda b, i: (i, 0)),
        ],
        out_specs=[
            pl.BlockSpec((1, tm, MLA_HEADS * MLA_HW), lambda b, i: (b, i, 0)),
            pl.BlockSpec((1, tm, MLA_HEADS * MLA_VW), lambda b, i: (b, i, 0)),
        ],
        out_shape=[jax.ShapeDtypeStruct((bsz, t, MLA_HEADS * MLA_HW), BF16),
                   jax.ShapeDtypeStruct((bsz, t, MLA_HEADS * MLA_VW), BF16)],
        compiler_params=pltpu.CompilerParams(
            dimension_semantics=("parallel", "parallel"), vmem_limit_bytes=V7X_VMEM_LIMIT_BYTES),
        name="mla_kv_up",
    )(u, u, gain[None], w, cos_t, sin_t)


def _flash_kernel(t_l, q_ref, kc_ref, vc_ref, kl_ref, vl_ref, o_ref):
    q = q_ref[0]

    tiles = [(kc_ref, vc_ref, slice(None))] + [(kl_ref, vl_ref, slice(i * MLA_TK, (i + 1) * MLA_TK)) for i in range(t_l // MLA_TK)]

    def scores(tile):
        k_ref, _, rows = tile
        return lax.dot_general(q, k_ref[0, rows, :], (((1,), (1,)), ((), ())), preferred_element_type=F32)

    tq = q.shape[0]
    m = jnp.full((tq, 1), -jnp.inf, F32)
    acc = jnp.zeros((tq, MLA_VW), F32)
    s_next = scores(tiles[0])
    for i, (_, v_ref, rows) in enumerate(tiles):
        s = s_next * MLA_SCALE_LOG2E
        if i + 1 < len(tiles):
            s_next = scores(tiles[i + 1])
        m_new = jnp.maximum(m, jnp.max(s, axis=-1, keepdims=True))
        p = jnp.exp2(s - m_new)
        acc = jnp.exp2(m - m_new) * acc + jnp.dot(p.astype(BF16), v_ref[0, rows, :], preferred_element_type=F32)
        m = m_new
    o_ref[0] = acc[:, :MLA_V] / acc[:, MLA_V:MLA_V + 1]


def _flash(q, k_c, v_c, k_l, v_l):
    bsz, t, _ = q.shape
    t_c, t_l = k_c.shape[1], k_l.shape[1]
    return pl.pallas_call(
        functools.partial(_flash_kernel, t_l),
        grid=(bsz, MLA_HEADS, t // MLA_TQ),
        in_specs=[
            pl.BlockSpec((1, MLA_TQ, MLA_HW), lambda b, h, i: (b, i, h)),
            pl.BlockSpec((1, t_c, MLA_HW), lambda b, h, i: (b, 0, h)),
            pl.BlockSpec((1, t_c, MLA_VW), lambda b, h, i: (b, 0, h)),
            pl.BlockSpec((1, t_l, MLA_HW), lambda b, h, i: (b, 0, h)),
            pl.BlockSpec((1, t_l, MLA_VW), lambda b, h, i: (b, 0, h)),
        ],
        out_specs=pl.BlockSpec((1, MLA_TQ, MLA_V), lambda b, h, i: (b, i, h)),
        out_shape=jax.ShapeDtypeStruct((bsz, t, MLA_HEADS * MLA_V), F32),
        compiler_params=pltpu.CompilerParams(
            dimension_semantics=("parallel", "parallel", "arbitrary"), vmem_limit_bytes=V7X_VMEM_LIMIT_BYTES),
        name="mla_flash",
    )(q, k_c, v_c, k_l, v_l)


def _mla(u_c, u_l, q_norm, w_q_up, kv_norm, w_kv_up, cos, sin):
    t_c, t_l = u_c.shape[1], u_l.shape[1]
    zeros = jnp.zeros((t_l, LANES - MLA_ROPE), F32)
    cos_l = jnp.concatenate([cos, cos, zeros], axis=-1)
    sin_l = jnp.concatenate([-sin, sin, zeros], axis=-1)
    cos_c = jnp.concatenate([jnp.ones((t_c, MLA_ROPE), F32), zeros[:t_c]], axis=-1)
    sin_c = jnp.zeros((t_c, LANES), F32)
    wq = w_q_up.reshape(MLA_Q_LORA, MLA_HEADS, MLA_NOPE + MLA_ROPE)
    wq = jnp.pad(wq, ((0, 0), (0, 0), (0, MLA_HW - MLA_NOPE - MLA_ROPE))).reshape(MLA_Q_LORA, -1).astype(BF16)
    wkv = w_kv_up.reshape(MLA_KV_LORA, MLA_HEADS, MLA_NOPE + MLA_V)
    wkv = jnp.concatenate([wkv[..., :MLA_NOPE].reshape(MLA_KV_LORA, -1), wkv[..., MLA_NOPE:].reshape(MLA_KV_LORA, -1)], axis=-1).astype(BF16)
    q = _mla_q(u_l, q_norm, wq, cos_l, sin_l)
    k_l, v_l = _mla_kv(u_l, OD_OUT // MLA_KV_LORA, (OD_OUT + MLA_KV_LORA) // LANES, kv_norm, wkv, cos_l, sin_l)
    k_c, v_c = _mla_kv(u_c, 0, MLA_KV_LORA // LANES, kv_norm, wkv, cos_c, sin_c)
    return _flash(q, k_c, v_c, k_l, v_l)


OD_BLK_Z = MLA_Q_LORA // LANES
OD_BLK_CKV = OD_OUT // LANES
OD_BLK_KR = OD_BLK_CKV + MLA_KV_LORA // LANES
OD_BLK_XBC = OD_BLK_KR + 1
SSD_XBC_BLKS = SSD_XBC // LANES
SSD_ACT_BLKS = SSD_XBC_BLKS + SSD_GROUPS
SSD_GH = SSD_HEADS // SSD_GROUPS
SSD_GW = SSD_WIDTH // SSD_GROUPS
SSD_TILE = 2 * SSD_CHUNK
SSD_CONV_ROWS = 512


def _od_weight_layout(w_in, ctx_only):
    k = w_in.shape[0]
    z = lambda n: jnp.zeros((k, n), w_in.dtype)
    o = OD_OUT + MLA_KV_LORA
    xbc0 = o + MLA_ROPE
    dt0 = xbc0 + SSD_XBC
    cols = [] if ctx_only else [w_in[:, :OD_OUT]]
    cols += [w_in[:, OD_OUT:o], w_in[:, o:xbc0], z(LANES - MLA_ROPE), w_in[:, xbc0:dt0]]
    for g in range(SSD_GROUPS):
        cols += [w_in[:, dt0 + g * SSD_GH:dt0 + (g + 1) * SSD_GH],
                 w_in[:, dt0 + SSD_HEADS + g * SSD_GH:dt0 + SSD_HEADS + (g + 1) * SSD_GH], z(LANES - 2 * SSD_GH)]
    w = jnp.concatenate(cols, axis=1)
    return jnp.pad(w, ((0, 0), (0, -w.shape[1] % (10 * LANES) if ctx_only else -w.shape[1] % (8 * LANES))))


def _ssd_group_lanes(v):
    rows = [jnp.concatenate([v[0, g * SSD_GH:(g + 1) * SSD_GH], v[1, g * SSD_GH:(g + 1) * SSD_GH],
                             jnp.zeros((LANES - 2 * SSD_GH,), v.dtype)]) for g in range(SSD_GROUPS)]
    return jnp.stack(rows)


def _ssd_act_kernel(seq, xc_ref, xl_ref, cw_ref, cb_ref, o_ref, xs_ref):
    t_c, t_l = seq
    j = pl.program_id(1)

    @pl.when(j < SSD_XBC_BLKS)
    def _():
        off_c, off_l = LRU_PAD, 2 * LRU_PAD + t_c
        zeros = jnp.zeros((LRU_PAD, LANES), F32)
        xs_ref[0:LRU_PAD] = zeros
        xs_ref[off_c + t_c:off_l] = zeros
        xs_ref[off_l + t_l:off_l + t_l + LRU_PAD] = zeros
        xs_ref[off_c:off_c + t_c] = xc_ref[0]
        xs_ref[off_l:off_l + t_l] = xl_ref[0]
        cw = cw_ref[...]
        cb = cb_ref[...]

        def conv(r0, rows):
            win = xs_ref[pl.ds(r0 - LRU_PAD, rows + 2 * LRU_PAD), :]
            y = cb
            for k in range(4):
                y = y + cw[k:k + 1] * win[LRU_PAD - 2 + k:LRU_PAD - 2 + k + rows]
            return jax.nn.silu(y)

        o_ref[0, 0:t_c, :] = conv(off_c, t_c)

        def body(i, _):
            r0 = pl.multiple_of(i * SSD_CONV_ROWS, SSD_CONV_ROWS)
            o_ref[0, pl.ds(t_c + r0, SSD_CONV_ROWS), :] = conv(off_l + r0, SSD_CONV_ROWS)
            return 0

        lax.fori_loop(0, t_l // SSD_CONV_ROWS, body, 0)

    @pl.when(j >= SSD_XBC_BLKS)
    def _():
        o_ref[0, 0:t_c, :] = jax.nn.softplus(xc_ref[0] + cb_ref[...])
        o_ref[0, t_c:t_c + t_l, :] = jax.nn.softplus(xl_ref[0] + cb_ref[...])


def _ssd_act(u_c, u_l, conv_w, conv_b, dt_bias):
    bsz, t_c, _ = u_c.shape
    t_l = u_l.shape[1]
    cw = jnp.pad(conv_w, ((0, 0), (0, SSD_GROUPS * LANES)))
    cb = jnp.concatenate([conv_b, _ssd_group_lanes(dt_bias).reshape(-1)])[None]
    ctx_blk = OD_BLK_XBC - OD_BLK_CKV
    return pl.pallas_call(
        functools.partial(_ssd_act_kernel, (t_c, t_l)),
        grid=(bsz, SSD_ACT_BLKS),
        in_specs=[
            pl.BlockSpec((1, t_c, LANES), lambda b, j: (b, 0, ctx_blk + j)),
            pl.BlockSpec((1, t_l, LANES), lambda b, j: (b, 0, OD_BLK_XBC + j)),
            pl.BlockSpec((4, LANES), lambda b, j: (0, j)),
            pl.BlockSpec((1, LANES), lambda b, j: (0, j)),
        ],
        out_specs=pl.BlockSpec((1, t_c + t_l, LANES), lambda b, j: (b, 0, j)),
        out_shape=jax.ShapeDtypeStruct((bsz, t_c + t_l, SSD_ACT_BLKS * LANES), F32),
        scratch_shapes=[pltpu.VMEM((t_c + t_l + 3 * LRU_PAD, LANES), F32)],
        compiler_params=pltpu.CompilerParams(
            dimension_semantics=("parallel", "parallel"), vmem_limit_bytes=V7X_VMEM_LIMIT_BYTES),
        name="ssd_act",
    )(u_c, u_l, cw, cb)


def _split3(x):
    hi = x.astype(BF16)
    r1 = x - hi.astype(F32)
    mid = r1.astype(BF16)
    return hi, mid, (r1 - mid.astype(F32)).astype(BF16)


def _ssd_scan_kernel(d, n_tiles, *refs):
    if d == 0:
        x_ref, b_ref, c_ref, dt_ref, alog_ref, y_ref, s_ref = refs
    else:
        x_ref, b_ref, c_ref, dt_ref, alog_ref, yf_ref, z_ref, dskip_ref, gain_ref, y_ref, s_ref = refs
    L = SSD_CHUNK
    P = SSD_HEADDIM

    @pl.when(pl.program_id(2) == 0)
    def _():
        s_ref[...] = jnp.zeros_like(s_ref)

    r_i = lax.broadcasted_iota(jnp.int32, (L, L), 0)
    c_i = lax.broadcasted_iota(jnp.int32, (L, L), 1)
    causal = (c_i >= r_i) if d else (c_i <= r_i)
    tri = jnp.where(causal, 1.0, 0.0).astype(BF16)
    low_half = lax.broadcasted_iota(jnp.int32, (L, LANES), 1) < P
    a_lane = -jnp.exp(alog_ref[...])

    def pair_cols(cols, p):
        return jnp.where(low_half, cols[2 * p], cols[2 * p + 1])

    for ch in ((1, 0) if d else (0, 1)):
        rows = slice(ch * L, (ch + 1) * L)
        xs = x_ref[0, rows, :]
        bm = b_ref[0, rows, :].astype(BF16)
        cm = c_ref[0, rows, :].astype(BF16)
        dt = dt_ref[0, rows, :]
        a3 = _split3(dt * a_lane)
        cs3 = jnp.dot(tri, jnp.concatenate(a3, axis=1), preferred_element_type=F32)
        cs_col = cs3[:, :LANES] + cs3[:, LANES:2 * LANES] + cs3[:, 2 * LANES:]
        cs_row = sum(lax.dot_general(t, tri, (((0,), (1,)), ((), ())), preferred_element_type=F32) for t in a3)
        cb = lax.dot_general(cm, bm, (((1,), (1,)), ((), ())), preferred_element_type=F32)
        s_in = s_ref[...].astype(BF16)
        y_off = lax.dot_general(cm, s_in, (((1,), (1,)), ((), ())), preferred_element_type=F32)
        last = 0 if d else L - 1
        cols, dts, lasts = [], [], []
        for i in range(SSD_GH):
            lane = SSD_GH * d + i
            cols.append(cs_col[:, lane:lane + 1])
            dts.append(dt[:, lane:lane + 1])
            lasts.append(cs_col[last:last + 1, lane:lane + 1])
        ys, xdecs = [], []
        for p in range(SSD_GH // 2):
            xp = xs[:, p * LANES:(p + 1) * LANES]
            col2 = pair_cols(cols, p)
            xdt = xp * pair_cols(dts, p)
            xdt_b = xdt.astype(BF16)
            y2 = []
            for e in range(2):
                i = 2 * p + e
                lane = SSD_GH * d + i
                decay = jnp.where(causal, jnp.exp(cols[i] - cs_row[lane:lane + 1, :]), 0.0)
                y2.append(jnp.dot((cb * decay).astype(BF16), xdt_b, preferred_element_type=F32))
            ys.append(jnp.where(low_half, y2[0], y2[1]) + y_off[:, p * LANES:(p + 1) * LANES] * jnp.exp(col2))
            last2 = jnp.where(low_half, lasts[2 * p], lasts[2 * p + 1])
            xdecs.append((xdt * jnp.exp(last2 - col2)).astype(BF16))
        y = jnp.concatenate(ys, axis=1)
        ds = lax.dot_general(jnp.concatenate(xdecs, axis=1), bm, (((0,), (0,)), ((), ())), preferred_element_type=F32)
        dec = jnp.concatenate([jnp.broadcast_to(jnp.exp(lasts[i]), (P, SSD_STATE)) for i in range(SSD_GH)], axis=0)
        s_ref[...] = dec * s_ref[...] + ds
        if d == 0:
            y_ref[0, rows, :] = y
        else:
            out = (yf_ref[0, rows, :] + y + dskip_ref[...] * xs) * jax.nn.silu(z_ref[0, rows, :])
            y_ref[0, rows, :] = _rms(out, gain_ref[...])


def _ssd_scan(d, act, t_c, a_lanes, extra=()):
    bsz, t, _ = act.shape
    n_tiles = t // SSD_TILE
    n_ctx = t_c // SSD_TILE
    assert n_ctx == 1
    if d == 0:
        tile = lambda i: i
    else:
        tile = lambda i: jnp.where(i < n_ctx, n_ctx - 1 - i, n_tiles + n_ctx - 1 - i)
    in_specs = [
        pl.BlockSpec((1, SSD_TILE, SSD_GW), lambda b, g, i: (b, tile(i), g)),
        pl.BlockSpec((1, SSD_TILE, LANES), lambda b, g, i: (b, tile(i), SSD_WIDTH // LANES + g)),
        pl.BlockSpec((1, SSD_TILE, LANES), lambda b, g, i: (b, tile(i), (SSD_WIDTH + SSD_BC) // LANES + g)),
        pl.BlockSpec((1, SSD_TILE, LANES), lambda b, g, i: (b, tile(i), SSD_XBC_BLKS + g)),
        pl.BlockSpec((1, LANES), lambda b, g, i: (0, g)),
    ]
    args = [act, act, act, act, a_lanes.reshape(1, -1)]
    if d:
        y_f, u_l, d_skip, onorm = extra
        in_specs += [
            pl.BlockSpec((1, SSD_TILE, SSD_GW), lambda b, g, i: (b, tile(i), g)),
            pl.BlockSpec((1, SSD_TILE, SSD_GW), lambda b, g, i: (b, jnp.maximum(tile(i) - n_ctx, 0), OD_BLK_Z * LANES // SSD_GW + g)),
            pl.BlockSpec((1, SSD_GW), lambda b, g, i: (0, g)),
            pl.BlockSpec((1, SSD_GW), lambda b, g, i: (0, g)),
        ]
        args += [y_f, u_l, jnp.repeat(d_skip, SSD_HEADDIM)[None], onorm[None]]
        out_tile = lambda i: jnp.where(i < n_ctx, n_tiles - n_ctx - 1, tile(i) - n_ctx)
        out_rows = t - t_c
    else:
        out_tile, out_rows = tile, t
    return pl.pallas_call(
        functools.partial(_ssd_scan_kernel, d, n_tiles),
        grid=(bsz, SSD_GROUPS, n_tiles),
        in_specs=in_specs,
        out_specs=pl.BlockSpec((1, SSD_TILE, SSD_GW), lambda b, g, i: (b, out_tile(i), g)),
        out_shape=jax.ShapeDtypeStruct((bsz, out_rows, SSD_WIDTH), F32),
        scratch_shapes=[pltpu.VMEM((SSD_GW, SSD_STATE), F32)],
        compiler_params=pltpu.CompilerParams(
            dimension_semantics=("parallel", "parallel", "arbitrary"), vmem_limit_bytes=V7X_VMEM_LIMIT_BYTES),
        name="ssd_scan_bwd" if d else "ssd_scan_fwd",
    )(*args)


def _ssd(u_c, u_l, conv_w, conv_b, dt_bias, a_log, d_skip, onorm):
    act = _ssd_act(u_c, u_l, conv_w, conv_b, dt_bias)
    a_lanes = _ssd_group_lanes(a_log)
    y_f = _ssd_scan(0, act, u_c.shape[1], a_lanes)
    return _ssd_scan(1, act, u_c.shape[1], a_lanes, (y_f, u_l, d_skip, onorm))


def axial_rope(rows):
    row = jnp.repeat(jnp.arange(rows, dtype=F32), GRID_W)
    col = jnp.arange(rows * GRID_W) % GRID_W
    n_freq = MLA_ROPE // 4
    inv = ROPE_THETA ** (-jnp.arange(n_freq, dtype=F32) / n_freq)
    ang = jnp.concatenate([row[:, None] * inv, col.astype(F32)[:, None] * inv], axis=-1)
    return jnp.cos(ang), jnp.sin(ang)


def even_mixer(u_c, u_l, lb_f, lb_b, onorm, conv_w, conv_b, wa, ba, wx, bx, lam, need_ctx):
    assert need_ctx
    lru_c, lru_l = _rglru(u_c, u_l, conv_w, conv_b, wa, ba, wx, bx, lam)
    hg_c, hg_l = _hgrn2(u_c, u_l, lb_f, lb_b, onorm)
    return (hg_c, lru_c), (hg_l, lru_l)


def odd_mixer(u_c, u_l, q_norm, w_q_up, kv_norm, w_kv_up, conv_w, conv_b, dt_bias, a_log, d_skip, onorm, cos, sin, need_ctx):
    assert not need_ctx
    att_l = _mla(u_c, u_l, q_norm, w_q_up, kv_norm, w_kv_up, cos, sin)
    ssd_l = _ssd(u_c, u_l, conv_w, conv_b, dt_bias, a_log, d_skip, onorm)
    return None, (att_l, ssd_l)


def ec_moe(x, gain, sc, sh, g2, w_router, w_gate, w_up, w_down, layer, merge_sets):
    b, t, d = x.shape
    cap = CAPACITY_FACTOR * t // N_EXPERTS
    h, aff = _router(x, gain, sc, sh, w_router)
    gate, idx = lax.top_k(aff, cap)
    bidx = jnp.arange(b)[:, None, None]
    xe = h[bidx, idx]
    if merge_sets:
        xe = jnp.swapaxes(xe, 0, 1).reshape(1, N_EXPERTS, b * cap, d)
        gate_m = jnp.swapaxes(gate, 0, 1).reshape(1, N_EXPERTS, b * cap)
        ye = _expert_ffn(xe, gate_m, jnp.repeat(g2, cap, axis=0)[None], w_gate, w_up, w_down, layer)
        ye = jnp.swapaxes(ye.reshape(N_EXPERTS, b, cap, d), 0, 1)
    else:
        ye = _expert_ffn(xe, gate, g2[:, None], w_gate, w_up, w_down, layer)
    return x.at[bidx, idx].add(ye)


def kernel(x, c, ctx, c_ctx, w_mod, b_mod, norm_mix, norm_ffn, w_out, ev_w_in, hg_lb, hg_onorm, lru_conv_w, lru_conv_b, lru_wa, lru_ba, lru_wx, lru_bx, lru_lambda, od_w_in, mla_q_norm, mla_w_q_up, mla_kv_norm, mla_w_kv_up, ssd_conv_w, ssd_conv_b, ssd_dt_bias, ssd_a_log, ssd_d, ssd_onorm, moe_router, moe_w_gate, moe_w_up, moe_w_down, final_norm):
    rows = x.shape[1] // GRID_W
    cos, sin = axial_rope(rows)
    lb_all = jnp.cumsum(jax.nn.softmax(hg_lb.astype(F32), axis=1), axis=1)
    bsz = x.shape[0]
    c_rows = jnp.concatenate([c, c_ctx[None], jnp.zeros((MOD_ROWS - bsz - 1, D_MODEL), F32)], axis=0)
    wg, wu, wd = moe_w_gate.astype(BF16), moe_w_up.astype(BF16), moe_w_down.astype(BF16)
    for l in range(DEPTH):
        need_ctx = l < DEPTH - 1
        j = l // 2
        mod_all = _modulation(c_rows, w_mod, b_mod, l)
        mod, mod_c = mod_all[:bsz], mod_all[bsz]
        sh1, sc1, g1, sh2, sc2, g2 = jnp.split(mod, 6, axis=-1)
        csh1, csc1, cg1, csh2, csc2, cg2 = (jnp.broadcast_to(v, (x.shape[0], D_MODEL)) for v in jnp.split(mod_c, 6, axis=-1))
        wo = w_out[l].astype(BF16)
        if l % 2 == 0:
            w_in = ev_w_in[j].astype(BF16)
            u_c = _proj_in(ctx, norm_mix[l], csc1, csh1, w_in)
            u_l = _proj_in(x, norm_mix[l], sc1, sh1, w_in)
            o_c, o_l = even_mixer(u_c, u_l, lb_all[0, j], lb_all[1, j], hg_onorm[j], lru_conv_w[j], lru_conv_b[j], lru_wa[j], lru_ba[j], lru_wx[j], lru_bx[j], lru_lambda[j], need_ctx)
        else:
            w_in = od_w_in[j]
            u_c = _proj_in(ctx, norm_mix[l], csc1, csh1, _od_weight_layout(w_in, True).astype(BF16))
            u_l = _proj_in(x, norm_mix[l], sc1, sh1, _od_weight_layout(w_in, False).astype(BF16))
            o_c, o_l = odd_mixer(u_c, u_l, mla_q_norm[j], mla_w_q_up[j], mla_kv_norm[j], mla_w_kv_up[j], ssd_conv_w[j], ssd_conv_b[j], ssd_dt_bias[j], ssd_a_log[j], ssd_d[j], ssd_onorm[j], cos, sin, need_ctx)
        x = _proj_out(*o_l, wo, x, g1)
        x = ec_moe(x, norm_ffn[l], sc2, sh2, g2, moe_router[l], wg, wu, wd, l, False)
        if need_ctx:
            ctx = _proj_out(*o_c, wo, ctx, cg1)
            ctx = ec_moe(ctx, norm_ffn[l], csc2, csh2, cg2, moe_router[l], wg, wu, wd, l, True)
    return _final_norm(x, final_norm)
```

```python
import functools
import math

import jax
import jax.numpy as jnp
from jax import lax
from jax.experimental import pallas as pl
from jax.experimental.pallas import tpu as pltpu

D_MODEL = 2048
DEPTH = 2
CTX_LEN = 256
GRID_W = 64
EPS = 1e-6
F32 = jnp.float32
BF16 = jnp.bfloat16

HG_HEADS = 8
HG_DK = 128
HG_DV = 128
HG_KW = HG_HEADS * HG_DK
HG_VW = HG_HEADS * HG_DV
HG_CHUNK = 64

LRU_WIDTH = 1024
LRU_HEADS = 8
LRU_C = 8.0

MLA_HEADS = 8
MLA_Q_LORA = 512
MLA_KV_LORA = 512
MLA_NOPE = 128
MLA_ROPE = 64
MLA_V = 128
ROPE_THETA = 10000.0

SSD_WIDTH = 1024
SSD_HEADDIM = 64
SSD_HEADS = SSD_WIDTH // SSD_HEADDIM
SSD_GROUPS = 2
SSD_STATE = 128
SSD_BC = SSD_GROUPS * SSD_STATE
SSD_XBC = SSD_WIDTH + 2 * SSD_BC
SSD_CHUNK = 128

N_EXPERTS = 16
CAPACITY_FACTOR = 2

EV_OUT = HG_KW + HG_VW + LRU_WIDTH
OD_OUT = MLA_Q_LORA + SSD_WIDTH

V7X_VMEM_LIMIT_BYTES = 56 * 1024 * 1024


def _lane_tile(n, target=1536):
    k = n // 128
    return 128 * max(d for d in range(1, k + 1) if k % d == 0 and 128 * d <= target)


PROJ_TM = 1024


def _proj_in_kernel(x_ref, gain_ref, sc_ref, sh_ref, w_ref, o_ref, h_ref):
    @pl.when(pl.program_id(2) == 0)
    def _():
        x = x_ref[0]
        y = x * lax.rsqrt(jnp.mean(x * x, axis=-1, keepdims=True) + EPS) * gain_ref[...]
        h_ref[...] = (y * (1.0 + sc_ref[0]) + sh_ref[0]).astype(BF16)

    o_ref[0] = jnp.dot(h_ref[...], w_ref[...], preferred_element_type=F32)


def _proj_in(x, gain, sc, sh, w):
    bsz, t, dm = x.shape
    n = w.shape[1]
    tm, tn = min(PROJ_TM, t), _lane_tile(n)
    return pl.pallas_call(
        _proj_in_kernel,
        grid=(bsz, t // tm, n // tn),
        in_specs=[
            pl.BlockSpec((1, tm, dm), lambda b, i, j: (b, i, 0)),
            pl.BlockSpec((1, dm), lambda b, i, j: (0, 0)),
            pl.BlockSpec((1, 1, dm), lambda b, i, j: (b, 0, 0)),
            pl.BlockSpec((1, 1, dm), lambda b, i, j: (b, 0, 0)),
            pl.BlockSpec((dm, tn), lambda b, i, j: (0, j)),
        ],
        out_specs=pl.BlockSpec((1, tm, tn), lambda b, i, j: (b, i, j)),
        out_shape=jax.ShapeDtypeStruct((bsz, t, n), F32),
        scratch_shapes=[pltpu.VMEM((tm, dm), BF16)],
        compiler_params=pltpu.CompilerParams(
            dimension_semantics=("parallel", "parallel", "arbitrary"), vmem_limit_bytes=V7X_VMEM_LIMIT_BYTES),
        name="proj_in",
    )(x, gain[None], sc[:, None], sh[:, None], w)


def _proj_out_kernel(ya_ref, yb_ref, wa_ref, wb_ref, x_ref, g_ref, o_ref):
    acc = jnp.dot(ya_ref[0].astype(BF16), wa_ref[...], preferred_element_type=F32)
    acc = acc + jnp.dot(yb_ref[0].astype(BF16), wb_ref[...], preferred_element_type=F32)
    o_ref[0] = x_ref[0] + g_ref[0] * acc


def _proj_out(ya, yb, w, x, g):
    bsz, t, kh = ya.shape
    dm = w.shape[1]
    tm, tn = min(PROJ_TM, t), _lane_tile(dm)
    half = lambda: pl.BlockSpec((1, tm, kh), lambda b, i, j: (b, i, 0))
    return pl.pallas_call(
        _proj_out_kernel,
        grid=(bsz, t // tm, dm // tn),
        in_specs=[
            half(), half(),
            pl.BlockSpec((kh, tn), lambda b, i, j: (0, j)),
            pl.BlockSpec((kh, tn), lambda b, i, j: (1, j)),
            pl.BlockSpec((1, tm, tn), lambda b, i, j: (b, i, j)),
            pl.BlockSpec((1, 1, tn), lambda b, i, j: (b, 0, j)),
        ],
        out_specs=pl.BlockSpec((1, tm, tn), lambda b, i, j: (b, i, j)),
        out_shape=jax.ShapeDtypeStruct((bsz, t, dm), F32),
        compiler_params=pltpu.CompilerParams(
            dimension_semantics=("parallel", "parallel", "arbitrary"), vmem_limit_bytes=V7X_VMEM_LIMIT_BYTES),
        name="proj_out",
    )(ya, yb, w, w, x, g[:, None])


MOD_ROWS = 8
MOD_TN = 1024


def _mod_kernel(c_ref, w_ref, b_ref, o_ref):
    o_ref[...] = jnp.dot(jax.nn.silu(c_ref[...]).astype(BF16), w_ref[0].astype(BF16), preferred_element_type=F32) + b_ref[0]


def _modulation(c_rows, w, b, layer):
    _, dm, n = w.shape
    return pl.pallas_call(
        _mod_kernel,
        grid=(n // MOD_TN,),
        in_specs=[
            pl.BlockSpec((MOD_ROWS, dm), lambda j: (0, 0)),
            pl.BlockSpec((1, dm, MOD_TN), lambda j: (layer, 0, j)),
            pl.BlockSpec((1, 1, MOD_TN), lambda j: (layer, 0, j)),
        ],
        out_specs=pl.BlockSpec((MOD_ROWS, MOD_TN), lambda j: (0, j)),
        out_shape=jax.ShapeDtypeStruct((MOD_ROWS, n), F32),
        compiler_params=pltpu.CompilerParams(dimension_semantics=("parallel",), vmem_limit_bytes=V7X_VMEM_LIMIT_BYTES),
        name="modulation",
    )(c_rows, w, b[:, None])


def _final_norm_kernel(x_ref, g_ref, o_ref):
    x = x_ref[0]
    o_ref[0] = x * lax.rsqrt(jnp.mean(x * x, axis=-1, keepdims=True) + EPS) * g_ref[...]


def _final_norm(x, gain):
    bsz, t, dm = x.shape
    return pl.pallas_call(
        _final_norm_kernel,
        grid=(bsz, t // PROJ_TM),
        in_specs=[pl.BlockSpec((1, PROJ_TM, dm), lambda b, i: (b, i, 0)), pl.BlockSpec((1, dm), lambda b, i: (0, 0))],
        out_specs=pl.BlockSpec((1, PROJ_TM, dm), lambda b, i: (b, i, 0)),
        out_shape=jax.ShapeDtypeStruct(x.shape, F32),
        compiler_params=pltpu.CompilerParams(dimension_semantics=("parallel", "parallel"), vmem_limit_bytes=V7X_VMEM_LIMIT_BYTES),
        name="final_norm",
    )(x, gain[None])


def _router_kernel(x_ref, gain_ref, sc_ref, sh_ref, wr_ref, h_ref, aff_ref):
    x = x_ref[0]
    y = x * lax.rsqrt(jnp.mean(x * x, axis=-1, keepdims=True) + EPS) * gain_ref[...]
    h = (y * (1.0 + sc_ref[0]) + sh_ref[0]).astype(BF16)
    h_ref[0] = h
    logits = lax.dot_general(wr_ref[...], h, (((1,), (1,)), ((), ())), preferred_element_type=F32)
    e = jnp.exp(logits - jnp.max(logits, axis=0, keepdims=True))
    aff_ref[0] = e / jnp.sum(e, axis=0, keepdims=True)


def _router(x, gain, sc, sh, w_router):
    bsz, t, dm = x.shape
    tm = min(PROJ_TM, t)
    return pl.pallas_call(
        _router_kernel,
        grid=(bsz, t // tm),
        in_specs=[
            pl.BlockSpec((1, tm, dm), lambda b, i: (b, i, 0)),
            pl.BlockSpec((1, dm), lambda b, i: (0, 0)),
            pl.BlockSpec((1, 1, dm), lambda b, i: (b, 0, 0)),
            pl.BlockSpec((1, 1, dm), lambda b, i: (b, 0, 0)),
            pl.BlockSpec((N_EXPERTS, dm), lambda b, i: (0, 0)),
        ],
        out_specs=[
            pl.BlockSpec((1, tm, dm), lambda b, i: (b, i, 0)),
            pl.BlockSpec((1, N_EXPERTS, tm), lambda b, i: (b, 0, i)),
        ],
        out_shape=[jax.ShapeDtypeStruct((bsz, t, dm), BF16), jax.ShapeDtypeStruct((bsz, N_EXPERTS, t), F32)],
        compiler_params=pltpu.CompilerParams(dimension_semantics=("parallel", "parallel"), vmem_limit_bytes=V7X_VMEM_LIMIT_BYTES),
        name="moe_router",
    )(x, gain[None], sc[:, None], sh[:, None], w_router.T.astype(BF16))


def _ffn_kernel(x_ref, gate_ref, g2_ref, wg_ref, wu_ref, wd_ref, o_ref):
    x = x_ref[0, 0]
    a = jnp.dot(x, wg_ref[0, 0], preferred_element_type=F32)
    u = jnp.dot(x, wu_ref[0, 0], preferred_element_type=F32)
    hid = (jax.nn.silu(a) * u).astype(BF16)
    o_ref[0, 0] = jnp.dot(hid, wd_ref[0, 0], preferred_element_type=F32) * gate_ref[0, 0] * g2_ref[0, 0]


def _expert_ffn(xe, gate, g2, w_gate, w_up, w_down, layer):
    s, e, c, dm = xe.shape
    ff = w_gate.shape[-1]
    g2_rows = g2.shape[1]
    return pl.pallas_call(
        _ffn_kernel,
        grid=(e, s),
        in_specs=[
            pl.BlockSpec((1, 1, c, dm), lambda ei, si: (si, ei, 0, 0)),
            pl.BlockSpec((1, 1, c, 1), lambda ei, si: (si, ei, 0, 0)),
            pl.BlockSpec((1, 1, g2_rows, dm), lambda ei, si: (si, 0, 0, 0)),
            pl.BlockSpec((1, 1, dm, ff), lambda ei, si: (layer, ei, 0, 0)),
            pl.BlockSpec((1, 1, dm, ff), lambda ei, si: (layer, ei, 0, 0)),
            pl.BlockSpec((1, 1, ff, dm), lambda ei, si: (layer, ei, 0, 0)),
        ],
        out_specs=pl.BlockSpec((1, 1, c, dm), lambda ei, si: (si, ei, 0, 0)),
        out_shape=jax.ShapeDtypeStruct((s, e, c, dm), F32),
        compiler_params=pltpu.CompilerParams(dimension_semantics=("parallel", "arbitrary"), vmem_limit_bytes=V7X_VMEM_LIMIT_BYTES),
        name="moe_ffn",
    )(xe, gate[..., None], g2[:, None], w_gate, w_up, w_down)


LRU_BLOCK = LRU_WIDTH // LRU_HEADS
LRU_L = 128
SUBLANES = 8
LRU_PAD = SUBLANES
LRU_UNROLL = 4


def _lru_kernel(seq, xc_ref, gc_ref, xl_ref, gl_ref, cw_ref, cb_ref, wa_ref, wx_ref, ba_ref, bx_ref, lam_ref,
                oc_ref, ol_ref, xs_ref, hf_ref):
    t_c, t_l = seq
    n_c, n_l = t_c // LRU_L, t_l // LRU_L
    off_c, off_l = LRU_PAD, 2 * LRU_PAD + t_c
    zeros = jnp.zeros((LRU_PAD, LRU_BLOCK), F32)
    xs_ref[0:LRU_PAD] = zeros
    xs_ref[off_c + t_c:off_l] = zeros
    xs_ref[off_l + t_l:off_l + t_l + LRU_PAD] = zeros
    xs_ref[off_c:off_c + t_c] = xc_ref[0]
    xs_ref[off_l:off_l + t_l] = xl_ref[0]

    cw = cw_ref[...]
    cb = cb_ref[...]
    row = lax.broadcasted_iota(jnp.int32, (LRU_L // SUBLANES, SUBLANES, LRU_BLOCK), 1)

    def gates(r0, d):
        x = cb
        for k in range(4):
            x = x + cw[k:k + 1] * xs_ref[pl.ds(r0 - 2 + k, LRU_L), :]
        xb = x.astype(BF16)
        r = jax.nn.sigmoid(jnp.dot(xb, wa_ref[d, 0].astype(BF16), preferred_element_type=F32) + ba_ref[d:d + 1])
        ig = jax.nn.sigmoid(jnp.dot(xb, wx_ref[d, 0].astype(BF16), preferred_element_type=F32) + bx_ref[d:d + 1])
        log_a = -LRU_C * r * jax.nn.softplus(-lam_ref[d:d + 1])
        a = jnp.exp(log_a)
        return a, jnp.sqrt(1.0 - a * a) * ig * x

    def chunk_scan(a, b, carry, reverse):
        n = LRU_L // SUBLANES
        a = a.reshape(n, SUBLANES, LRU_BLOCK)
        b = b.reshape(n, SUBLANES, LRU_BLOCK)
        for s in (1, 2, 4):
            if reverse:
                keep = row < SUBLANES - s
                shift = SUBLANES - s
            else:
                keep = row >= s
                shift = s
            a_n = jnp.where(keep, pltpu.roll(a, shift, 1), 1.0)
            b_n = jnp.where(keep, pltpu.roll(b, shift, 1), 0.0)
            b = a * b_n + b
            a = a * a_n
        outs = [None] * n
        last = 0 if reverse else SUBLANES - 1
        for j in (range(n - 1, -1, -1) if reverse else range(n)):
            h = b[j] + a[j] * carry
            carry = h[last:last + 1]
            outs[j] = h
        return jnp.concatenate(outs, axis=0), carry

    def fwd_chunk(t0, r0, carry):
        a, b = gates(r0, 0)
        h, carry = chunk_scan(a, b, carry, False)
        hf_ref[pl.ds(t0, LRU_L), :] = h
        return carry

    def bwd_chunk(t0, r0, carry):
        a, b = gates(r0, 1)
        h, carry = chunk_scan(a, b, carry, True)
        return hf_ref[pl.ds(t0, LRU_L), :] + h, carry

    carry = jnp.zeros((1, LRU_BLOCK), F32)
    for c in range(n_c):
        carry = fwd_chunk(c * LRU_L, off_c + c * LRU_L, carry)

    def fwd_body(c, carry):
        t0 = pl.multiple_of(c * LRU_L, LRU_L)
        return fwd_chunk(t_c + t0, off_l + t0, carry)

    lax.fori_loop(0, n_l, fwd_body, carry, unroll=LRU_UNROLL)

    carry = jnp.zeros((1, LRU_BLOCK), F32)
    for c in range(n_c - 1, -1, -1):
        h, carry = bwd_chunk(c * LRU_L, off_c + c * LRU_L, carry)
        oc_ref[0, c * LRU_L:(c + 1) * LRU_L, :] = h * jax.nn.gelu(gc_ref[0, c * LRU_L:(c + 1) * LRU_L, :])

    def bwd_body(i, carry):
        t0 = pl.multiple_of((n_l - 1 - i) * LRU_L, LRU_L)
        h, carry = bwd_chunk(t_c + t0, off_l + t0, carry)
        ol_ref[0, pl.ds(t0, LRU_L), :] = h * jax.nn.gelu(gl_ref[0, pl.ds(t0, LRU_L), :])
        return carry

    lax.fori_loop(0, n_l, bwd_body, carry, unroll=LRU_UNROLL)


def _rglru(u_c, u_l, conv_w, conv_b, wa, ba, wx, bx, lam):
    bsz, t_c, _ = u_c.shape
    t_l = u_l.shape[1]
    x_blk = (EV_OUT + 2 * HG_KW + HG_VW) // LRU_BLOCK
    g_blk = (HG_KW + HG_VW) // LRU_BLOCK
    t_pad = t_c + t_l + 3 * LRU_PAD
    vec = lambda: pl.BlockSpec((2, LRU_BLOCK), lambda b, h: (0, h))
    return pl.pallas_call(
        functools.partial(_lru_kernel, (t_c, t_l)),
        grid=(bsz, LRU_HEADS),
        in_specs=[
            pl.BlockSpec((1, t_c, LRU_BLOCK), lambda b, h: (b, 0, x_blk + h)),
            pl.BlockSpec((1, t_c, LRU_BLOCK), lambda b, h: (b, 0, g_blk + h)),
            pl.BlockSpec((1, t_l, LRU_BLOCK), lambda b, h: (b, 0, x_blk + h)),
            pl.BlockSpec((1, t_l, LRU_BLOCK), lambda b, h: (b, 0, g_blk + h)),
            pl.BlockSpec((4, LRU_BLOCK), lambda b, h: (0, h)),
            pl.BlockSpec((1, LRU_BLOCK), lambda b, h: (0, h)),
            pl.BlockSpec((2, 1, LRU_BLOCK, LRU_BLOCK), lambda b, h: (0, h, 0, 0)),
            pl.BlockSpec((2, 1, LRU_BLOCK, LRU_BLOCK), lambda b, h: (0, h, 0, 0)),
            vec(), vec(), vec(),
        ],
        out_specs=[
            pl.BlockSpec((1, t_c, LRU_BLOCK), lambda b, h: (b, 0, h)),
            pl.BlockSpec((1, t_l, LRU_BLOCK), lambda b, h: (b, 0, h)),
        ],
        out_shape=[jax.ShapeDtypeStruct((bsz, t_c, LRU_WIDTH), F32), jax.ShapeDtypeStruct((bsz, t_l, LRU_WIDTH), F32)],
        scratch_shapes=[pltpu.VMEM((t_pad, LRU_BLOCK), F32), pltpu.VMEM((t_c + t_l, LRU_BLOCK), F32)],
        compiler_params=pltpu.CompilerParams(
            dimension_semantics=("parallel", "parallel"), vmem_limit_bytes=V7X_VMEM_LIMIT_BYTES),
        name="rglru",
    )(u_c, u_c, u_l, u_l, conv_w, conv_b[None], wa, wx, ba, bx, lam)


HG_GROUP = 8


def _hgrn2_kernel(seq, qc_ref, gc_ref, ffc_ref, fbc_ref, vc_ref, ql_ref, gl_ref, ffl_ref, fbl_ref, vl_ref,
                  lbf_ref, lbb_ref, gain_ref, oc_ref, ol_ref, of_ref):
    t_c, t_l = seq
    L = HG_CHUNK
    R = HG_GROUP * L
    n_gc = t_c // L
    r_i = lax.broadcasted_iota(jnp.int32, (R, R), 0)
    c_i = lax.broadcasted_iota(jnp.int32, (R, R), 1)
    same = (r_i // L) == (c_i // L)
    causal = (same & (c_i <= r_i), same & (c_i >= r_i))
    lb = (lbf_ref[...], lbb_ref[...])
    gain = gain_ref[...]
    V = L // SUBLANES

    def chunk_cumsum(d, n, x):
        x = x.reshape(n * V, SUBLANES, HG_DK)
        row = lax.broadcasted_iota(jnp.int32, x.shape, 1)
        for s in (1, 2, 4):
            keep, shift = (row < SUBLANES - s, SUBLANES - s) if d else (row >= s, s)
            x = x + jnp.where(keep, pltpu.roll(x, shift, 1), 0.0)
        x = x.reshape(n, V, SUBLANES, HG_DK)
        edge = 0 if d else SUBLANES - 1
        tot = jnp.broadcast_to(x[:, :, edge:edge + 1, :], x.shape)
        carry = jnp.zeros((n, SUBLANES, HG_DK), F32)
        outs = [None] * V
        for j in (range(V - 1, -1, -1) if d else range(V)):
            outs[j] = x[:, j] + carry
            carry = carry + tot[:, j]
        return jnp.stack(outs, axis=1).reshape(n * L, HG_DK)

    def group(d, n, q_raw, f_raw, v, st):
        rows = n * L
        f = lb[d] + (1.0 - lb[d]) * jax.nn.sigmoid(f_raw)
        k = 1.0 - f
        g = chunk_cumsum(d, n, jnp.log(f))
        g4 = g.reshape(n, L, HG_DK)
        g_last = g4[:, 0:1] if d else g4[:, L - 1:L]
        g_rest = (jnp.broadcast_to(g_last, (n, L, HG_DK)) - g4).reshape(rows, HG_DK)
        qg = (jax.nn.silu(q_raw) * jnp.exp(g)).astype(BF16)
        kg = (k * jnp.exp(-g)).astype(BF16)
        kd = (k * jnp.exp(g_rest)).astype(BF16)
        vb = v.astype(BF16)
        att = lax.dot_general(qg, kg, (((1,), (1,)), ((), ())), preferred_element_type=F32)
        att = jnp.where(causal[d][:rows, :rows], att, 0.0).astype(BF16)
        o = jnp.dot(att, vb, preferred_element_type=F32)
        dec = jnp.exp(g_last)
        sl = [slice(j * L, (j + 1) * L) for j in range(n)]
        ds_t = [lax.dot_general(vb[s], kd[s], (((0,), (0,)), ((), ())), preferred_element_type=F32) for s in sl]
        entering = [None] * n
        for j in (range(n - 1, -1, -1) if d else range(n)):
            entering[j] = st.astype(BF16)
            st = dec[j] * st + ds_t[j]
        outs = [o[s] + lax.dot_general(qg[s], e, (((1,), (1,)), ((), ())), preferred_element_type=F32)
                for s, e in zip(sl, entering)]
        return jnp.concatenate(outs, axis=0), st

    def finish(o, gate):
        return _rms(o, gain) * jax.nn.silu(gate)

    def lat_rows(i):
        return pl.ds(pl.multiple_of(i * R, R), R), pl.ds(pl.multiple_of(t_c + i * R, math.gcd(t_c, R)), R)

    st = jnp.zeros((HG_DV, HG_DK), F32)
    o, st = group(0, n_gc, qc_ref[0], ffc_ref[0], vc_ref[0], st)
    of_ref[0:t_c, :] = o

    def fwd_body(i, st):
        rows, srows = lat_rows(i)
        o, st = group(0, HG_GROUP, ql_ref[0, rows, :], ffl_ref[0, rows, :], vl_ref[0, rows, :], st)
        of_ref[srows, :] = o
        return st

    lax.fori_loop(0, t_l // R, fwd_body, st, unroll=2)

    st = jnp.zeros((HG_DV, HG_DK), F32)
    o, st = group(1, n_gc, qc_ref[0], fbc_ref[0], vc_ref[0], st)
    oc_ref[0] = finish(of_ref[0:t_c, :] + o, gc_ref[0])

    def bwd_body(i, st):
        rows, srows = lat_rows(t_l // R - 1 - i)
        o, st = group(1, HG_GROUP, ql_ref[0, rows, :], fbl_ref[0, rows, :], vl_ref[0, rows, :], st)
        ol_ref[0, rows, :] = finish(of_ref[srows, :] + o, gl_ref[0, rows, :])
        return st

    lax.fori_loop(0, t_l // R, bwd_body, st, unroll=2)


def _hgrn2(u_c, u_l, lb_f, lb_b, onorm):
    bsz, t_c, _ = u_c.shape
    t_l = u_l.shape[1]
    nb = HG_KW // HG_DK
    col = lambda t, blk: pl.BlockSpec((1, t, HG_DK), lambda b, h: (b, 0, blk + h))
    seq_cols = lambda t: [col(t, 0), col(t, nb), col(t, EV_OUT // HG_DK), col(t, EV_OUT // HG_DK + nb), col(t, EV_OUT // HG_DK + 2 * nb)]
    vec = lambda: pl.BlockSpec((1, HG_DK), lambda b, h: (0, h))
    return pl.pallas_call(
        functools.partial(_hgrn2_kernel, (t_c, t_l)),
        grid=(bsz, HG_HEADS),
        in_specs=seq_cols(t_c) + seq_cols(t_l) + [vec(), vec(), vec()],
        out_specs=[
            pl.BlockSpec((1, t_c, HG_DV), lambda b, h: (b, 0, h)),
            pl.BlockSpec((1, t_l, HG_DV), lambda b, h: (b, 0, h)),
        ],
        out_shape=[jax.ShapeDtypeStruct((bsz, t_c, HG_VW), F32), jax.ShapeDtypeStruct((bsz, t_l, HG_VW), F32)],
        scratch_shapes=[pltpu.VMEM((t_c + t_l, HG_DV), F32)],
        compiler_params=pltpu.CompilerParams(
            dimension_semantics=("parallel", "parallel"), vmem_limit_bytes=V7X_VMEM_LIMIT_BYTES),
        name="hgrn2",
    )(*([u_c] * 5), *([u_l] * 5), lb_f[None], lb_b[None], onorm[None])


LANES = 128
MLA_HW = 2 * LANES
MLA_TM = 512
MLA_TQ = 1024
MLA_TK = 512
MLA_VW = MLA_V + LANES
MLA_SCALE_LOG2E = (MLA_NOPE + MLA_ROPE) ** -0.5 * math.log2(math.e)


def _rms(x, gain):
    return x * lax.rsqrt(jnp.mean(x * x, axis=-1, keepdims=True) + EPS) * gain


def _rope_tile(r, cos_t, sin_t):
    return r * cos_t + (pltpu.roll(r, MLA_ROPE // 2, 1) + pltpu.roll(r, LANES - MLA_ROPE // 2, 1)) * sin_t


def _mla_q_kernel(u_ref, gain_ref, w_ref, cos_ref, sin_ref, q_ref):
    q = jnp.dot(_rms(u_ref[0], gain_ref[...]).astype(BF16), w_ref[...], preferred_element_type=F32)
    for h in range(MLA_HEADS):
        lo = h * MLA_HW
        q_ref[0, :, lo:lo + LANES] = q[:, lo:lo + LANES].astype(BF16)
        q_ref[0, :, lo + LANES:lo + MLA_HW] = _rope_tile(q[:, lo + LANES:lo + MLA_HW], cos_ref[...], sin_ref[...]).astype(BF16)


def _mla_kv_kernel(u_ref, kr_ref, gain_ref, w_ref, cos_ref, sin_ref, k_ref, v_ref):
    kv = jnp.dot(_rms(u_ref[0], gain_ref[...]).astype(BF16), w_ref[...], preferred_element_type=F32)
    kr = _rope_tile(kr_ref[0], cos_ref[...], sin_ref[...]).astype(BF16)
    for h in range(MLA_HEADS):
        k_ref[0, :, h * MLA_HW:h * MLA_HW + LANES] = kv[:, h * LANES:(h + 1) * LANES].astype(BF16)
        k_ref[0, :, h * MLA_HW + LANES:(h + 1) * MLA_HW] = kr
    lane = lax.broadcasted_iota(jnp.int32, (kv.shape[0], LANES), 1)
    one_hot = jnp.where(lane == 0, 1.0, 0.0).astype(BF16)
    for h in range(MLA_HEADS):
        v_ref[0, :, h * MLA_VW:h * MLA_VW + MLA_V] = kv[:, (MLA_HEADS + h) * LANES:(MLA_HEADS + h + 1) * LANES].astype(BF16)
        v_ref[0, :, h * MLA_VW + MLA_V:(h + 1) * MLA_VW] = one_hot


def _mla_q(u, gain, w, cos_t, sin_t):
    bsz, t, _ = u.shape
    tm = min(MLA_TM, t)
    return pl.pallas_call(
        _mla_q_kernel,
        grid=(bsz, t // tm),
        in_specs=[
            pl.BlockSpec((1, tm, MLA_Q_LORA), lambda b, i: (b, i, 0)),
            pl.BlockSpec((1, MLA_Q_LORA), lambda b, i: (0, 0)),
            pl.BlockSpec((MLA_Q_LORA, MLA_HEADS * MLA_HW), lambda b, i: (0, 0)),
            pl.BlockSpec((tm, LANES), lambda b, i: (i, 0)),
            pl.BlockSpec((tm, LANES), lambda b, i: (i, 0)),
        ],
        out_specs=pl.BlockSpec((1, tm, MLA_HEADS * MLA_HW), lambda b, i: (b, i, 0)),
        out_shape=jax.ShapeDtypeStruct((bsz, t, MLA_HEADS * MLA_HW), BF16),
        compiler_params=pltpu.CompilerParams(
            dimension_semantics=("parallel", "parallel"), vmem_limit_bytes=V7X_VMEM_LIMIT_BYTES),
        name="mla_q_up",
    )(u, gain[None], w, cos_t, sin_t)


def _mla_kv(u, c_blk, r_blk, gain, w, cos_t, sin_t):
    bsz, t, _ = u.shape
    tm = min(MLA_TM, t)
    return pl.pallas_call(
        _mla_kv_kernel,
        grid=(bsz, t // tm),
        in_specs=[
            pl.BlockSpec((1, tm, MLA_KV_LORA), lambda b, i: (b, i, c_blk)),
            pl.BlockSpec((1, tm, LANES), lambda b, i: (b, i, r_blk)),
            pl.BlockSpec((1, MLA_KV_LORA), lambda b, i: (0, 0)),
            pl.BlockSpec((MLA_KV_LORA, MLA_HEADS * (LANES + MLA_V)), lambda b, i: (0, 0)),
            pl.BlockSpec((tm, LANES), lambda b, i: (i, 0)),
            pl.BlockSpec((tm, LANES), lambda b, i: (i, 0)),
        ],
        out_specs=[
            pl.BlockSpec((1, tm, MLA_HEADS * MLA_HW), lambda b, i: (b, i, 0)),
            pl.BlockSpec((1, tm, MLA_HEADS * MLA_VW), lambda b, i: (b, i, 0)),
        ],
        out_shape=[jax.ShapeDtypeStruct((bsz, t, MLA_HEADS * MLA_HW), BF16),
                   jax.ShapeDtypeStruct((bsz, t, MLA_HEADS * MLA_VW), BF16)],
        compiler_params=pltpu.CompilerParams(
            dimension_semantics=("parallel", "parallel"), vmem_limit_bytes=V7X_VMEM_LIMIT_BYTES),
        name="mla_kv_up",
    )(u, u, gain[None], w, cos_t, sin_t)


def _flash_kernel(t_l, q_ref, kc_ref, vc_ref, kl_ref, vl_ref, o_ref):
    q = q_ref[0]

    tiles = [(kc_ref, vc_ref, slice(None))] + [(kl_ref, vl_ref, slice(i * MLA_TK, (i + 1) * MLA_TK)) for i in range(t_l // MLA_TK)]

    def scores(tile):
        k_ref, _, rows = tile
        return lax.dot_general(q, k_ref[0, rows, :], (((1,), (1,)), ((), ())), preferred_element_type=F32)

    tq = q.shape[0]
    m = jnp.full((tq, 1), -jnp.inf, F32)
    acc = jnp.zeros((tq, MLA_VW), F32)
    s_next = scores(tiles[0])
    for i, (_, v_ref, rows) in enumerate(tiles):
        s = s_next * MLA_SCALE_LOG2E
        if i + 1 < len(tiles):
            s_next = scores(tiles[i + 1])
        m_new = jnp.maximum(m, jnp.max(s, axis=-1, keepdims=True))
        p = jnp.exp2(s - m_new)
        acc = jnp.exp2(m - m_new) * acc + jnp.dot(p.astype(BF16), v_ref[0, rows, :], preferred_element_type=F32)
        m = m_new
    o_ref[0] = acc[:, :MLA_V] / acc[:, MLA_V:MLA_V + 1]


def _flash(q, k_c, v_c, k_l, v_l):
    bsz, t, _ = q.shape
    t_c, t_l = k_c.shape[1], k_l.shape[1]
    return pl.pallas_call(
        functools.partial(_flash_kernel, t_l),
        grid=(bsz, MLA_HEADS, t // MLA_TQ),
        in_specs=[
            pl.BlockSpec((1, MLA_TQ, MLA_HW), lambda b, h, i: (b, i, h)),
            pl.BlockSpec((1, t_c, MLA_HW), lambda b, h, i: (b, 0, h)),
            pl.BlockSpec((1, t_c, MLA_VW), lambda b, h, i: (b, 0, h)),
            pl.BlockSpec((1, t_l, MLA_HW), lambda b, h, i: (b, 0, h)),
            pl.BlockSpec((1, t_l, MLA_VW), lambda b, h, i: (b, 0, h)),
        ],
        out_specs=pl.BlockSpec((1, MLA_TQ, MLA_V), lambda b, h, i: (b, i, h)),
        out_shape=jax.ShapeDtypeStruct((bsz, t, MLA_HEADS * MLA_V), F32),
        compiler_params=pltpu.CompilerParams(
            dimension_semantics=("parallel", "parallel", "arbitrary"), vmem_limit_bytes=V7X_VMEM_LIMIT_BYTES),
        name="mla_flash",
    )(q, k_c, v_c, k_l, v_l)


def _mla(u_c, u_l, q_norm, w_q_up, kv_norm, w_kv_up, cos, sin):
    t_c, t_l = u_c.shape[1], u_l.shape[1]
    zeros = jnp.zeros((t_l, LANES - MLA_ROPE), F32)
    cos_l = jnp.concatenate([cos, cos, zeros], axis=-1)
    sin_l = jnp.concatenate([-sin, sin, zeros], axis=-1)
    cos_c = jnp.concatenate([jnp.ones((t_c, MLA_ROPE), F32), zeros[:t_c]], axis=-1)
    sin_c = jnp.zeros((t_c, LANES), F32)
    wq = w_q_up.reshape(MLA_Q_LORA, MLA_HEADS, MLA_NOPE + MLA_ROPE)
    wq = jnp.pad(wq, ((0, 0), (0, 0), (0, MLA_HW - MLA_NOPE - MLA_ROPE))).reshape(MLA_Q_LORA, -1).astype(BF16)
    wkv = w_kv_up.reshape(MLA_KV_LORA, MLA_HEADS, MLA_NOPE + MLA_V)
    wkv = jnp.concatenate([wkv[..., :MLA_NOPE].reshape(MLA_KV_LORA, -1), wkv[..., MLA_NOPE:].reshape(MLA_KV_LORA, -1)], axis=-1).astype(BF16)
    q = _mla_q(u_l, q_norm, wq, cos_l, sin_l)
    k_l, v_l = _mla_kv(u_l, OD_OUT // MLA_KV_LORA, (OD_OUT + MLA_KV_LORA) // LANES, kv_norm, wkv, cos_l, sin_l)
    k_c, v_c = _mla_kv(u_c, 0, MLA_KV_LORA // LANES, kv_norm, wkv, cos_c, sin_c)
    return _flash(q, k_c, v_c, k_l, v_l)


OD_BLK_Z = MLA_Q_LORA // LANES
OD_BLK_CKV = OD_OUT // LANES
OD_BLK_KR = OD_BLK_CKV + MLA_KV_LORA // LANES
OD_BLK_XBC = OD_BLK_KR + 1
SSD_XBC_BLKS = SSD_XBC // LANES
SSD_ACT_BLKS = SSD_XBC_BLKS + SSD_GROUPS
SSD_GH = SSD_HEADS // SSD_GROUPS
SSD_GW = SSD_WIDTH // SSD_GROUPS
SSD_TILE = 2 * SSD_CHUNK
SSD_CONV_ROWS = 512


def _od_weight_layout(w_in, ctx_only):
    k = w_in.shape[0]
    z = lambda n: jnp.zeros((k, n), w_in.dtype)
    o = OD_OUT + MLA_KV_LORA
    xbc0 = o + MLA_ROPE
    dt0 = xbc0 + SSD_XBC
    cols = [] if ctx_only else [w_in[:, :OD_OUT]]
    cols += [w_in[:, OD_OUT:o], w_in[:, o:xbc0], z(LANES - MLA_ROPE), w_in[:, xbc0:dt0]]
    for g in range(SSD_GROUPS):
        cols += [w_in[:, dt0 + g * SSD_GH:dt0 + (g + 1) * SSD_GH],
                 w_in[:, dt0 + SSD_HEADS + g * SSD_GH:dt0 + SSD_HEADS + (g + 1) * SSD_GH], z(LANES - 2 * SSD_GH)]
    w = jnp.concatenate(cols, axis=1)
    return jnp.pad(w, ((0, 0), (0, -w.shape[1] % (10 * LANES) if ctx_only else -w.shape[1] % (8 * LANES))))


def _ssd_group_lanes(v):
    rows = [jnp.concatenate([v[0, g * SSD_GH:(g + 1) * SSD_GH], v[1, g * SSD_GH:(g + 1) * SSD_GH],
                             jnp.zeros((LANES - 2 * SSD_GH,), v.dtype)]) for g in range(SSD_GROUPS)]
    return jnp.stack(rows)


def _ssd_act_kernel(seq, xc_ref, xl_ref, cw_ref, cb_ref, o_ref, xs_ref):
    t_c, t_l = seq
    j = pl.program_id(1)

    @pl.when(j < SSD_XBC_BLKS)
    def _():
        off_c, off_l = LRU_PAD, 2 * LRU_PAD + t_c
        zeros = jnp.zeros((LRU_PAD, LANES), F32)
        xs_ref[0:LRU_PAD] = zeros
        xs_ref[off_c + t_c:off_l] = zeros
        xs_ref[off_l + t_l:off_l + t_l + LRU_PAD] = zeros
        xs_ref[off_c:off_c + t_c] = xc_ref[0]
        xs_ref[off_l:off_l + t_l] = xl_ref[0]
        cw = cw_ref[...]
        cb = cb_ref[...]

        def conv(r0, rows):
            y = cb
            for k in range(4):
                y = y + cw[k:k + 1] * xs_ref[pl.ds(r0 - 2 + k, rows), :]
            return jax.nn.silu(y)

        o_ref[0, 0:t_c, :] = conv(off_c, t_c)

        def body(i, _):
            r0 = pl.multiple_of(i * SSD_CONV_ROWS, SSD_CONV_ROWS)
            o_ref[0, pl.ds(t_c + r0, SSD_CONV_ROWS), :] = conv(off_l + r0, SSD_CONV_ROWS)
            return 0

        lax.fori_loop(0, t_l // SSD_CONV_ROWS, body, 0)

    @pl.when(j >= SSD_XBC_BLKS)
    def _():
        o_ref[0, 0:t_c, :] = jax.nn.softplus(xc_ref[0] + cb_ref[...])
        o_ref[0, t_c:t_c + t_l, :] = jax.nn.softplus(xl_ref[0] + cb_ref[...])


def _ssd_act(u_c, u_l, conv_w, conv_b, dt_bias):
    bsz, t_c, _ = u_c.shape
    t_l = u_l.shape[1]
    cw = jnp.pad(conv_w, ((0, 0), (0, SSD_GROUPS * LANES)))
    cb = jnp.concatenate([conv_b, _ssd_group_lanes(dt_bias).reshape(-1)])[None]
    ctx_blk = OD_BLK_XBC - OD_BLK_CKV
    return pl.pallas_call(
        functools.partial(_ssd_act_kernel, (t_c, t_l)),
        grid=(bsz, SSD_ACT_BLKS),
        in_specs=[
            pl.BlockSpec((1, t_c, LANES), lambda b, j: (b, 0, ctx_blk + j)),
            pl.BlockSpec((1, t_l, LANES), lambda b, j: (b, 0, OD_BLK_XBC + j)),
            pl.BlockSpec((4, LANES), lambda b, j: (0, j)),
            pl.BlockSpec((1, LANES), lambda b, j: (0, j)),
        ],
        out_specs=pl.BlockSpec((1, t_c + t_l, LANES), lambda b, j: (b, 0, j)),
        out_shape=jax.ShapeDtypeStruct((bsz, t_c + t_l, SSD_ACT_BLKS * LANES), F32),
        scratch_shapes=[pltpu.VMEM((t_c + t_l + 3 * LRU_PAD, LANES), F32)],
        compiler_params=pltpu.CompilerParams(
            dimension_semantics=("parallel", "parallel"), vmem_limit_bytes=V7X_VMEM_LIMIT_BYTES),
        name="ssd_act",
    )(u_c, u_l, cw, cb)


def _split3(x):
    hi = x.astype(BF16)
    r1 = x - hi.astype(F32)
    mid = r1.astype(BF16)
    return hi, mid, (r1 - mid.astype(F32)).astype(BF16)


def _ssd_scan_kernel(d, n_tiles, *refs):
    if d == 0:
        x_ref, b_ref, c_ref, dt_ref, alog_ref, y_ref, s_ref = refs
    else:
        x_ref, b_ref, c_ref, dt_ref, alog_ref, yf_ref, z_ref, dskip_ref, gain_ref, y_ref, s_ref = refs
    L = SSD_CHUNK
    P = SSD_HEADDIM

    @pl.when(pl.program_id(2) == 0)
    def _():
        s_ref[...] = jnp.zeros_like(s_ref)

    r_i = lax.broadcasted_iota(jnp.int32, (L, L), 0)
    c_i = lax.broadcasted_iota(jnp.int32, (L, L), 1)
    causal = (c_i >= r_i) if d else (c_i <= r_i)
    tri = jnp.where(causal, 1.0, 0.0).astype(BF16)
    low_half = lax.broadcasted_iota(jnp.int32, (L, LANES), 1) < P
    a_lane = -jnp.exp(alog_ref[...])

    def pair_cols(cols, p):
        return jnp.where(low_half, cols[2 * p], cols[2 * p + 1])

    for ch in ((1, 0) if d else (0, 1)):
        rows = slice(ch * L, (ch + 1) * L)
        xs = x_ref[0, rows, :]
        bm = b_ref[0, rows, :].astype(BF16)
        cm = c_ref[0, rows, :].astype(BF16)
        dt = dt_ref[0, rows, :]
        a3 = _split3(dt * a_lane)
        cs3 = jnp.dot(tri, jnp.concatenate(a3, axis=1), preferred_element_type=F32)
        cs_col = cs3[:, :LANES] + cs3[:, LANES:2 * LANES] + cs3[:, 2 * LANES:]
        cs_row = sum(lax.dot_general(t, tri, (((0,), (1,)), ((), ())), preferred_element_type=F32) for t in a3)
        cb = lax.dot_general(cm, bm, (((1,), (1,)), ((), ())), preferred_element_type=F32)
        s_in = s_ref[...].astype(BF16)
        y_off = lax.dot_general(cm, s_in, (((1,), (1,)), ((), ())), preferred_element_type=F32)
        last = 0 if d else L - 1
        cols, dts, lasts = [], [], []
        for i in range(SSD_GH):
            lane = SSD_GH * d + i
            cols.append(cs_col[:, lane:lane + 1])
            dts.append(dt[:, lane:lane + 1])
            lasts.append(cs_col[last:last + 1, lane:lane + 1])
        ys, xdecs = [], []
        for p in range(SSD_GH // 2):
            xp = xs[:, p * LANES:(p + 1) * LANES]
            col2 = pair_cols(cols, p)
            xdt = xp * pair_cols(dts, p)
            xdt_b = xdt.astype(BF16)
            y2 = []
            for e in range(2):
                i = 2 * p + e
                lane = SSD_GH * d + i
                decay = jnp.where(causal, jnp.exp(cols[i] - cs_row[lane:lane + 1, :]), 0.0)
                y2.append(jnp.dot((cb * decay).astype(BF16), xdt_b, preferred_element_type=F32))
            ys.append(jnp.where(low_half, y2[0], y2[1]) + y_off[:, p * LANES:(p + 1) * LANES] * jnp.exp(col2))
            last2 = jnp.where(low_half, lasts[2 * p], lasts[2 * p + 1])
            xdecs.append((xdt * jnp.exp(last2 - col2)).astype(BF16))
        y = jnp.concatenate(ys, axis=1)
        ds = lax.dot_general(jnp.concatenate(xdecs, axis=1), bm, (((0,), (0,)), ((), ())), preferred_element_type=F32)
        dec = jnp.concatenate([jnp.broadcast_to(jnp.exp(lasts[i]), (P, SSD_STATE)) for i in range(SSD_GH)], axis=0)
        s_ref[...] = dec * s_ref[...] + ds
        if d == 0:
            y_ref[0, rows, :] = y
        else:
            out = (yf_ref[0, rows, :] + y + dskip_ref[...] * xs) * jax.nn.silu(z_ref[0, rows, :])
            y_ref[0, rows, :] = _rms(out, gain_ref[...])


def _ssd_scan(d, act, t_c, a_lanes, extra=()):
    bsz, t, _ = act.shape
    n_tiles = t // SSD_TILE
    n_ctx = t_c // SSD_TILE
    assert n_ctx == 1
    if d == 0:
        tile = lambda i: i
    else:
        tile = lambda i: jnp.where(i < n_ctx, n_ctx - 1 - i, n_tiles + n_ctx - 1 - i)
    in_specs = [
        pl.BlockSpec((1, SSD_TILE, SSD_GW), lambda b, g, i: (b, tile(i), g)),
        pl.BlockSpec((1, SSD_TILE, LANES), lambda b, g, i: (b, tile(i), SSD_WIDTH // LANES + g)),
        pl.BlockSpec((1, SSD_TILE, LANES), lambda b, g, i: (b, tile(i), (SSD_WIDTH + SSD_BC) // LANES + g)),
        pl.BlockSpec((1, SSD_TILE, LANES), lambda b, g, i: (b, tile(i), SSD_XBC_BLKS + g)),
        pl.BlockSpec((1, LANES), lambda b, g, i: (0, g)),
    ]
    args = [act, act, act, act, a_lanes.reshape(1, -1)]
    if d:
        y_f, u_l, d_skip, onorm = extra
        in_specs += [
            pl.BlockSpec((1, SSD_TILE, SSD_GW), lambda b, g, i: (b, tile(i), g)),
            pl.BlockSpec((1, SSD_TILE, SSD_GW), lambda b, g, i: (b, jnp.maximum(tile(i) - n_ctx, 0), OD_BLK_Z * LANES // SSD_GW + g)),
            pl.BlockSpec((1, SSD_GW), lambda b, g, i: (0, g)),
            pl.BlockSpec((1, SSD_GW), lambda b, g, i: (0, g)),
        ]
        args += [y_f, u_l, jnp.repeat(d_skip, SSD_HEADDIM)[None], onorm[None]]
        out_tile = lambda i: jnp.where(i < n_ctx, n_tiles - n_ctx - 1, tile(i) - n_ctx)
        out_rows = t - t_c
    else:
        out_tile, out_rows = tile, t
    return pl.pallas_call(
        functools.partial(_ssd_scan_kernel, d, n_tiles),
        grid=(bsz, SSD_GROUPS, n_tiles),
        in_specs=in_specs,
        out_specs=pl.BlockSpec((1, SSD_TILE, SSD_GW), lambda b, g, i: (b, out_tile(i), g)),
        out_shape=jax.ShapeDtypeStruct((bsz, out_rows, SSD_WIDTH), F32),
        scratch_shapes=[pltpu.VMEM((SSD_GW, SSD_STATE), F32)],
        compiler_params=pltpu.CompilerParams(
            dimension_semantics=("parallel", "parallel", "arbitrary"), vmem_limit_bytes=V7X_VMEM_LIMIT_BYTES),
        name="ssd_scan_bwd" if d else "ssd_scan_fwd",
    )(*args)


def _ssd(u_c, u_l, conv_w, conv_b, dt_bias, a_log, d_skip, onorm):
    act = _ssd_act(u_c, u_l, conv_w, conv_b, dt_bias)
    a_lanes = _ssd_group_lanes(a_log)
    y_f = _ssd_scan(0, act, u_c.shape[1], a_lanes)
    return _ssd_scan(1, act, u_c.shape[1], a_lanes, (y_f, u_l, d_skip, onorm))


def axial_rope(rows):
    row = jnp.repeat(jnp.arange(rows, dtype=F32), GRID_W)
    col = jnp.arange(rows * GRID_W) % GRID_W
    n_freq = MLA_ROPE // 4
    inv = ROPE_THETA ** (-jnp.arange(n_freq, dtype=F32) / n_freq)
    ang = jnp.concatenate([row[:, None] * inv, col.astype(F32)[:, None] * inv], axis=-1)
    return jnp.cos(ang), jnp.sin(ang)


def even_mixer(u_c, u_l, lb_f, lb_b, onorm, conv_w, conv_b, wa, ba, wx, bx, lam, need_ctx):
    assert need_ctx
    lru_c, lru_l = _rglru(u_c, u_l, conv_w, conv_b, wa, ba, wx, bx, lam)
    hg_c, hg_l = _hgrn2(u_c, u_l, lb_f, lb_b, onorm)
    return (hg_c, lru_c), (hg_l, lru_l)


def odd_mixer(u_c, u_l, q_norm, w_q_up, kv_norm, w_kv_up, conv_w, conv_b, dt_bias, a_log, d_skip, onorm, cos, sin, need_ctx):
    assert not need_ctx
    att_l = _mla(u_c, u_l, q_norm, w_q_up, kv_norm, w_kv_up, cos, sin)
    ssd_l = _ssd(u_c, u_l, conv_w, conv_b, dt_bias, a_log, d_skip, onorm)
    return None, (att_l, ssd_l)


def ec_moe(x, gain, sc, sh, g2, w_router, w_gate, w_up, w_down, layer, merge_sets):
    b, t, d = x.shape
    cap = CAPACITY_FACTOR * t // N_EXPERTS
    h, aff = _router(x, gain, sc, sh, w_router)
    gate, idx = lax.top_k(aff, cap)
    bidx = jnp.arange(b)[:, None, None]
    xe = h[bidx, idx]
    if merge_sets:
        xe = jnp.swapaxes(xe, 0, 1).reshape(1, N_EXPERTS, b * cap, d)
        gate_m = jnp.swapaxes(gate, 0, 1).reshape(1, N_EXPERTS, b * cap)
        ye = _expert_ffn(xe, gate_m, jnp.repeat(g2, cap, axis=0)[None], w_gate, w_up, w_down, layer)
        ye = jnp.swapaxes(ye.reshape(N_EXPERTS, b, cap, d), 0, 1)
    else:
        ye = _expert_ffn(xe, gate, g2[:, None], w_gate, w_up, w_down, layer)
    return x.at[bidx, idx].add(ye)


def kernel(x, c, ctx, c_ctx, w_mod, b_mod, norm_mix, norm_ffn, w_out, ev_w_in, hg_lb, hg_onorm, lru_conv_w, lru_conv_b, lru_wa, lru_ba, lru_wx, lru_bx, lru_lambda, od_w_in, mla_q_norm, mla_w_q_up, mla_kv_norm, mla_w_kv_up, ssd_conv_w, ssd_conv_b, ssd_dt_bias, ssd_a_log, ssd_d, ssd_onorm, moe_router, moe_w_gate, moe_w_up, moe_w_down, final_norm):
    rows = x.shape[1] // GRID_W
    cos, sin = axial_rope(rows)
    lb_all = jnp.cumsum(jax.nn.softmax(hg_lb.astype(F32), axis=1), axis=1)
    bsz = x.shape[0]
    c_rows = jnp.concatenate([c, c_ctx[None], jnp.zeros((MOD_ROWS - bsz - 1, D_MODEL), F32)], axis=0)
    wg, wu, wd = moe_w_gate.astype(BF16), moe_w_up.astype(BF16), moe_w_down.astype(BF16)
    for l in range(DEPTH):
        need_ctx = l < DEPTH - 1
        j = l // 2
        mod_all = _modulation(c_rows, w_mod, b_mod, l)
        mod, mod_c = mod_all[:bsz], mod_all[bsz]
        sh1, sc1, g1, sh2, sc2, g2 = jnp.split(mod, 6, axis=-1)
        csh1, csc1, cg1, csh2, csc2, cg2 = (jnp.broadcast_to(v, (x.shape[0], D_MODEL)) for v in jnp.split(mod_c, 6, axis=-1))
        wo = w_out[l].astype(BF16)
        if l % 2 == 0:
            w_in = ev_w_in[j].astype(BF16)
            u_c = _proj_in(ctx, norm_mix[l], csc1, csh1, w_in)
            u_l = _proj_in(x, norm_mix[l], sc1, sh1, w_in)
            o_c, o_l = even_mixer(u_c, u_l, lb_all[0, j], lb_all[1, j], hg_onorm[j], lru_conv_w[j], lru_conv_b[j], lru_wa[j], lru_ba[j], lru_wx[j], lru_bx[j], lru_lambda[j], need_ctx)
        else:
            w_in = od_w_in[j]
            u_c = _proj_in(ctx, norm_mix[l], csc1, csh1, _od_weight_layout(w_in, True).astype(BF16))
            u_l = _proj_in(x, norm_mix[l], sc1, sh1, _od_weight_layout(w_in, False).astype(BF16))
            o_c, o_l = odd_mixer(u_c, u_l, mla_q_norm[j], mla_w_q_up[j], mla_kv_norm[j], mla_w_kv_up[j], ssd_conv_w[j], ssd_conv_b[j], ssd_dt_bias[j], ssd_a_log[j], ssd_d[j], ssd_onorm[j], cos, sin, need_ctx)
        x = _proj_out(*o_l, wo, x, g1)
        x = ec_moe(x, norm_ffn[l], sc2, sh2, g2, moe_router[l], wg, wu, wd, l, False)
        if need_ctx:
            ctx = _proj_out(*o_c, wo, ctx, cg1)
            ctx = ec_moe(ctx, norm_ffn[l], csc2, csh2, cg2, moe_router[l], wg, wu, wd, l, True)
    return _final_norm(x, final_norm)
```
